```python
import math
import jax, jax.numpy as jnp
from jax import lax
import numpy as np

D_MODEL = 1024
BATCH = 8
SEQ = 2048
DEPTH = 1

PLE_DIM = 256
D_MIX = D_MODEL
D_GMLP = D_MIX // 2
GMLP_GROUPS = 4
GMLP_GROUP_DIM = D_GMLP // GMLP_GROUPS
CHUNK = 128
D_ATTN = D_MIX - D_GMLP
HEAD_DIM = 64
N_Q_HEADS = D_ATTN // HEAD_DIM
N_KV_HEADS = 2
Q_PER_KV = N_Q_HEADS // N_KV_HEADS
WINDOW = 128
BLOCK = WINDOW
REL_BUCKETS = 32
REL_MAX_DIST = 128
D_FF = 4 * D_MODEL
D_IN = 2 * D_GMLP + D_ATTN + 2 * N_KV_HEADS * HEAD_DIM
EPS = 1e-6
NEG_INF = -1e30

kernel_name = "hybrid_gmlp_swa_sink_block"


def rmsnorm(x, g):
    xf = x.astype(jnp.float32)
    y = xf * lax.rsqrt(jnp.mean(xf * xf, axis=-1, keepdims=True) + EPS)
    return (y * g.astype(jnp.float32)).astype(x.dtype)


def t5_causal_bucket(n):
    max_exact = REL_BUCKETS // 2
    nf = jnp.maximum(n, 1).astype(jnp.float32)
    large = max_exact + (jnp.log(nf / max_exact) / math.log(REL_MAX_DIST / max_exact)
                         * (REL_BUCKETS - max_exact)).astype(jnp.int32)
    large = jnp.minimum(large, REL_BUCKETS - 1)
    return jnp.where(n < max_exact, n, large)


def band_bias_and_mask(rel_table):
    a = jnp.arange(BLOCK)[:, None]
    j = jnp.arange(2 * BLOCK)[None, :]
    n = BLOCK + a - j
    valid = (n >= 0) & (n < WINDOW)
    bucket = t5_causal_bucket(jnp.maximum(n, 0))
    bias = jnp.transpose(rel_table[bucket], (2, 0, 1))
    return bias, valid


def gmlp_chunk_mixer(u, v, v_gain, w_spatial, b_spatial):
    B, S, _ = u.shape
    nc = S // CHUNK
    vg = v.reshape(B, S, GMLP_GROUPS, GMLP_GROUP_DIM)
    vg = rmsnorm(vg, v_gain.reshape(GMLP_GROUPS, GMLP_GROUP_DIM))
    vc = vg.reshape(B, nc, CHUNK, GMLP_GROUPS, GMLP_GROUP_DIM)
    causal = jnp.tril(jnp.ones((CHUNK, CHUNK), dtype=bool))
    w = jnp.where(causal[None], w_spatial, 0).astype(vc.dtype)
    sv = jnp.einsum('gts,bnsgd->bntgd', w, vc) + jnp.transpose(b_spatial)[:, :, None].astype(vc.dtype)
    return u * sv.reshape(B, S, D_GMLP)


def swa_sink_attention(q, k, v, sinks, rel_table):
    B, S, _ = q.shape
    nb = S // BLOCK
    qb = q.reshape(B, nb, BLOCK, N_KV_HEADS, Q_PER_KV, HEAD_DIM)

    def band(t):
        t = t.reshape(B, S, N_KV_HEADS, HEAD_DIM)
        t = jnp.pad(t, ((0, 0), (BLOCK, 0), (0, 0), (0, 0)))
        t = t.reshape(B, nb + 1, BLOCK, N_KV_HEADS, HEAD_DIM)
        return jnp.concatenate([t[:, :-1], t[:, 1:]], axis=2)

    kb, vb = band(k), band(v)
    logits = jnp.einsum('bnqkgd,bnskd->bkgnqs', qb, kb).astype(jnp.float32) * (HEAD_DIM ** -0.5)
    bias, valid = band_bias_and_mask(rel_table)
    bias = bias.astype(jnp.float32).reshape(N_KV_HEADS, Q_PER_KV, 1, BLOCK, 2 * BLOCK)
    first = (jnp.arange(nb)[:, None, None] == 0) & (jnp.arange(2 * BLOCK)[None, None, :] < BLOCK)
    mask = valid[None] & ~first
    logits = jnp.where(mask, logits + bias, NEG_INF)
    sink = sinks.astype(jnp.float32).reshape(N_KV_HEADS, Q_PER_KV, 1, 1, 1)
    m = jnp.maximum(jnp.max(logits, axis=-1, keepdims=True), sink)
    e = jnp.exp(logits - m)
    denom = jnp.sum(e, axis=-1, keepdims=True) + jnp.exp(sink - m)
    probs = (e / denom).astype(v.dtype)
    out = jnp.einsum('bkgnqs,bnskd->bnqkgd', probs, vb)
    return out.reshape(B, S, D_ATTN)


def setup_inputs(seed: int = 0) -> dict:
    key = jax.random.key(seed)
    ks = jax.random.split(key, 16)
    f32 = jnp.float32
    nrm = lambda k, shape, s: jax.random.normal(k, shape, f32) * s
    return {
        "x": nrm(ks[0], (BATCH, SEQ, D_MODEL), 1.0),
        "p": nrm(ks[1], (DEPTH, BATCH, SEQ, PLE_DIM), 1.0),
        "norm1_gain": 1.0 + nrm(ks[2], (DEPTH, D_MODEL), 0.02),
        "w_in": nrm(ks[3], (DEPTH, D_MODEL, D_IN), D_MODEL ** -0.5),
        "gmlp_v_gain": 1.0 + nrm(ks[4], (DEPTH, D_GMLP), 0.02),
        "w_spatial": nrm(ks[5], (DEPTH, GMLP_GROUPS, CHUNK, CHUNK), CHUNK ** -0.5),
        "b_spatial": 1.0 + nrm(ks[6], (DEPTH, GMLP_GROUPS, CHUNK), 0.01),
        "attn_sinks": nrm(ks[7], (DEPTH, N_Q_HEADS), 0.5),
        "rel_bias_table": nrm(ks[8], (REL_BUCKETS, N_Q_HEADS), 0.5),
        "w_out": nrm(ks[9], (DEPTH, D_MIX, D_MODEL), D_MIX ** -0.5),
        "norm2_gain": 1.0 + nrm(ks[10], (DEPTH, D_MODEL), 0.02),
        "w_ff1": nrm(ks[11], (DEPTH, D_MODEL, D_FF), D_MODEL ** -0.5),
        "w_ff2": nrm(ks[12], (DEPTH, D_FF, D_MODEL), D_FF ** -0.5),
        "w_ple_proj": nrm(ks[13], (DEPTH, PLE_DIM, D_MODEL), PLE_DIM ** -0.5),
        "w_ple_gate": nrm(ks[14], (DEPTH, D_MODEL, D_MODEL), D_MODEL ** -0.5),
        "final_gain": 1.0 + nrm(ks[15], (D_MODEL,), 0.02),
    }


def reference(x, p, norm1_gain, w_in, gmlp_v_gain, w_spatial, b_spatial, attn_sinks,
              rel_bias_table, w_out, norm2_gain, w_ff1, w_ff2, w_ple_proj, w_ple_gate,
              final_gain):
    h = x
    for i in range(DEPTH):
        hn = rmsnorm(h, norm1_gain[i])
        z = hn @ w_in[i]
        zg = jax.nn.gelu(z[..., :2 * D_GMLP])
        u, vg = zg[..., :D_GMLP], zg[..., D_GMLP:]
        o = 2 * D_GMLP
        q = z[..., o:o + D_ATTN]
        kv_w = N_KV_HEADS * HEAD_DIM
        k = z[..., o + D_ATTN:o + D_ATTN + kv_w]
        v = z[..., o + D_ATTN + kv_w:]
        a_out = gmlp_chunk_mixer(u, vg, gmlp_v_gain[i], w_spatial[i], b_spatial[i])
        b_out = swa_sink_attention(q, k, v, attn_sinks[i], rel_bias_table)
        h = h + jnp.concatenate([a_out, b_out], axis=-1) @ w_out[i]
        hn = rmsnorm(h, norm2_gain[i])
        h = h + jnp.square(jax.nn.relu(hn @ w_ff1[i])) @ w_ff2[i]
        gate = jax.nn.sigmoid(h @ w_ple_gate[i])
        h = h + gate * (p[i] @ w_ple_proj[i])
    return rmsnorm(h, final_gain)
```

```python
import functools
import math

import jax
import jax.numpy as jnp
import numpy as np
from jax import lax
from jax.experimental import pallas as pl
from jax.experimental.pallas import tpu as pltpu

D_MODEL = 1024
PLE_DIM = 256
D_GMLP = 512
GMLP_GROUPS = 4
GROUP_DIM = D_GMLP // GMLP_GROUPS
CHUNK = 128
D_ATTN = 512
HEAD_DIM = 64
N_Q_HEADS = D_ATTN // HEAD_DIM
N_KV_HEADS = 2
Q_PER_KV = N_Q_HEADS // N_KV_HEADS
WINDOW = 128
BLOCK = WINDOW
REL_BUCKETS = 32
REL_MAX_DIST = 128
D_FF = 4 * D_MODEL
KV_W = N_KV_HEADS * HEAD_DIM
D_IN = 2 * D_GMLP + D_ATTN + 2 * KV_W
EPS = 1e-6
NEG_INF = -1e30

LANES = 128
SEQ_TILE = 512
ROW_TILE = 512
FF_CHUNK = 1024
VMEM_LIMIT_BYTES = 56 * 1024 * 1024


def _bucket_thresholds():
    max_exact = REL_BUCKETS // 2
    n = np.arange(WINDOW)
    nf = np.maximum(n, 1).astype(np.float32)
    large = max_exact + (np.log(nf / max_exact) / np.float32(math.log(REL_MAX_DIST / max_exact))
                         * (REL_BUCKETS - max_exact)).astype(np.int32)
    bucket = np.where(n < max_exact, n, np.minimum(large, REL_BUCKETS - 1))
    assert (np.diff(bucket) >= 0).all()
    return [(b, int(np.argmax(bucket >= b))) for b in range(1, REL_BUCKETS) if (bucket >= b).any()]


_BUCKET_THRESHOLDS = _bucket_thresholds()


def _rms(x, gain):
    return x * lax.rsqrt(jnp.mean(x * x, axis=-1, keepdims=True) + EPS) * gain


def _gelu_tanh(x):
    c = np.float32(np.sqrt(2 / np.pi))
    return x * (0.5 * (1.0 + jnp.tanh(c * (x + 0.044715 * (x * x * x)))))


def _dot(a, b):
    return jnp.dot(a, b, preferred_element_type=jnp.float32)


def _dot_nt(a, b):
    return lax.dot_general(a, b, (((1,), (1,)), ((), ())), preferred_element_type=jnp.float32)


def _build_bias(rel_ref, bias_ref):
    a = lax.broadcasted_iota(jnp.int32, (BLOCK, 2 * BLOCK), 0)
    j = lax.broadcasted_iota(jnp.int32, (BLOCK, 2 * BLOCK), 1)
    n = BLOCK + a - j
    valid = (n >= 0) & (n < WINDOW)
    valid_first = valid & (j >= BLOCK)

    def per_head(h, carry):
        val = jnp.full((BLOCK, 2 * BLOCK), rel_ref[0, h], jnp.float32)
        for b, thr in _BUCKET_THRESHOLDS:
            val = jnp.where(n >= thr, rel_ref[b, h], val)
        stack = 2 * (h // Q_PER_KV) + h % 2
        half = (h % Q_PER_KV) // 2
        bias_ref[0, stack, half] = jnp.where(valid, val, NEG_INF)
        bias_ref[1, stack, half] = jnp.where(valid_first, val, NEG_INF)
        return carry

    lax.fori_loop(0, N_Q_HEADS, per_head, 0)


def _mixer_kernel(x_ref, g1_ref, w_in_ref, vgain_ref, wsp_ref, bsp_ref, sink_ref, rel_ref,
                  w_out_ref, o_ref, kbuf, vbuf, mix_ref, bias_ref):
    ts = x_ref.shape[1]
    n_blk = ts // BLOCK
    first_tile = pl.program_id(1) == 0

    @pl.when((pl.program_id(0) == 0) & first_tile)
    def _():
        _build_bias(rel_ref, bias_ref)

    @pl.when(first_tile)
    def _():
        kbuf[:, 0:BLOCK, :] = jnp.zeros((4, BLOCK, LANES), kbuf.dtype)
        vbuf[:, 0:BLOCK, :] = jnp.zeros((4, BLOCK, LANES), vbuf.dtype)

    x = x_ref[0]
    hn = _rms(x, g1_ref[...]).astype(jnp.bfloat16)

    u = _gelu_tanh(_dot(hn, w_in_ref[:, 0:D_GMLP]))
    vg = _gelu_tanh(_dot(hn, w_in_ref[:, D_GMLP:2 * D_GMLP]))
    t_idx = lax.broadcasted_iota(jnp.int32, (CHUNK, CHUNK), 0)
    s_idx = lax.broadcasted_iota(jnp.int32, (CHUNK, CHUNK), 1)
    causal = s_idx <= t_idx
    for g in range(GMLP_GROUPS):
        cols = slice(g * GROUP_DIM, (g + 1) * GROUP_DIM)
        vn = _rms(vg[:, cols], vgain_ref[:, cols]).astype(jnp.bfloat16)
        w_g = jnp.where(causal, wsp_ref[g], 0.0).astype(jnp.bfloat16)
        b_g = bsp_ref[:, g:g + 1]
        for c in range(ts // CHUNK):
            rows = slice(c * CHUNK, (c + 1) * CHUNK)
            sv = _dot(w_g, vn[rows]) + b_g
            mix_ref[rows, cols] = (u[rows, cols] * sv).astype(mix_ref.dtype)

    o = 2 * D_GMLP
    q = (_dot(hn, w_in_ref[:, o:o + D_ATTN]) * (HEAD_DIM ** -0.5)).astype(jnp.bfloat16)
    kv = _dot(hn, w_in_ref[:, o + D_ATTN:o + D_ATTN + 2 * KV_W])
    lo = lax.broadcasted_iota(jnp.int32, (ts, LANES), 1) < HEAD_DIM
    for buf, t in ((kbuf, kv[:, 0:KV_W]), (vbuf, kv[:, KV_W:2 * KV_W])):
        tr = pltpu.roll(t, HEAD_DIM, axis=1)
        buf[0, BLOCK:BLOCK + ts, :] = jnp.where(lo, t, 0.0).astype(buf.dtype)
        buf[1, BLOCK:BLOCK + ts, :] = jnp.where(lo, 0.0, tr).astype(buf.dtype)
        buf[2, BLOCK:BLOCK + ts, :] = jnp.where(lo, tr, 0.0).astype(buf.dtype)
        buf[3, BLOCK:BLOCK + ts, :] = jnp.where(lo, 0.0, t).astype(buf.dtype)
    variant = ((0, 1), (2, 3))

    upper = lax.broadcasted_iota(jnp.int32, (2 * BLOCK, 1), 0) < BLOCK
    for i in range(n_blk):
        rows = slice(i * BLOCK, (i + 1) * BLOCK)
        keys = slice(i * BLOCK, (i + 2) * BLOCK)
        first = jnp.where(first_tile, 1, 0) if i == 0 else 0
        for g in range(N_KV_HEADS):
            pair0 = D_GMLP + 2 * g * LANES
            qs = jnp.concatenate([q[rows, 2 * g * LANES:(2 * g + 1) * LANES],
                                  q[rows, (2 * g + 1) * LANES:(2 * g + 2) * LANES]], axis=0)
            acc = jnp.zeros((2 * BLOCK, LANES), jnp.float32)
            for parity in range(2):
                head = Q_PER_KV * g + parity
                sink = jnp.where(upper, sink_ref[head], sink_ref[head + 2])
                s = _dot_nt(qs, kbuf[variant[g][parity], keys, :])
                s = s + bias_ref[first, 2 * g + parity].reshape(2 * BLOCK, 2 * BLOCK)
                m = jnp.maximum(jnp.max(s, axis=-1, keepdims=True), sink)
                e = jnp.exp(s - m)
                denom = jnp.sum(e, axis=-1, keepdims=True) + jnp.exp(sink - m)
                p = (e / denom).astype(jnp.bfloat16)
                acc = acc + _dot(p, vbuf[variant[g][parity], keys, :])
            mix_ref[rows, pair0:pair0 + LANES] = acc[0:BLOCK].astype(mix_ref.dtype)
            mix_ref[rows, pair0 + LANES:pair0 + 2 * LANES] = acc[BLOCK:].astype(mix_ref.dtype)

    kbuf[:, 0:BLOCK, :] = kbuf[:, ts:ts + BLOCK, :]
    vbuf[:, 0:BLOCK, :] = vbuf[:, ts:ts + BLOCK, :]

    o_ref[0] = x + _dot(mix_ref[...], w_out_ref[...])


def _ffn_kernel(h_ref, p_ref, g2_ref, w1_ref, w2_ref, wproj_ref, wgate_ref, gf_ref, o_ref, *,
                final_norm):
    h = h_ref[...]
    hn = _rms(h, g2_ref[...]).astype(jnp.bfloat16)
    acc = h
    for c in range(D_FF // FF_CHUNK):
        cols = slice(c * FF_CHUNK, (c + 1) * FF_CHUNK)
        t = jnp.maximum(_dot(hn, w1_ref[:, cols]), 0.0)
        acc = acc + _dot((t * t).astype(jnp.bfloat16), w2_ref[cols, :])
    gate = 1.0 / (1.0 + jnp.exp(-_dot(acc.astype(jnp.bfloat16), wgate_ref[...])))
    ple = _dot(p_ref[...].astype(jnp.bfloat16), wproj_ref[...])
    out = acc + gate * ple
    o_ref[...] = _rms(out, gf_ref[...]) if final_norm else out


def _resident(shape):
    return pl.BlockSpec(shape, lambda *_: (0,) * len(shape), pipeline_mode=pl.Buffered(1))


def _mixer(x, g1, w_in, vgain, wsp, bsp_t, sinks, rel, w_out):
    batch, seq, d = x.shape
    ts = SEQ_TILE
    smem = pl.BlockSpec(memory_space=pltpu.SMEM)
    return pl.pallas_call(
        _mixer_kernel,
        grid=(batch, seq // ts),
        in_specs=[
            pl.BlockSpec((1, ts, d), lambda b, j: (b, j, 0)),
            _resident(g1.shape), _resident(w_in.shape), _resident(vgain.shape),
            _resident(wsp.shape), _resident(bsp_t.shape), smem, smem,
            _resident(w_out.shape),
        ],
        out_specs=pl.BlockSpec((1, ts, d), lambda b, j: (b, j, 0)),
        out_shape=jax.ShapeDtypeStruct(x.shape, x.dtype),
        scratch_shapes=[
            pltpu.VMEM((4, BLOCK + ts, LANES), jnp.bfloat16),
            pltpu.VMEM((4, BLOCK + ts, LANES), jnp.bfloat16),
            pltpu.VMEM((ts, D_MODEL), jnp.bfloat16),
            pltpu.VMEM((2, 4, 2, BLOCK, 2 * BLOCK), jnp.float32),
        ],
        compiler_params=pltpu.CompilerParams(
            dimension_semantics=("arbitrary", "arbitrary"),
            vmem_limit_bytes=VMEM_LIMIT_BYTES),
        name="mixer",
    )(x, g1, w_in, vgain, wsp, bsp_t, sinks, rel, w_out)


def _ffn(h, p, g2, w1, w2, wproj, wgate, gf, *, final_norm):
    rows, d = h.shape
    tm = ROW_TILE
    return pl.pallas_call(
        functools.partial(_ffn_kernel, final_norm=final_norm),
        grid=(rows // tm,),
        in_specs=[
            pl.BlockSpec((tm, d), lambda i: (i, 0)),
            pl.BlockSpec((tm, PLE_DIM), lambda i: (i, 0)),
            _resident(g2.shape), _resident(w1.shape), _resident(w2.shape),
            _resident(wproj.shape), _resident(wgate.shape), _resident(gf.shape),
        ],
        out_specs=pl.BlockSpec((tm, d), lambda i: (i, 0)),
        out_shape=jax.ShapeDtypeStruct(h.shape, h.dtype),
        compiler_params=pltpu.CompilerParams(
            dimension_semantics=("arbitrary",),
            vmem_limit_bytes=VMEM_LIMIT_BYTES),
        name="ffn",
    )(h, p, g2, w1, w2, wproj, wgate, gf)


def kernel(x, p, norm1_gain, w_in, gmlp_v_gain, w_spatial, b_spatial, attn_sinks, rel_bias_table, w_out, norm2_gain, w_ff1, w_ff2, w_ple_proj, w_ple_gate, final_gain):
    batch, seq, d = x.shape
    depth = w_in.shape[0]
    bf16 = jnp.bfloat16
    h = x
    for i in range(depth):
        h = _mixer(h, norm1_gain[i][None], w_in[i].astype(bf16), gmlp_v_gain[i][None],
                   w_spatial[i], b_spatial[i].T, attn_sinks[i], rel_bias_table,
                   w_out[i].astype(bf16))
        h = _ffn(h.reshape(batch * seq, d), p[i].reshape(batch * seq, PLE_DIM),
                 norm2_gain[i][None], w_ff1[i].astype(bf16), w_ff2[i].astype(bf16),
                 w_ple_proj[i].astype(bf16), w_ple_gate[i].astype(bf16),
                 final_gain[None], final_norm=(i == depth - 1)).reshape(batch, seq, d)
    return h
```

```python
import functools
import math

import jax
import jax.numpy as jnp
import numpy as np
from jax import lax
from jax.experimental import pallas as pl
from jax.experimental.pallas import tpu as pltpu

D_MODEL = 1024
PLE_DIM = 256
D_GMLP = 512
GMLP_GROUPS = 4
GROUP_DIM = D_GMLP // GMLP_GROUPS
CHUNK = 128
D_ATTN = 512
HEAD_DIM = 64
N_Q_HEADS = D_ATTN // HEAD_DIM
N_KV_HEADS = 2
Q_PER_KV = N_Q_HEADS // N_KV_HEADS
WINDOW = 128
BLOCK = WINDOW
REL_BUCKETS = 32
REL_MAX_DIST = 128
D_FF = 4 * D_MODEL
KV_W = N_KV_HEADS * HEAD_DIM
D_IN = 2 * D_GMLP + D_ATTN + 2 * KV_W
EPS = 1e-6
NEG_INF = -1e30

LANES = 128
SEQ_TILE = 512
ROW_TILE = 512
FF_CHUNK = 1024
VMEM_LIMIT_BYTES = 56 * 1024 * 1024


def _bucket_thresholds():
    max_exact = REL_BUCKETS // 2
    n = np.arange(WINDOW)
    nf = np.maximum(n, 1).astype(np.float32)
    large = max_exact + (np.log(nf / max_exact) / np.float32(math.log(REL_MAX_DIST / max_exact))
                         * (REL_BUCKETS - max_exact)).astype(np.int32)
    bucket = np.where(n < max_exact, n, np.minimum(large, REL_BUCKETS - 1))
    assert (np.diff(bucket) >= 0).all()
    return [(b, int(np.argmax(bucket >= b))) for b in range(1, REL_BUCKETS) if (bucket >= b).any()]


_BUCKET_THRESHOLDS = _bucket_thresholds()


def _rms(x, gain):
    return x * lax.rsqrt(jnp.mean(x * x, axis=-1, keepdims=True) + EPS) * gain


def _gelu_tanh(x):
    c = np.float32(np.sqrt(2 / np.pi))
    return x * (0.5 * (1.0 + jnp.tanh(c * (x + 0.044715 * (x * x * x)))))


def _dot(a, b):
    return jnp.dot(a, b, preferred_element_type=jnp.float32)


def _dot_nt(a, b):
    return lax.dot_general(a, b, (((1,), (1,)), ((), ())), preferred_element_type=jnp.float32)


def _build_bias(rel_ref, bias_ref):
    a = lax.broadcasted_iota(jnp.int32, (BLOCK, 2 * BLOCK), 0)
    j = lax.broadcasted_iota(jnp.int32, (BLOCK, 2 * BLOCK), 1)
    n = BLOCK + a - j
    valid = (n >= 0) & (n < WINDOW)
    valid_first = valid & (j >= BLOCK)

    def per_head(h, carry):
        val = jnp.full((BLOCK, 2 * BLOCK), rel_ref[0, h], jnp.float32)
        for b, thr in _BUCKET_THRESHOLDS:
            val = jnp.where(n >= thr, rel_ref[b, h], val)
        stack = 2 * (h // Q_PER_KV) + h % 2
        half = (h % Q_PER_KV) // 2
        bias_ref[0, stack, half] = jnp.where(valid, val, NEG_INF)
        bias_ref[1, stack, half] = jnp.where(valid_first, val, NEG_INF)
        return carry

    lax.fori_loop(0, N_Q_HEADS, per_head, 0)


_VARIANT = ((0, 1), (2, 3))


def _project_pieces(x_ref, g1_ref, w_in_ref, vgain_ref, dst):
    u_ref, vn_ref, q_ref, kbuf, vbuf = dst
    ts = x_ref.shape[0]
    o = 2 * D_GMLP
    state = {}

    def norm():
        state["hn"] = _rms(x_ref[...], g1_ref[...]).astype(jnp.bfloat16)

    def gate_u():
        u_ref[...] = _gelu_tanh(_dot(state["hn"], w_in_ref[:, 0:D_GMLP]))

    def gate_v():
        vg = _gelu_tanh(_dot(state["hn"], w_in_ref[:, D_GMLP:2 * D_GMLP]))
        for g in range(GMLP_GROUPS):
            cols = slice(g * GROUP_DIM, (g + 1) * GROUP_DIM)
            vn_ref[:, cols] = _rms(vg[:, cols], vgain_ref[:, cols]).astype(vn_ref.dtype)

    def query():
        q = _dot(state["hn"], w_in_ref[:, o:o + D_ATTN]) * (HEAD_DIM ** -0.5)
        q_ref[...] = q.astype(q_ref.dtype)

    def key_value():
        kv = _dot(state["hn"], w_in_ref[:, o + D_ATTN:o + D_ATTN + 2 * KV_W])
        lo = lax.broadcasted_iota(jnp.int32, (ts, LANES), 1) < HEAD_DIM
        for buf, t in ((kbuf, kv[:, 0:KV_W]), (vbuf, kv[:, KV_W:2 * KV_W])):
            tr = pltpu.roll(t, HEAD_DIM, axis=1)
            buf[0, BLOCK:BLOCK + ts, :] = jnp.where(lo, t, 0.0).astype(buf.dtype)
            buf[1, BLOCK:BLOCK + ts, :] = jnp.where(lo, 0.0, tr).astype(buf.dtype)
            buf[2, BLOCK:BLOCK + ts, :] = jnp.where(lo, tr, 0.0).astype(buf.dtype)
            buf[3, BLOCK:BLOCK + ts, :] = jnp.where(lo, 0.0, t).astype(buf.dtype)

    return [norm, gate_u, gate_v, query, key_value]


def _attend_pieces(i, src, seq_start, sink_ref, bias_ref, mix_ref):
    _, _, q_ref, kbuf, vbuf = src
    rows = slice(i * BLOCK, (i + 1) * BLOCK)
    keys = slice(i * BLOCK, (i + 2) * BLOCK)
    first = jnp.where(seq_start, 1, 0) if i == 0 else 0
    upper = lax.broadcasted_iota(jnp.int32, (2 * BLOCK, 1), 0) < BLOCK
    stacks = [(g, parity) for g in range(N_KV_HEADS) for parity in range(2)]
    state = {}

    def scores():
        for g, parity in stacks:
            qs = jnp.concatenate([q_ref[rows, 2 * g * LANES:(2 * g + 1) * LANES],
                                  q_ref[rows, (2 * g + 1) * LANES:(2 * g + 2) * LANES]], axis=0)
            state[g, parity] = _dot_nt(qs, kbuf[_VARIANT[g][parity], keys, :])

    def softmax():
        for g, parity in stacks:
            head = Q_PER_KV * g + parity
            sink = jnp.where(upper, sink_ref[head], sink_ref[head + 2])
            s = state[g, parity] + bias_ref[first, 2 * g + parity].reshape(2 * BLOCK, 2 * BLOCK)
            m = jnp.maximum(jnp.max(s, axis=-1, keepdims=True), sink)
            e = jnp.exp(s - m)
            denom = jnp.sum(e, axis=-1, keepdims=True) + jnp.exp(sink - m)
            state[g, parity] = (e / denom).astype(jnp.bfloat16)

    def values():
        for g in range(N_KV_HEADS):
            pair0 = D_GMLP + 2 * g * LANES
            acc = (_dot(state[g, 0], vbuf[_VARIANT[g][0], keys, :])
                   + _dot(state[g, 1], vbuf[_VARIANT[g][1], keys, :]))
            mix_ref[rows, pair0:pair0 + LANES] = acc[0:BLOCK].astype(mix_ref.dtype)
            mix_ref[rows, pair0 + LANES:pair0 + 2 * LANES] = acc[BLOCK:].astype(mix_ref.dtype)

    return scores, softmax, values


def _spatial_gate(src, wsp_ref, bsp_ref, mix_ref):
    u_ref, vn_ref = src[0:2]
    t_idx = lax.broadcasted_iota(jnp.int32, (CHUNK, CHUNK), 0)
    s_idx = lax.broadcasted_iota(jnp.int32, (CHUNK, CHUNK), 1)
    causal = s_idx <= t_idx
    for g in range(GMLP_GROUPS):
        cols = slice(g * GROUP_DIM, (g + 1) * GROUP_DIM)
        w_g = jnp.where(causal, wsp_ref[g], 0.0).astype(jnp.bfloat16)
        b_g = bsp_ref[:, g:g + 1]
        for c in range(u_ref.shape[0] // CHUNK):
            rows = slice(c * CHUNK, (c + 1) * CHUNK)
            sv = _dot(w_g, vn_ref[rows, cols]) + b_g
            mix_ref[rows, cols] = (u_ref[rows, cols] * sv).astype(mix_ref.dtype)


def _mixer_kernel(x_ref, xprev_ref, g1_ref, w_in_ref, vgain_ref, wsp_ref, bsp_ref, sink_ref,
                  rel_ref, w_out_ref, o_ref, *scratch, tiles_per_seq):
    slots = (scratch[0:5], scratch[5:10])
    mix_ref, bias_ref = scratch[10:12]
    ts = x_ref.shape[0]
    t = pl.program_id(0)

    @pl.when(t == 0)
    def _():
        _build_bias(rel_ref, bias_ref)
        for ref in slots[1]:
            ref[...] = jnp.zeros(ref.shape, ref.dtype)

    cur_starts = t % tiles_per_seq == 0
    prev_started = (t + tiles_per_seq - 1) % tiles_per_seq == 0

    def step(dst, src):
        project = _project_pieces(x_ref, g1_ref, w_in_ref, vgain_ref, dst)
        project.pop(0)()
        for i in range(ts // BLOCK):
            scores, softmax, values = _attend_pieces(i, src, prev_started, sink_ref, bias_ref,
                                                     mix_ref)
            scores()
            softmax()
            project.pop(0)()
            values()
        assert not project
        _spatial_gate(src, wsp_ref, bsp_ref, mix_ref)
        o_ref[...] = xprev_ref[...] + _dot(mix_ref[...], w_out_ref[...])
        for d, s in ((dst[3], src[3]), (dst[4], src[4])):
            halo = s[:, ts:ts + BLOCK, :]
            d[:, 0:BLOCK, :] = jnp.where(cur_starts, jnp.zeros_like(halo), halo)

    @pl.when(t % 2 == 0)
    def _():
        step(slots[0], slots[1])

    @pl.when(t % 2 == 1)
    def _():
        step(slots[1], slots[0])


def _ffn_kernel(h_ref, p_ref, g2_ref, w1_ref, w2_ref, wproj_ref, wgate_ref, gf_ref, o_ref, *,
                final_norm):
    h = h_ref[...]
    hn = _rms(h, g2_ref[...]).astype(jnp.bfloat16)
    acc = h
    for c in range(D_FF // FF_CHUNK):
        cols = slice(c * FF_CHUNK, (c + 1) * FF_CHUNK)
        t = jnp.maximum(_dot(hn, w1_ref[:, cols]), 0.0)
        acc = acc + _dot((t * t).astype(jnp.bfloat16), w2_ref[cols, :])
    gate = 1.0 / (1.0 + jnp.exp(-_dot(acc.astype(jnp.bfloat16), wgate_ref[...])))
    ple = _dot(p_ref[...].astype(jnp.bfloat16), wproj_ref[...])
    out = acc + gate * ple
    o_ref[...] = _rms(out, gf_ref[...]) if final_norm else out


def _resident(shape):
    return pl.BlockSpec(shape, lambda *_: (0,) * len(shape), pipeline_mode=pl.Buffered(1))


def _mixer(x, g1, w_in, vgain, wsp, bsp_t, sinks, rel, w_out):
    batch, seq, d = x.shape
    ts = SEQ_TILE
    assert ts // BLOCK == 4, "stage-1 pieces are paired one-to-one with attention blocks"
    n_tiles = batch * seq // ts
    x2 = x.reshape(batch * seq, d)
    smem = pl.BlockSpec(memory_space=pltpu.SMEM)
    stage_slot = [
        pltpu.VMEM((ts, D_GMLP), jnp.float32),
        pltpu.VMEM((ts, D_GMLP), jnp.bfloat16),
        pltpu.VMEM((ts, D_ATTN), jnp.bfloat16),
        pltpu.VMEM((4, BLOCK + ts, LANES), jnp.bfloat16),
        pltpu.VMEM((4, BLOCK + ts, LANES), jnp.bfloat16),
    ]
    out = pl.pallas_call(
        functools.partial(_mixer_kernel, tiles_per_seq=seq // ts),
        grid=(n_tiles + 1,),
        in_specs=[
            pl.BlockSpec((ts, d), lambda t: (jnp.minimum(t, n_tiles - 1), 0)),
            pl.BlockSpec((ts, d), lambda t: (jnp.maximum(t - 1, 0), 0)),
            _resident(g1.shape), _resident(w_in.shape), _resident(vgain.shape),
            _resident(wsp.shape), _resident(bsp_t.shape), smem, smem,
            _resident(w_out.shape),
        ],
        out_specs=pl.BlockSpec((ts, d), lambda t: (jnp.maximum(t - 1, 0), 0)),
        out_shape=jax.ShapeDtypeStruct(x2.shape, x2.dtype),
        scratch_shapes=stage_slot + stage_slot + [
            pltpu.VMEM((ts, D_MODEL), jnp.bfloat16),
            pltpu.VMEM((2, 4, 2, BLOCK, 2 * BLOCK), jnp.float32),
        ],
        compiler_params=pltpu.CompilerParams(
            dimension_semantics=("arbitrary",),
            vmem_limit_bytes=VMEM_LIMIT_BYTES),
        name="mixer",
    )(x2, x2, g1, w_in, vgain, wsp, bsp_t, sinks, rel, w_out)
    return out.reshape(batch, seq, d)


def _ffn(h, p, g2, w1, w2, wproj, wgate, gf, *, final_norm):
    rows, d = h.shape
    tm = ROW_TILE
    return pl.pallas_call(
        functools.partial(_ffn_kernel, final_norm=final_norm),
        grid=(rows // tm,),
        in_specs=[
            pl.BlockSpec((tm, d), lambda i: (i, 0)),
            pl.BlockSpec((tm, PLE_DIM), lambda i: (i, 0)),
            _resident(g2.shape), _resident(w1.shape), _resident(w2.shape),
            _resident(wproj.shape), _resident(wgate.shape), _resident(gf.shape),
        ],
        out_specs=pl.BlockSpec((tm, d), lambda i: (i, 0)),
        out_shape=jax.ShapeDtypeStruct(h.shape, h.dtype),
        compiler_params=pltpu.CompilerParams(
            dimension_semantics=("arbitrary",),
            vmem_limit_bytes=VMEM_LIMIT_BYTES),
        name="ffn",
    )(h, p, g2, w1, w2, wproj, wgate, gf)


def kernel(x, p, norm1_gain, w_in, gmlp_v_gain, w_spatial, b_spatial, attn_sinks, rel_bias_table, w_out, norm2_gain, w_ff1, w_ff2, w_ple_proj, w_ple_gate, final_gain):
    batch, seq, d = x.shape
    depth = w_in.shape[0]
    bf16 = jnp.bfloat16
    h = x
    for i in range(depth):
        h = _mixer(h, norm1_gain[i][None], w_in[i].astype(bf16), gmlp_v_gain[i][None],
                   w_spatial[i], b_spatial[i].T, attn_sinks[i], rel_bias_table,
                   w_out[i].astype(bf16))
        h = _ffn(h.reshape(batch * seq, d), p[i].reshape(batch * seq, PLE_DIM),
                 norm2_gain[i][None], w_ff1[i].astype(bf16), w_ff2[i].astype(bf16),
                 w_ple_proj[i].astype(bf16), w_ple_gate[i].astype(bf16),
                 final_gain[None], final_norm=(i == depth - 1)).reshape(batch, seq, d)
    return h
```

```python
import functools
import math

import jax
import jax.numpy as jnp
import numpy as np
from jax import lax
from jax.experimental import pallas as pl
from jax.experimental.pallas import tpu as pltpu

D_MODEL = 1024
PLE_DIM = 256
D_GMLP = 512
GMLP_GROUPS = 4
GROUP_DIM = D_GMLP // GMLP_GROUPS
CHUNK = 128
D_ATTN = 512
HEAD_DIM = 64
N_Q_HEADS = D_ATTN // HEAD_DIM
N_KV_HEADS = 2
Q_PER_KV = N_Q_HEADS // N_KV_HEADS
WINDOW = 128
BLOCK = WINDOW
REL_BUCKETS = 32
REL_MAX_DIST = 128
D_FF = 4 * D_MODEL
KV_W = N_KV_HEADS * HEAD_DIM
D_IN = 2 * D_GMLP + D_ATTN + 2 * KV_W
EPS = 1e-6
NEG_INF = -1e30
LOG2_E = math.log2(math.e)

LANES = 128
SEQ_TILE = 512
ROW_TILE = 512
FF_CHUNK = 1024
VMEM_LIMIT_BYTES = 56 * 1024 * 1024


def _bucket_thresholds():
    max_exact = REL_BUCKETS // 2
    n = np.arange(WINDOW)
    nf = np.maximum(n, 1).astype(np.float32)
    large = max_exact + (np.log(nf / max_exact) / np.float32(math.log(REL_MAX_DIST / max_exact))
                         * (REL_BUCKETS - max_exact)).astype(np.int32)
    bucket = np.where(n < max_exact, n, np.minimum(large, REL_BUCKETS - 1))
    assert (np.diff(bucket) >= 0).all()
    return [(b, int(np.argmax(bucket >= b))) for b in range(1, REL_BUCKETS) if (bucket >= b).any()]


_BUCKET_THRESHOLDS = _bucket_thresholds()


def _rms(x, gain):
    return x * lax.rsqrt(jnp.mean(x * x, axis=-1, keepdims=True) + EPS) * gain


def _gelu_tanh(x):
    c = np.float32(np.sqrt(2 / np.pi))
    return x * (0.5 * (1.0 + jnp.tanh(c * (x + 0.044715 * (x * x * x)))))


def _dot(a, b):
    return jnp.dot(a, b, preferred_element_type=jnp.float32)


def _dot_nt(a, b):
    return lax.dot_general(a, b, (((1,), (1,)), ((), ())), preferred_element_type=jnp.float32)


def _build_bias(rel_ref, bias_ref):
    a = lax.broadcasted_iota(jnp.int32, (BLOCK, 2 * BLOCK), 0)
    j = lax.broadcasted_iota(jnp.int32, (BLOCK, 2 * BLOCK), 1)
    n = BLOCK + a - j
    valid = (n >= 0) & (n < WINDOW)
    valid_first = valid & (j >= BLOCK)

    def per_head(h, carry):
        val = jnp.full((BLOCK, 2 * BLOCK), rel_ref[0, h], jnp.float32)
        for b, thr in _BUCKET_THRESHOLDS:
            val = jnp.where(n >= thr, rel_ref[b, h], val)
        stack = 2 * (h // Q_PER_KV) + h % 2
        half = (h % Q_PER_KV) // 2
        bias_ref[0, stack, half] = jnp.where(valid, val, NEG_INF)
        bias_ref[1, stack, half] = jnp.where(valid_first, val, NEG_INF)
        return carry

    lax.fori_loop(0, N_Q_HEADS, per_head, 0)


_VARIANT = ((0, 1), (2, 3))


def _project_pieces(x_ref, g1_ref, w_in_ref, vgain_ref, dst):
    u_ref, vn_ref, q_ref, kbuf, vbuf = dst
    ts = x_ref.shape[0]
    o = 2 * D_GMLP
    state = {}

    def norm():
        state["hn"] = _rms(x_ref[...], g1_ref[...]).astype(jnp.bfloat16)

    def gate_u():
        u_ref[...] = _gelu_tanh(_dot(state["hn"], w_in_ref[:, 0:D_GMLP]))

    def gate_v():
        vg = _gelu_tanh(_dot(state["hn"], w_in_ref[:, D_GMLP:2 * D_GMLP]))
        for g in range(GMLP_GROUPS):
            cols = slice(g * GROUP_DIM, (g + 1) * GROUP_DIM)
            vn_ref[:, cols] = _rms(vg[:, cols], vgain_ref[:, cols]).astype(vn_ref.dtype)

    def query():
        q = _dot(state["hn"], w_in_ref[:, o:o + D_ATTN]) * (HEAD_DIM ** -0.5)
        q_ref[...] = q.astype(q_ref.dtype)

    def key_value():
        kv = _dot(state["hn"], w_in_ref[:, o + D_ATTN:o + D_ATTN + 2 * KV_W])
        lo = lax.broadcasted_iota(jnp.int32, (ts, LANES), 1) < HEAD_DIM
        for buf, t in ((kbuf, kv[:, 0:KV_W]), (vbuf, kv[:, KV_W:2 * KV_W])):
            tr = pltpu.roll(t, HEAD_DIM, axis=1)
            buf[0, BLOCK:BLOCK + ts, :] = jnp.where(lo, t, 0.0).astype(buf.dtype)
            buf[1, BLOCK:BLOCK + ts, :] = jnp.where(lo, 0.0, tr).astype(buf.dtype)
            buf[2, BLOCK:BLOCK + ts, :] = jnp.where(lo, tr, 0.0).astype(buf.dtype)
            buf[3, BLOCK:BLOCK + ts, :] = jnp.where(lo, 0.0, t).astype(buf.dtype)

    return [norm, gate_u, gate_v, query, key_value]


def _attend_pieces(i, src, seq_start, sink_ref, bias_ref, mix_ref):
    _, _, q_ref, kbuf, vbuf = src
    rows = slice(i * BLOCK, (i + 1) * BLOCK)
    keys = slice(i * BLOCK, (i + 2) * BLOCK)
    first = jnp.where(seq_start, 1, 0) if i == 0 else 0
    upper = lax.broadcasted_iota(jnp.int32, (2 * BLOCK, 1), 0) < BLOCK
    stacks = [(g, parity) for g in range(N_KV_HEADS) for parity in range(2)]
    state = {}

    def scores():
        for g, parity in stacks:
            qs = jnp.concatenate([q_ref[rows, 2 * g * LANES:(2 * g + 1) * LANES],
                                  q_ref[rows, (2 * g + 1) * LANES:(2 * g + 2) * LANES]], axis=0)
            state[g, parity] = _dot_nt(qs, kbuf[_VARIANT[g][parity], keys, :])

    def softmax():
        for g, parity in stacks:
            head = Q_PER_KV * g + parity
            sink = jnp.where(upper, sink_ref[head], sink_ref[head + 2])
            s = state[g, parity] + bias_ref[first, 2 * g + parity].reshape(2 * BLOCK, 2 * BLOCK)
            m = jnp.maximum(jnp.max(s, axis=-1, keepdims=True), sink)
            e = jnp.exp(s - m)
            denom = jnp.sum(e, axis=-1, keepdims=True) + jnp.exp(sink - m)
            state[g, parity] = (e / denom).astype(jnp.bfloat16)

    def values():
        for g in range(N_KV_HEADS):
            pair0 = D_GMLP + 2 * g * LANES
            acc = (_dot(state[g, 0], vbuf[_VARIANT[g][0], keys, :])
                   + _dot(state[g, 1], vbuf[_VARIANT[g][1], keys, :]))
            mix_ref[rows, pair0:pair0 + LANES] = acc[0:BLOCK].astype(mix_ref.dtype)
            mix_ref[rows, pair0 + LANES:pair0 + 2 * LANES] = acc[BLOCK:].astype(mix_ref.dtype)

    return scores, softmax, values


def _spatial_gate(src, wsp_ref, bsp_ref, mix_ref):
    u_ref, vn_ref = src[0:2]
    t_idx = lax.broadcasted_iota(jnp.int32, (CHUNK, CHUNK), 0)
    s_idx = lax.broadcasted_iota(jnp.int32, (CHUNK, CHUNK), 1)
    causal = s_idx <= t_idx
    for g in range(GMLP_GROUPS):
        cols = slice(g * GROUP_DIM, (g + 1) * GROUP_DIM)
        w_g = jnp.where(causal, wsp_ref[g], 0.0).astype(jnp.bfloat16)
        b_g = bsp_ref[:, g:g + 1]
        for c in range(u_ref.shape[0] // CHUNK):
            rows = slice(c * CHUNK, (c + 1) * CHUNK)
            sv = _dot(w_g, vn_ref[rows, cols]) + b_g
            mix_ref[rows, cols] = (u_ref[rows, cols] * sv).astype(mix_ref.dtype)


N_CAST = 4


def _mixer_kernel(x_ref, xprev_ref, g1_ref, w_in_ref, vgain_ref, wsp_ref, bsp_ref, sink_ref,
                  rel_ref, w_out_ref, *rest, tiles_per_seq):
    cast_src, o_ref = rest[0:N_CAST], rest[N_CAST]
    cast_dst, scratch = rest[N_CAST + 1:2 * N_CAST + 1], rest[2 * N_CAST + 1:]
    slots = (scratch[0:5], scratch[5:10])
    mix_ref, bias_ref = scratch[10:12]
    ts = x_ref.shape[0]
    t = pl.program_id(0)

    @pl.when(t == 0)
    def _():
        _build_bias(rel_ref, bias_ref)
        for ref in slots[1]:
            ref[...] = jnp.zeros(ref.shape, ref.dtype)

    cur_starts = t % tiles_per_seq == 0
    prev_started = (t + tiles_per_seq - 1) % tiles_per_seq == 0

    def step(dst, src):
        project = _project_pieces(x_ref, g1_ref, w_in_ref, vgain_ref, dst)
        project.pop(0)()
        for i in range(ts // BLOCK):
            scores, softmax, values = _attend_pieces(i, src, prev_started, sink_ref, bias_ref,
                                                     mix_ref)
            scores()
            softmax()
            project.pop(0)()
            values()
        assert not project
        _spatial_gate(src, wsp_ref, bsp_ref, mix_ref)
        o_ref[...] = xprev_ref[...] + _dot(mix_ref[...], w_out_ref[...])
        for s, d in zip(cast_src, cast_dst):
            d[...] = s[...].astype(d.dtype)
        for d, s in ((dst[3], src[3]), (dst[4], src[4])):
            halo = s[:, ts:ts + BLOCK, :]
            d[:, 0:BLOCK, :] = jnp.where(cur_starts, jnp.zeros_like(halo), halo)

    @pl.when(t % 2 == 0)
    def _():
        step(slots[0], slots[1])

    @pl.when(t % 2 == 1)
    def _():
        step(slots[1], slots[0])


def _ffn_kernel(h_ref, p_ref, g2_ref, w1_ref, w2_ref, wproj_ref, wgate_ref, gf_ref, o_ref, *,
                final_norm):
    tm = h_ref.shape[0]
    halves = (slice(0, tm // 2), slice(tm // 2, tm))
    hn_halves = [_rms(h_ref[r, :], g2_ref[...]).astype(jnp.bfloat16) for r in halves]
    hn = jnp.concatenate(hn_halves, axis=0)
    acc = h_ref[...]
    for c in range(D_FF // FF_CHUNK):
        cols = slice(c * FF_CHUNK, (c + 1) * FF_CHUNK)
        if c == 0:
            t = jnp.concatenate([_dot(part, w1_ref[:, cols]) for part in hn_halves], axis=0)
        else:
            t = _dot(hn, w1_ref[:, cols])
        t = jnp.maximum(t, 0.0)
        acc = acc + _dot((t * t).astype(jnp.bfloat16), w2_ref[cols, :])
    acc_bf16 = acc.astype(jnp.bfloat16)
    gates = [1.0 / (1.0 + jnp.exp2(_dot(acc_bf16[r], wgate_ref[...]) * (-LOG2_E)))
             for r in halves]
    ple = _dot(p_ref[...].astype(jnp.bfloat16), wproj_ref[...])
    for r, gate in zip(halves, gates):
        out = acc[r] + gate * ple[r]
        o_ref[r, :] = _rms(out, gf_ref[...]) if final_norm else out


def _resident(shape):
    return pl.BlockSpec(shape, lambda *_: (0,) * len(shape), pipeline_mode=pl.Buffered(1))


def _cast_slice_spec(arr, axis, n_tiles):
    size = arr.shape[axis]
    min_thick = 16 if axis == 0 else LANES
    thick = max(size // n_tiles, min_thick)
    assert size % thick == 0
    last = size // thick - 1
    block = tuple(thick if a == axis else n for a, n in enumerate(arr.shape))
    return pl.BlockSpec(block, lambda t: tuple(jnp.minimum(t, last) if a == axis else 0
                                               for a in range(arr.ndim)))


def _mixer(x, g1, w_in, vgain, wsp, bsp_t, sinks, rel, w_out, cast):
    batch, seq, d = x.shape
    ts = SEQ_TILE
    assert ts // BLOCK == 4, "stage-1 pieces are paired one-to-one with attention blocks"
    assert len(cast) == N_CAST
    n_tiles = batch * seq // ts
    x2 = x.reshape(batch * seq, d)
    smem = pl.BlockSpec(memory_space=pltpu.SMEM)
    cast_specs = [_cast_slice_spec(w, axis, n_tiles) for w, axis in cast]
    stage_slot = [
        pltpu.VMEM((ts, D_GMLP), jnp.float32),
        pltpu.VMEM((ts, D_GMLP), jnp.bfloat16),
        pltpu.VMEM((ts, D_ATTN), jnp.bfloat16),
        pltpu.VMEM((4, BLOCK + ts, LANES), jnp.bfloat16),
        pltpu.VMEM((4, BLOCK + ts, LANES), jnp.bfloat16),
    ]
    out, *cast_out = pl.pallas_call(
        functools.partial(_mixer_kernel, tiles_per_seq=seq // ts),
        grid=(n_tiles + 1,),
        in_specs=[
            pl.BlockSpec((ts, d), lambda t: (jnp.minimum(t, n_tiles - 1), 0)),
            pl.BlockSpec((ts, d), lambda t: (jnp.maximum(t - 1, 0), 0)),
            _resident(g1.shape), _resident(w_in.shape), _resident(vgain.shape),
            _resident(wsp.shape), _resident(bsp_t.shape), smem, smem,
            _resident(w_out.shape),
        ] + cast_specs,
        out_specs=[pl.BlockSpec((ts, d), lambda t: (jnp.maximum(t - 1, 0), 0))] + cast_specs,
        out_shape=[jax.ShapeDtypeStruct(x2.shape, x2.dtype)]
        + [jax.ShapeDtypeStruct(w.shape, jnp.bfloat16) for w, _ in cast],
        scratch_shapes=stage_slot + stage_slot + [
            pltpu.VMEM((ts, D_MODEL), jnp.bfloat16),
            pltpu.VMEM((2, 4, 2, BLOCK, 2 * BLOCK), jnp.float32),
        ],
        compiler_params=pltpu.CompilerParams(
            dimension_semantics=("arbitrary",),
            vmem_limit_bytes=VMEM_LIMIT_BYTES),
        name="mixer",
    )(x2, x2, g1, w_in, vgain, wsp, bsp_t, sinks, rel, w_out, *[w for w, _ in cast])
    return out.reshape(batch, seq, d), cast_out


def _ffn(h, p, g2, w1, w2, wproj, wgate, gf, *, final_norm):
    rows, d = h.shape
    tm = ROW_TILE
    return pl.pallas_call(
        functools.partial(_ffn_kernel, final_norm=final_norm),
        grid=(rows // tm,),
        in_specs=[
            pl.BlockSpec((tm, d), lambda i: (i, 0)),
            pl.BlockSpec((tm, PLE_DIM), lambda i: (i, 0)),
            _resident(g2.shape), _resident(w1.shape), _resident(w2.shape),
            _resident(wproj.shape), _resident(wgate.shape), _resident(gf.shape),
        ],
        out_specs=pl.BlockSpec((tm, d), lambda i: (i, 0)),
        out_shape=jax.ShapeDtypeStruct(h.shape, h.dtype),
        compiler_params=pltpu.CompilerParams(
            dimension_semantics=("arbitrary",),
            vmem_limit_bytes=VMEM_LIMIT_BYTES),
        name="ffn",
    )(h, p, g2, w1, w2, wproj, wgate, gf)


def kernel(x, p, norm1_gain, w_in, gmlp_v_gain, w_spatial, b_spatial, attn_sinks, rel_bias_table, w_out, norm2_gain, w_ff1, w_ff2, w_ple_proj, w_ple_gate, final_gain):
    batch, seq, d = x.shape
    depth = w_in.shape[0]
    bf16 = jnp.bfloat16
    h = x
    for i in range(depth):
        h, (w1, w2, wproj, wgate) = _mixer(
            h, norm1_gain[i][None], w_in[i].astype(bf16), gmlp_v_gain[i][None],
            w_spatial[i], b_spatial[i].T, attn_sinks[i], rel_bias_table, w_out[i].astype(bf16),
            cast=[(w_ff1[i], 1), (w_ff2[i], 0), (w_ple_proj[i], 0), (w_ple_gate[i], 0)])
        h = _ffn(h.reshape(batch * seq, d), p[i].reshape(batch * seq, PLE_DIM),
                 norm2_gain[i][None], w1, w2, wproj, wgate,
                 final_gain[None], final_norm=(i == depth - 1)).reshape(batch, seq, d)
    return h
```

```python
import functools
import math

import jax
import jax.numpy as jnp
import numpy as np
from jax import lax
from jax.experimental import pallas as pl
from jax.experimental.pallas import tpu as pltpu

D_MODEL = 1024
PLE_DIM = 256
D_GMLP = 512
GMLP_GROUPS = 4
GROUP_DIM = D_GMLP // GMLP_GROUPS
CHUNK = 128
D_ATTN = 512
HEAD_DIM = 64
N_Q_HEADS = D_ATTN // HEAD_DIM
N_KV_HEADS = 2
Q_PER_KV = N_Q_HEADS // N_KV_HEADS
WINDOW = 128
BLOCK = WINDOW
REL_BUCKETS = 32
REL_MAX_DIST = 128
D_FF = 4 * D_MODEL
KV_W = N_KV_HEADS * HEAD_DIM
D_IN = 2 * D_GMLP + D_ATTN + 2 * KV_W
EPS = 1e-6
NEG_INF = -1e30
LOG2_E = math.log2(math.e)

LANES = 128
SEQ_TILE = 512
ROW_TILE = 1024
FF_CHUNK = 1024
VMEM_LIMIT_BYTES = 56 * 1024 * 1024


def _bucket_thresholds():
    max_exact = REL_BUCKETS // 2
    n = np.arange(WINDOW)
    nf = np.maximum(n, 1).astype(np.float32)
    large = max_exact + (np.log(nf / max_exact) / np.float32(math.log(REL_MAX_DIST / max_exact))
                         * (REL_BUCKETS - max_exact)).astype(np.int32)
    bucket = np.where(n < max_exact, n, np.minimum(large, REL_BUCKETS - 1))
    assert (np.diff(bucket) >= 0).all()
    return [(b, int(np.argmax(bucket >= b))) for b in range(1, REL_BUCKETS) if (bucket >= b).any()]


_BUCKET_THRESHOLDS = _bucket_thresholds()


def _rms(x, gain):
    return x * lax.rsqrt(jnp.mean(x * x, axis=-1, keepdims=True) + EPS) * gain


def _gelu_tanh(x):
    c = np.float32(np.sqrt(2 / np.pi))
    return x * (0.5 * (1.0 + jnp.tanh(c * (x + 0.044715 * (x * x * x)))))


def _dot(a, b):
    return jnp.dot(a, b, preferred_element_type=jnp.float32)


def _dot_nt(a, b):
    return lax.dot_general(a, b, (((1,), (1,)), ((), ())), preferred_element_type=jnp.float32)


def _build_bias(rel_ref, bias_ref):
    a = lax.broadcasted_iota(jnp.int32, (BLOCK, 2 * BLOCK), 0)
    j = lax.broadcasted_iota(jnp.int32, (BLOCK, 2 * BLOCK), 1)
    n = BLOCK + a - j
    valid = (n >= 0) & (n < WINDOW)
    valid_first = valid & (j >= BLOCK)

    def per_head(h, carry):
        val = jnp.full((BLOCK, 2 * BLOCK), rel_ref[0, h], jnp.float32)
        for b, thr in _BUCKET_THRESHOLDS:
            val = jnp.where(n >= thr, rel_ref[b, h], val)
        stack = 2 * (h // Q_PER_KV) + h % 2
        half = (h % Q_PER_KV) // 2
        bias_ref[0, stack, half] = jnp.where(valid, val, NEG_INF)
        bias_ref[1, stack, half] = jnp.where(valid_first, val, NEG_INF)
        return carry

    lax.fori_loop(0, N_Q_HEADS, per_head, 0)


_VARIANT = ((0, 1), (2, 3))


def _project_pieces(x_ref, g1_ref, w_in_ref, vgain_ref, dst):
    u_ref, vn_ref, q_ref, kbuf, vbuf = dst
    ts = x_ref.shape[0]
    o = 2 * D_GMLP
    state = {}

    def norm():
        state["hn"] = _rms(x_ref[...], g1_ref[...]).astype(jnp.bfloat16)

    def gate_u():
        u_ref[...] = _gelu_tanh(_dot(state["hn"], w_in_ref[:, 0:D_GMLP]))

    def gate_v():
        vg = _gelu_tanh(_dot(state["hn"], w_in_ref[:, D_GMLP:2 * D_GMLP]))
        for g in range(GMLP_GROUPS):
            cols = slice(g * GROUP_DIM, (g + 1) * GROUP_DIM)
            vn_ref[:, cols] = _rms(vg[:, cols], vgain_ref[:, cols]).astype(vn_ref.dtype)

    def query():
        q = _dot(state["hn"], w_in_ref[:, o:o + D_ATTN]) * (HEAD_DIM ** -0.5)
        q_ref[...] = q.astype(q_ref.dtype)

    def key_value():
        kv = _dot(state["hn"], w_in_ref[:, o + D_ATTN:o + D_ATTN + 2 * KV_W])
        lo = lax.broadcasted_iota(jnp.int32, (ts, LANES), 1) < HEAD_DIM
        for buf, t in ((kbuf, kv[:, 0:KV_W]), (vbuf, kv[:, KV_W:2 * KV_W])):
            tr = pltpu.roll(t, HEAD_DIM, axis=1)
            buf[0, BLOCK:BLOCK + ts, :] = jnp.where(lo, t, 0.0).astype(buf.dtype)
            buf[1, BLOCK:BLOCK + ts, :] = jnp.where(lo, 0.0, tr).astype(buf.dtype)
            buf[2, BLOCK:BLOCK + ts, :] = jnp.where(lo, tr, 0.0).astype(buf.dtype)
            buf[3, BLOCK:BLOCK + ts, :] = jnp.where(lo, 0.0, t).astype(buf.dtype)

    return [norm, gate_u, gate_v, query, key_value]


def _attend_pieces(i, src, seq_start, sink_ref, bias_ref, mix_ref):
    _, _, q_ref, kbuf, vbuf = src
    rows = slice(i * BLOCK, (i + 1) * BLOCK)
    keys = slice(i * BLOCK, (i + 2) * BLOCK)
    first = jnp.where(seq_start, 1, 0) if i == 0 else 0
    upper = lax.broadcasted_iota(jnp.int32, (2 * BLOCK, 1), 0) < BLOCK
    stacks = [(g, parity) for g in range(N_KV_HEADS) for parity in range(2)]
    state = {}

    def scores():
        for g, parity in stacks:
            qs = jnp.concatenate([q_ref[rows, 2 * g * LANES:(2 * g + 1) * LANES],
                                  q_ref[rows, (2 * g + 1) * LANES:(2 * g + 2) * LANES]], axis=0)
            state[g, parity] = _dot_nt(qs, kbuf[_VARIANT[g][parity], keys, :])

    def softmax():
        for g, parity in stacks:
            head = Q_PER_KV * g + parity
            sink = jnp.where(upper, sink_ref[head], sink_ref[head + 2])
            s = state[g, parity] + bias_ref[first, 2 * g + parity].reshape(2 * BLOCK, 2 * BLOCK)
            m = jnp.maximum(jnp.max(s, axis=-1, keepdims=True), sink)
            e = jnp.exp(s - m)
            denom = jnp.sum(e, axis=-1, keepdims=True) + jnp.exp(sink - m)
            state[g, parity] = (e / denom).astype(jnp.bfloat16)

    def values():
        for g in range(N_KV_HEADS):
            pair0 = D_GMLP + 2 * g * LANES
            acc = (_dot(state[g, 0], vbuf[_VARIANT[g][0], keys, :])
                   + _dot(state[g, 1], vbuf[_VARIANT[g][1], keys, :]))
            mix_ref[rows, pair0:pair0 + LANES] = acc[0:BLOCK].astype(mix_ref.dtype)
            mix_ref[rows, pair0 + LANES:pair0 + 2 * LANES] = acc[BLOCK:].astype(mix_ref.dtype)

    return scores, softmax, values


def _spatial_gate(src, wsp_ref, bsp_ref, mix_ref):
    u_ref, vn_ref = src[0:2]
    t_idx = lax.broadcasted_iota(jnp.int32, (CHUNK, CHUNK), 0)
    s_idx = lax.broadcasted_iota(jnp.int32, (CHUNK, CHUNK), 1)
    causal = s_idx <= t_idx
    for g in range(GMLP_GROUPS):
        cols = slice(g * GROUP_DIM, (g + 1) * GROUP_DIM)
        w_g = jnp.where(causal, wsp_ref[g], 0.0).astype(jnp.bfloat16)
        b_g = bsp_ref[:, g:g + 1]
        for c in range(u_ref.shape[0] // CHUNK):
            rows = slice(c * CHUNK, (c + 1) * CHUNK)
            sv = _dot(w_g, vn_ref[rows, cols]) + b_g
            mix_ref[rows, cols] = (u_ref[rows, cols] * sv).astype(mix_ref.dtype)


N_CAST = 4


def _mixer_kernel(x_ref, xprev_ref, g1_ref, w_in_ref, vgain_ref, wsp_ref, bsp_ref, sink_ref,
                  rel_ref, w_out_ref, *rest, tiles_per_seq):
    cast_src, o_ref = rest[0:N_CAST], rest[N_CAST]
    cast_dst, scratch = rest[N_CAST + 1:2 * N_CAST + 1], rest[2 * N_CAST + 1:]
    slots = (scratch[0:5], scratch[5:10])
    mix_ref, bias_ref = scratch[10:12]
    ts = x_ref.shape[0]
    t = pl.program_id(0)

    @pl.when(t == 0)
    def _():
        _build_bias(rel_ref, bias_ref)
        for ref in slots[1]:
            ref[...] = jnp.zeros(ref.shape, ref.dtype)

    cur_starts = t % tiles_per_seq == 0
    prev_started = (t + tiles_per_seq - 1) % tiles_per_seq == 0

    def step(dst, src):
        project = _project_pieces(x_ref, g1_ref, w_in_ref, vgain_ref, dst)
        project.pop(0)()
        for i in range(ts // BLOCK):
            scores, softmax, values = _attend_pieces(i, src, prev_started, sink_ref, bias_ref,
                                                     mix_ref)
            scores()
            softmax()
            project.pop(0)()
            values()
        assert not project
        _spatial_gate(src, wsp_ref, bsp_ref, mix_ref)
        o_ref[...] = xprev_ref[...] + _dot(mix_ref[...], w_out_ref[...])
        for s, d in zip(cast_src, cast_dst):
            d[...] = s[...].astype(d.dtype)
        for d, s in ((dst[3], src[3]), (dst[4], src[4])):
            halo = s[:, ts:ts + BLOCK, :]
            d[:, 0:BLOCK, :] = jnp.where(cur_starts, jnp.zeros_like(halo), halo)

    @pl.when(t % 2 == 0)
    def _():
        step(slots[0], slots[1])

    @pl.when(t % 2 == 1)
    def _():
        step(slots[1], slots[0])


def _ffn_kernel(h_ref, p_ref, g2_ref, w1_ref, w2_ref, wproj_ref, wgate_ref, gf_ref, o_ref, *,
                final_norm):
    tm = h_ref.shape[0]
    halves = (slice(0, tm // 2), slice(tm // 2, tm))
    hn_halves = [_rms(h_ref[r, :], g2_ref[...]).astype(jnp.bfloat16) for r in halves]
    hn = jnp.concatenate(hn_halves, axis=0)
    acc = h_ref[...]
    for c in range(D_FF // FF_CHUNK):
        cols = slice(c * FF_CHUNK, (c + 1) * FF_CHUNK)
        if c == 0:
            t = jnp.concatenate([_dot(part, w1_ref[:, cols]) for part in hn_halves], axis=0)
        else:
            t = _dot(hn, w1_ref[:, cols])
        t = jnp.maximum(t, 0.0)
        acc = acc + _dot((t * t).astype(jnp.bfloat16), w2_ref[cols, :])
    acc_bf16 = acc.astype(jnp.bfloat16)
    gates = [1.0 / (1.0 + jnp.exp2(_dot(acc_bf16[r], wgate_ref[...]) * (-LOG2_E)))
             for r in halves]
    ple = _dot(p_ref[...].astype(jnp.bfloat16), wproj_ref[...])
    for r, gate in zip(halves, gates):
        out = acc[r] + gate * ple[r]
        o_ref[r, :] = _rms(out, gf_ref[...]) if final_norm else out


def _resident(shape):
    return pl.BlockSpec(shape, lambda *_: (0,) * len(shape), pipeline_mode=pl.Buffered(1))


def _cast_slice_spec(arr, axis, n_tiles):
    size = arr.shape[axis]
    min_thick = 16 if axis == 0 else LANES
    thick = max(size // n_tiles, min_thick)
    assert size % thick == 0
    last = size // thick - 1
    block = tuple(thick if a == axis else n for a, n in enumerate(arr.shape))
    return pl.BlockSpec(block, lambda t: tuple(jnp.minimum(t, last) if a == axis else 0
                                               for a in range(arr.ndim)))


def _mixer(x, g1, w_in, vgain, wsp, bsp_t, sinks, rel, w_out, cast):
    batch, seq, d = x.shape
    ts = SEQ_TILE
    assert ts // BLOCK == 4, "stage-1 pieces are paired one-to-one with attention blocks"
    assert len(cast) == N_CAST
    n_tiles = batch * seq // ts
    x2 = x.reshape(batch * seq, d)
    smem = pl.BlockSpec(memory_space=pltpu.SMEM)
    cast_specs = [_cast_slice_spec(w, axis, n_tiles) for w, axis in cast]
    stage_slot = [
        pltpu.VMEM((ts, D_GMLP), jnp.float32),
        pltpu.VMEM((ts, D_GMLP), jnp.bfloat16),
        pltpu.VMEM((ts, D_ATTN), jnp.bfloat16),
        pltpu.VMEM((4, BLOCK + ts, LANES), jnp.bfloat16),
        pltpu.VMEM((4, BLOCK + ts, LANES), jnp.bfloat16),
    ]
    out, *cast_out = pl.pallas_call(
        functools.partial(_mixer_kernel, tiles_per_seq=seq // ts),
        grid=(n_tiles + 1,),
        in_specs=[
            pl.BlockSpec((ts, d), lambda t: (jnp.minimum(t, n_tiles - 1), 0)),
            pl.BlockSpec((ts, d), lambda t: (jnp.maximum(t - 1, 0), 0)),
            _resident(g1.shape), _resident(w_in.shape), _resident(vgain.shape),
            _resident(wsp.shape), _resident(bsp_t.shape), smem, smem,
            _resident(w_out.shape),
        ] + cast_specs,
        out_specs=[pl.BlockSpec((ts, d), lambda t: (jnp.maximum(t - 1, 0), 0))] + cast_specs,
        out_shape=[jax.ShapeDtypeStruct(x2.shape, x2.dtype)]
        + [jax.ShapeDtypeStruct(w.shape, jnp.bfloat16) for w, _ in cast],
        scratch_shapes=stage_slot + stage_slot + [
            pltpu.VMEM((ts, D_MODEL), jnp.bfloat16),
            pltpu.VMEM((2, 4, 2, BLOCK, 2 * BLOCK), jnp.float32),
        ],
        compiler_params=pltpu.CompilerParams(
            dimension_semantics=("arbitrary",),
            vmem_limit_bytes=VMEM_LIMIT_BYTES),
        name="mixer",
    )(x2, x2, g1, w_in, vgain, wsp, bsp_t, sinks, rel, w_out, *[w for w, _ in cast])
    return out.reshape(batch, seq, d), cast_out


def _ffn(h, p, g2, w1, w2, wproj, wgate, gf, *, final_norm):
    rows, d = h.shape
    tm = ROW_TILE
    return pl.pallas_call(
        functools.partial(_ffn_kernel, final_norm=final_norm),
        grid=(rows // tm,),
        in_specs=[
            pl.BlockSpec((tm, d), lambda i: (i, 0)),
            pl.BlockSpec((tm, PLE_DIM), lambda i: (i, 0)),
            _resident(g2.shape), _resident(w1.shape), _resident(w2.shape),
            _resident(wproj.shape), _resident(wgate.shape), _resident(gf.shape),
        ],
        out_specs=pl.BlockSpec((tm, d), lambda i: (i, 0)),
        out_shape=jax.ShapeDtypeStruct(h.shape, h.dtype),
        compiler_params=pltpu.CompilerParams(
            dimension_semantics=("arbitrary",),
            vmem_limit_bytes=VMEM_LIMIT_BYTES),
        name="ffn",
    )(h, p, g2, w1, w2, wproj, wgate, gf)


def kernel(x, p, norm1_gain, w_in, gmlp_v_gain, w_spatial, b_spatial, attn_sinks, rel_bias_table, w_out, norm2_gain, w_ff1, w_ff2, w_ple_proj, w_ple_gate, final_gain):
    batch, seq, d = x.shape
    depth = w_in.shape[0]
    bf16 = jnp.bfloat16
    h = x
    for i in range(depth):
        h, (w1, w2, wproj, wgate) = _mixer(
            h, norm1_gain[i][None], w_in[i].astype(bf16), gmlp_v_gain[i][None],
            w_spatial[i], b_spatial[i].T, attn_sinks[i], rel_bias_table, w_out[i].astype(bf16),
            cast=[(w_ff1[i], 1), (w_ff2[i], 0), (w_ple_proj[i], 0), (w_ple_gate[i], 0)])
        h = _ffn(h.reshape(batch * seq, d), p[i].reshape(batch * seq, PLE_DIM),
                 norm2_gain[i][None], w1, w2, wproj, wgate,
                 final_gain[None], final_norm=(i == depth - 1)).reshape(batch, seq, d)
    return h
```

```python
import functools
import math

import jax
import jax.numpy as jnp
import numpy as np
from jax import lax
from jax.experimental import pallas as pl
from jax.experimental.pallas import tpu as pltpu

D_MODEL = 1024
PLE_DIM = 256
D_GMLP = 512
GMLP_GROUPS = 4
GROUP_DIM = D_GMLP // GMLP_GROUPS
CHUNK = 128
D_ATTN = 512
HEAD_DIM = 64
N_Q_HEADS = D_ATTN // HEAD_DIM
N_KV_HEADS = 2
Q_PER_KV = N_Q_HEADS // N_KV_HEADS
WINDOW = 128
BLOCK = WINDOW
REL_BUCKETS = 32
REL_MAX_DIST = 128
D_FF = 4 * D_MODEL
KV_W = N_KV_HEADS * HEAD_DIM
D_IN = 2 * D_GMLP + D_ATTN + 2 * KV_W
EPS = 1e-6
NEG_INF = -1e30
LOG2_E = math.log2(math.e)

LANES = 128
SEQ_TILE = 512
ROW_TILE = 1024
FF_CHUNK = 1024
VMEM_LIMIT_BYTES = 56 * 1024 * 1024


def _bucket_thresholds():
    max_exact = REL_BUCKETS // 2
    n = np.arange(WINDOW)
    nf = np.maximum(n, 1).astype(np.float32)
    large = max_exact + (np.log(nf / max_exact) / np.float32(math.log(REL_MAX_DIST / max_exact))
                         * (REL_BUCKETS - max_exact)).astype(np.int32)
    bucket = np.where(n < max_exact, n, np.minimum(large, REL_BUCKETS - 1))
    assert (np.diff(bucket) >= 0).all()
    return [(b, int(np.argmax(bucket >= b))) for b in range(1, REL_BUCKETS) if (bucket >= b).any()]


_BUCKET_THRESHOLDS = _bucket_thresholds()


def _rms(x, gain):
    return x * lax.rsqrt(jnp.mean(x * x, axis=-1, keepdims=True) + EPS) * gain


def _gelu_tanh(x):
    c = float(np.float32(np.sqrt(2 / np.pi)))
    poly = x * x * (-2.0 * c * 0.044715 * LOG2_E) + (-2.0 * c * LOG2_E)
    return x / (1.0 + jnp.exp2(x * poly))


def _dot(a, b):
    return jnp.dot(a, b, preferred_element_type=jnp.float32)


def _dot_nt(a, b):
    return lax.dot_general(a, b, (((1,), (1,)), ((), ())), preferred_element_type=jnp.float32)


def _build_bias(rel_ref, bias_ref):
    a = lax.broadcasted_iota(jnp.int32, (BLOCK, 2 * BLOCK), 0)
    j = lax.broadcasted_iota(jnp.int32, (BLOCK, 2 * BLOCK), 1)
    n = BLOCK + a - j
    valid = (n >= 0) & (n < WINDOW)
    valid_first = valid & (j >= BLOCK)

    def per_head(h, carry):
        val = jnp.full((BLOCK, 2 * BLOCK), rel_ref[0, h], jnp.float32)
        for b, thr in _BUCKET_THRESHOLDS:
            val = jnp.where(n >= thr, rel_ref[b, h], val)
        stack = 2 * (h // Q_PER_KV) + h % 2
        half = (h % Q_PER_KV) // 2
        val = val * LOG2_E
        bias_ref[0, stack, half] = jnp.where(valid, val, NEG_INF)
        bias_ref[1, stack, half] = jnp.where(valid_first, val, NEG_INF)
        return carry

    lax.fori_loop(0, N_Q_HEADS, per_head, 0)


_VARIANT = ((0, 1), (2, 3))


def _project_pieces(x_ref, g1_ref, w_in_ref, vgain_ref, dst):
    u_ref, vn_ref, q_ref, kbuf, vbuf = dst
    ts = x_ref.shape[0]
    o = 2 * D_GMLP
    state = {}

    def norm():
        state["hn"] = _rms(x_ref[...], g1_ref[...]).astype(jnp.bfloat16)

    def gate_u():
        u_ref[...] = _gelu_tanh(_dot(state["hn"], w_in_ref[:, 0:D_GMLP]))

    def gate_v():
        vg = _gelu_tanh(_dot(state["hn"], w_in_ref[:, D_GMLP:2 * D_GMLP]))
        for g in range(GMLP_GROUPS):
            cols = slice(g * GROUP_DIM, (g + 1) * GROUP_DIM)
            vn_ref[:, cols] = _rms(vg[:, cols], vgain_ref[:, cols]).astype(vn_ref.dtype)

    def query():
        q = _dot(state["hn"], w_in_ref[:, o:o + D_ATTN]) * (HEAD_DIM ** -0.5 * LOG2_E)
        q_ref[...] = q.astype(q_ref.dtype)

    def key_value():
        kv = _dot(state["hn"], w_in_ref[:, o + D_ATTN:o + D_ATTN + 2 * KV_W])
        lo = lax.broadcasted_iota(jnp.int32, (ts, LANES), 1) < HEAD_DIM
        for buf, t in ((kbuf, kv[:, 0:KV_W]), (vbuf, kv[:, KV_W:2 * KV_W])):
            tr = pltpu.roll(t, HEAD_DIM, axis=1)
            buf[0, BLOCK:BLOCK + ts, :] = jnp.where(lo, t, 0.0).astype(buf.dtype)
            buf[1, BLOCK:BLOCK + ts, :] = jnp.where(lo, 0.0, tr).astype(buf.dtype)
            buf[2, BLOCK:BLOCK + ts, :] = jnp.where(lo, tr, 0.0).astype(buf.dtype)
            buf[3, BLOCK:BLOCK + ts, :] = jnp.where(lo, 0.0, t).astype(buf.dtype)

    return [norm, gate_u, gate_v, query, key_value]


def _attend_pieces(i, src, seq_start, sink_ref, bias_ref, mix_ref):
    _, _, q_ref, kbuf, vbuf = src
    rows = slice(i * BLOCK, (i + 1) * BLOCK)
    keys = slice(i * BLOCK, (i + 2) * BLOCK)
    first = jnp.where(seq_start, 1, 0) if i == 0 else 0
    upper = lax.broadcasted_iota(jnp.int32, (2 * BLOCK, 1), 0) < BLOCK
    stacks = [(g, parity) for g in range(N_KV_HEADS) for parity in range(2)]
    state = {}

    def scores():
        for g, parity in stacks:
            qs = jnp.concatenate([q_ref[rows, 2 * g * LANES:(2 * g + 1) * LANES],
                                  q_ref[rows, (2 * g + 1) * LANES:(2 * g + 2) * LANES]], axis=0)
            state[g, parity] = _dot_nt(qs, kbuf[_VARIANT[g][parity], keys, :])

    def softmax():
        for g, parity in stacks:
            head = Q_PER_KV * g + parity
            sink = jnp.where(upper, sink_ref[head], sink_ref[head + 2]) * LOG2_E
            s = state[g, parity] + bias_ref[first, 2 * g + parity].reshape(2 * BLOCK, 2 * BLOCK)
            m = jnp.maximum(jnp.max(s, axis=-1, keepdims=True), sink)
            e = jnp.exp2(s - m)
            denom = jnp.sum(e, axis=-1, keepdims=True) + jnp.exp2(sink - m)
            state[g, parity] = e.astype(jnp.bfloat16)
            state["inv", g, parity] = 1.0 / denom

    def values():
        low = lax.broadcasted_iota(jnp.int32, (2 * BLOCK, LANES), 1) < HEAD_DIM
        for g in range(N_KV_HEADS):
            pair0 = D_GMLP + 2 * g * LANES
            acc = (_dot(state[g, 0], vbuf[_VARIANT[g][0], keys, :])
                   + _dot(state[g, 1], vbuf[_VARIANT[g][1], keys, :]))
            acc = acc * jnp.where(low, state["inv", g, 0], state["inv", g, 1])
            mix_ref[rows, pair0:pair0 + LANES] = acc[0:BLOCK].astype(mix_ref.dtype)
            mix_ref[rows, pair0 + LANES:pair0 + 2 * LANES] = acc[BLOCK:].astype(mix_ref.dtype)

    return scores, softmax, values


def _spatial_gate(src, wsp_ref, bsp_ref, mix_ref):
    u_ref, vn_ref = src[0:2]
    t_idx = lax.broadcasted_iota(jnp.int32, (CHUNK, CHUNK), 0)
    s_idx = lax.broadcasted_iota(jnp.int32, (CHUNK, CHUNK), 1)
    causal = s_idx <= t_idx
    for g in range(GMLP_GROUPS):
        cols = slice(g * GROUP_DIM, (g + 1) * GROUP_DIM)
        w_g = jnp.where(causal, wsp_ref[g], 0.0).astype(jnp.bfloat16)
        b_g = bsp_ref[:, g:g + 1]
        for c in range(u_ref.shape[0] // CHUNK):
            rows = slice(c * CHUNK, (c + 1) * CHUNK)
            sv = _dot(w_g, vn_ref[rows, cols]) + b_g
            mix_ref[rows, cols] = (u_ref[rows, cols] * sv).astype(mix_ref.dtype)


N_CAST = 4


def _mixer_kernel(x_ref, xprev_ref, g1_ref, w_in_ref, vgain_ref, wsp_ref, bsp_ref, sink_ref,
                  rel_ref, w_out_ref, *rest, tiles_per_seq):
    cast_src, o_ref = rest[0:N_CAST], rest[N_CAST]
    cast_dst, scratch = rest[N_CAST + 1:2 * N_CAST + 1], rest[2 * N_CAST + 1:]
    slots = (scratch[0:5], scratch[5:10])
    mix_ref, bias_ref = scratch[10:12]
    ts = x_ref.shape[0]
    t = pl.program_id(0)

    @pl.when(t == 0)
    def _():
        _build_bias(rel_ref, bias_ref)
        for ref in slots[1]:
            ref[...] = jnp.zeros(ref.shape, ref.dtype)

    cur_starts = t % tiles_per_seq == 0
    prev_started = (t + tiles_per_seq - 1) % tiles_per_seq == 0

    def step(dst, src):
        project = _project_pieces(x_ref, g1_ref, w_in_ref, vgain_ref, dst)
        n_blk = ts // BLOCK
        attend = [_attend_pieces(i, src, prev_started, sink_ref, bias_ref, mix_ref)
                  for i in range(n_blk)]
        attend[0][0]()
        _spatial_gate(src, wsp_ref, bsp_ref, mix_ref)
        project.pop(0)()
        for i in range(n_blk):
            scores, softmax, values = attend[i]
            if i + 1 < n_blk:
                attend[i + 1][0]()
            softmax()
            project.pop(0)()
            values()
        assert not project
        o_ref[...] = xprev_ref[...] + _dot(mix_ref[...], w_out_ref[...])
        for s, d in zip(cast_src, cast_dst):
            d[...] = s[...].astype(d.dtype)
        for d, s in ((dst[3], src[3]), (dst[4], src[4])):
            halo = s[:, ts:ts + BLOCK, :]
            d[:, 0:BLOCK, :] = jnp.where(cur_starts, jnp.zeros_like(halo), halo)

    @pl.when(t % 2 == 0)
    def _():
        step(slots[0], slots[1])

    @pl.when(t % 2 == 1)
    def _():
        step(slots[1], slots[0])


def _ffn_kernel(h_ref, p_ref, g2_ref, w1_ref, w2_ref, wproj_ref, wgate_ref, gf_ref, o_ref, *,
                final_norm):
    tm = h_ref.shape[0]
    halves = (slice(0, tm // 2), slice(tm // 2, tm))
    hn_halves = [_rms(h_ref[r, :], g2_ref[...]).astype(jnp.bfloat16) for r in halves]
    hn = jnp.concatenate(hn_halves, axis=0)
    acc = h_ref[...]
    for c in range(D_FF // FF_CHUNK):
        cols = slice(c * FF_CHUNK, (c + 1) * FF_CHUNK)
        if c == 0:
            t = jnp.concatenate([_dot(part, w1_ref[:, cols]) for part in hn_halves], axis=0)
        else:
            t = _dot(hn, w1_ref[:, cols])
        t = jnp.maximum(t, 0.0)
        acc = acc + _dot((t * t).astype(jnp.bfloat16), w2_ref[cols, :])
    acc_bf16 = acc.astype(jnp.bfloat16)
    gates = [1.0 / (1.0 + jnp.exp2(_dot(acc_bf16[r], wgate_ref[...]) * (-LOG2_E)))
             for r in halves]
    ple = _dot(p_ref[...].astype(jnp.bfloat16), wproj_ref[...])
    for r, gate in zip(halves, gates):
        out = acc[r] + gate * ple[r]
        o_ref[r, :] = _rms(out, gf_ref[...]) if final_norm else out


def _resident(shape):
    return pl.BlockSpec(shape, lambda *_: (0,) * len(shape), pipeline_mode=pl.Buffered(1))


def _cast_slice_spec(arr, axis, n_tiles):
    size = arr.shape[axis]
    min_thick = 16 if axis == 0 else LANES
    thick = max(size // n_tiles, min_thick)
    assert size % thick == 0
    last = size // thick - 1
    block = tuple(thick if a == axis else n for a, n in enumerate(arr.shape))
    return pl.BlockSpec(block, lambda t: tuple(jnp.minimum(t, last) if a == axis else 0
                                               for a in range(arr.ndim)))


def _mixer(x, g1, w_in, vgain, wsp, bsp_t, sinks, rel, w_out, cast):
    batch, seq, d = x.shape
    ts = SEQ_TILE
    assert ts // BLOCK == 4, "stage-1 pieces are paired one-to-one with attention blocks"
    assert len(cast) == N_CAST
    n_tiles = batch * seq // ts
    x2 = x.reshape(batch * seq, d)
    smem = pl.BlockSpec(memory_space=pltpu.SMEM)
    cast_specs = [_cast_slice_spec(w, axis, n_tiles) for w, axis in cast]
    stage_slot = [
        pltpu.VMEM((ts, D_GMLP), jnp.float32),
        pltpu.VMEM((ts, D_GMLP), jnp.bfloat16),
        pltpu.VMEM((ts, D_ATTN), jnp.bfloat16),
        pltpu.VMEM((4, BLOCK + ts, LANES), jnp.bfloat16),
        pltpu.VMEM((4, BLOCK + ts, LANES), jnp.bfloat16),
    ]
    out, *cast_out = pl.pallas_call(
        functools.partial(_mixer_kernel, tiles_per_seq=seq // ts),
        grid=(n_tiles + 1,),
        in_specs=[
            pl.BlockSpec((ts, d), lambda t: (jnp.minimum(t, n_tiles - 1), 0)),
            pl.BlockSpec((ts, d), lambda t: (jnp.maximum(t - 1, 0), 0)),
            _resident(g1.shape), _resident(w_in.shape), _resident(vgain.shape),
            _resident(wsp.shape), _resident(bsp_t.shape), smem, smem,
            _resident(w_out.shape),
        ] + cast_specs,
        out_specs=[pl.BlockSpec((ts, d), lambda t: (jnp.maximum(t - 1, 0), 0))] + cast_specs,
        out_shape=[jax.ShapeDtypeStruct(x2.shape, x2.dtype)]
        + [jax.ShapeDtypeStruct(w.shape, jnp.bfloat16) for w, _ in cast],
        scratch_shapes=stage_slot + stage_slot + [
            pltpu.VMEM((ts, D_MODEL), jnp.bfloat16),
            pltpu.VMEM((2, 4, 2, BLOCK, 2 * BLOCK), jnp.float32),
        ],
        compiler_params=pltpu.CompilerParams(
            dimension_semantics=("arbitrary",),
            vmem_limit_bytes=VMEM_LIMIT_BYTES),
        name="mixer",
    )(x2, x2, g1, w_in, vgain, wsp, bsp_t, sinks, rel, w_out, *[w for w, _ in cast])
    return out.reshape(batch, seq, d), cast_out


def _ffn(h, p, g2, w1, w2, wproj, wgate, gf, *, final_norm):
    rows, d = h.shape
    tm = ROW_TILE
    return pl.pallas_call(
        functools.partial(_ffn_kernel, final_norm=final_norm),
        grid=(rows // tm,),
        in_specs=[
            pl.BlockSpec((tm, d), lambda i: (i, 0)),
            pl.BlockSpec((tm, PLE_DIM), lambda i: (i, 0)),
            _resident(g2.shape), _resident(w1.shape), _resident(w2.shape),
            _resident(wproj.shape), _resident(wgate.shape), _resident(gf.shape),
        ],
        out_specs=pl.BlockSpec((tm, d), lambda i: (i, 0)),
        out_shape=jax.ShapeDtypeStruct(h.shape, h.dtype),
        compiler_params=pltpu.CompilerParams(
            dimension_semantics=("arbitrary",),
            vmem_limit_bytes=VMEM_LIMIT_BYTES),
        name="ffn",
    )(h, p, g2, w1, w2, wproj, wgate, gf)


def kernel(x, p, norm1_gain, w_in, gmlp_v_gain, w_spatial, b_spatial, attn_sinks, rel_bias_table, w_out, norm2_gain, w_ff1, w_ff2, w_ple_proj, w_ple_gate, final_gain):
    batch, seq, d = x.shape
    depth = w_in.shape[0]
    bf16 = jnp.bfloat16
    h = x
    for i in range(depth):
        h, (w1, w2, wproj, wgate) = _mixer(
            h, norm1_gain[i][None], w_in[i].astype(bf16), gmlp_v_gain[i][None],
            w_spatial[i], b_spatial[i].T, attn_sinks[i], rel_bias_table, w_out[i].astype(bf16),
            cast=[(w_ff1[i], 1), (w_ff2[i], 0), (w_ple_proj[i], 0), (w_ple_gate[i], 0)])
        h = _ffn(h.reshape(batch * seq, d), p[i].reshape(batch * seq, PLE_DIM),
                 norm2_gain[i][None], w1, w2, wproj, wgate,
                 final_gain[None], final_norm=(i == depth - 1)).reshape(batch, seq, d)
    return h
```

```python
import functools
import math

import jax
import jax.numpy as jnp
import numpy as np
from jax import lax
from jax.experimental import pallas as pl
from jax.experimental.pallas import tpu as pltpu

D_MODEL = 1024
PLE_DIM = 256
D_GMLP = 512
GMLP_GROUPS = 4
GROUP_DIM = D_GMLP // GMLP_GROUPS
CHUNK = 128
D_ATTN = 512
HEAD_DIM = 64
N_Q_HEADS = D_ATTN // HEAD_DIM
N_KV_HEADS = 2
Q_PER_KV = N_Q_HEADS // N_KV_HEADS
WINDOW = 128
BLOCK = WINDOW
REL_BUCKETS = 32
REL_MAX_DIST = 128
D_FF = 4 * D_MODEL
KV_W = N_KV_HEADS * HEAD_DIM
D_IN = 2 * D_GMLP + D_ATTN + 2 * KV_W
EPS = 1e-6
NEG_INF = -1e30
LOG2_E = math.log2(math.e)

LANES = 128
SEQ_TILE = 512
ROW_TILE = 1024
FF_CHUNK = 1024
VMEM_LIMIT_BYTES = 56 * 1024 * 1024


def _bucket_thresholds():
    max_exact = REL_BUCKETS // 2
    n = np.arange(WINDOW)
    nf = np.maximum(n, 1).astype(np.float32)
    large = max_exact + (np.log(nf / max_exact) / np.float32(math.log(REL_MAX_DIST / max_exact))
                         * (REL_BUCKETS - max_exact)).astype(np.int32)
    bucket = np.where(n < max_exact, n, np.minimum(large, REL_BUCKETS - 1))
    assert (np.diff(bucket) >= 0).all()
    return [(b, int(np.argmax(bucket >= b))) for b in range(1, REL_BUCKETS) if (bucket >= b).any()]


_BUCKET_THRESHOLDS = _bucket_thresholds()


def _rms(x, gain):
    return x * lax.rsqrt(jnp.mean(x * x, axis=-1, keepdims=True) + EPS) * gain


def _gelu_tanh(x):
    c = float(np.float32(np.sqrt(2 / np.pi)))
    poly = x * x * (-2.0 * c * 0.044715 * LOG2_E) + (-2.0 * c * LOG2_E)
    return x / (1.0 + jnp.exp2(x * poly))


def _dot(a, b):
    return jnp.dot(a, b, preferred_element_type=jnp.float32)


def _dot_nt(a, b):
    return lax.dot_general(a, b, (((1,), (1,)), ((), ())), preferred_element_type=jnp.float32)


def _build_bias(rel_ref, bias_ref):
    a = lax.broadcasted_iota(jnp.int32, (BLOCK, BLOCK), 0)
    j = lax.broadcasted_iota(jnp.int32, (BLOCK, BLOCK), 1)
    from_prev = j > a
    n = jnp.where(from_prev, BLOCK + a - j, a - j)

    def per_head(h, carry):
        val = jnp.full((BLOCK, BLOCK), rel_ref[0, h], jnp.float32)
        for b, thr in _BUCKET_THRESHOLDS:
            val = jnp.where(n >= thr, rel_ref[b, h], val)
        stack = 2 * (h // Q_PER_KV) + h % 2
        half = (h % Q_PER_KV) // 2
        val = val * LOG2_E
        bias_ref[0, stack, half] = val
        bias_ref[1, stack, half] = jnp.where(from_prev, NEG_INF, val)
        return carry

    lax.fori_loop(0, N_Q_HEADS, per_head, 0)


_VARIANT = ((0, 1), (2, 3))


def _project_pieces(x_ref, g1_ref, w_in_ref, vgain_ref, dst):
    u_ref, vn_ref, q_ref, kbuf, vbuf = dst
    ts = x_ref.shape[0]
    o = 2 * D_GMLP
    state = {}

    def norm():
        state["hn"] = _rms(x_ref[...], g1_ref[...]).astype(jnp.bfloat16)

    def gate_u():
        u_ref[...] = _gelu_tanh(_dot(state["hn"], w_in_ref[:, 0:D_GMLP]))

    def gate_v():
        vg = _gelu_tanh(_dot(state["hn"], w_in_ref[:, D_GMLP:2 * D_GMLP]))
        for g in range(GMLP_GROUPS):
            cols = slice(g * GROUP_DIM, (g + 1) * GROUP_DIM)
            vn_ref[:, cols] = _rms(vg[:, cols], vgain_ref[:, cols]).astype(vn_ref.dtype)

    def query():
        q = _dot(state["hn"], w_in_ref[:, o:o + D_ATTN]) * (HEAD_DIM ** -0.5 * LOG2_E)
        q_ref[...] = q.astype(q_ref.dtype)

    def key_value():
        kv = _dot(state["hn"], w_in_ref[:, o + D_ATTN:o + D_ATTN + 2 * KV_W])
        lo = lax.broadcasted_iota(jnp.int32, (ts, LANES), 1) < HEAD_DIM
        for buf, t in ((kbuf, kv[:, 0:KV_W]), (vbuf, kv[:, KV_W:2 * KV_W])):
            tr = pltpu.roll(t, HEAD_DIM, axis=1)
            buf[0, BLOCK:BLOCK + ts, :] = jnp.where(lo, t, 0.0).astype(buf.dtype)
            buf[1, BLOCK:BLOCK + ts, :] = jnp.where(lo, 0.0, tr).astype(buf.dtype)
            buf[2, BLOCK:BLOCK + ts, :] = jnp.where(lo, tr, 0.0).astype(buf.dtype)
            buf[3, BLOCK:BLOCK + ts, :] = jnp.where(lo, 0.0, t).astype(buf.dtype)

    return [norm, gate_u, gate_v, query, key_value]


def _attend_pieces(i, src, seq_start, sink_ref, bias_ref, mix_ref):
    _, _, q_ref, kbuf, vbuf = src
    rows = slice(i * BLOCK, (i + 1) * BLOCK)
    keys = slice(i * BLOCK, (i + 2) * BLOCK)
    first = jnp.where(seq_start, 1, 0) if i == 0 else 0
    upper = lax.broadcasted_iota(jnp.int32, (2 * BLOCK, 1), 0) < BLOCK
    query = lax.broadcasted_iota(jnp.int32, (2 * BLOCK, BLOCK), 0) % BLOCK
    from_prev = lax.broadcasted_iota(jnp.int32, (2 * BLOCK, BLOCK), 1) > query
    stacks = [(g, parity) for g in range(N_KV_HEADS) for parity in range(2)]
    state = {}

    def scores():
        for g, parity in stacks:
            qs = jnp.concatenate([q_ref[rows, 2 * g * LANES:(2 * g + 1) * LANES],
                                  q_ref[rows, (2 * g + 1) * LANES:(2 * g + 2) * LANES]], axis=0)
            state[g, parity] = _dot_nt(qs, kbuf[_VARIANT[g][parity], keys, :])

    def softmax():
        zero = jnp.zeros((2 * BLOCK, BLOCK), jnp.bfloat16)
        for g, parity in stacks:
            head = Q_PER_KV * g + parity
            sink = jnp.where(upper, sink_ref[head], sink_ref[head + 2]) * LOG2_E
            s = state[g, parity]
            s = jnp.where(from_prev, s[:, 0:BLOCK], s[:, BLOCK:2 * BLOCK])
            s = s + bias_ref[first, 2 * g + parity].reshape(2 * BLOCK, BLOCK)
            m = jnp.maximum(jnp.max(s, axis=-1, keepdims=True), sink)
            e = jnp.exp2(s - m)
            denom = jnp.sum(e, axis=-1, keepdims=True) + jnp.exp2(sink - m)
            p = e.astype(jnp.bfloat16)
            state[g, parity] = jnp.concatenate(
                [jnp.where(from_prev, p, zero), jnp.where(from_prev, zero, p)], axis=1)
            state["inv", g, parity] = 1.0 / denom

    def values():
        low = lax.broadcasted_iota(jnp.int32, (2 * BLOCK, LANES), 1) < HEAD_DIM
        for g in range(N_KV_HEADS):
            pair0 = D_GMLP + 2 * g * LANES
            acc = (_dot(state[g, 0], vbuf[_VARIANT[g][0], keys, :])
                   + _dot(state[g, 1], vbuf[_VARIANT[g][1], keys, :]))
            acc = acc * jnp.where(low, state["inv", g, 0], state["inv", g, 1])
            mix_ref[rows, pair0:pair0 + LANES] = acc[0:BLOCK].astype(mix_ref.dtype)
            mix_ref[rows, pair0 + LANES:pair0 + 2 * LANES] = acc[BLOCK:].astype(mix_ref.dtype)

    return scores, softmax, values


def _spatial_gate(src, wsp_ref, bsp_ref, mix_ref):
    u_ref, vn_ref = src[0:2]
    t_idx = lax.broadcasted_iota(jnp.int32, (CHUNK, CHUNK), 0)
    s_idx = lax.broadcasted_iota(jnp.int32, (CHUNK, CHUNK), 1)
    causal = s_idx <= t_idx
    for g in range(GMLP_GROUPS):
        cols = slice(g * GROUP_DIM, (g + 1) * GROUP_DIM)
        w_g = jnp.where(causal, wsp_ref[g], 0.0).astype(jnp.bfloat16)
        b_g = bsp_ref[:, g:g + 1]
        for c in range(u_ref.shape[0] // CHUNK):
            rows = slice(c * CHUNK, (c + 1) * CHUNK)
            sv = _dot(w_g, vn_ref[rows, cols]) + b_g
            mix_ref[rows, cols] = (u_ref[rows, cols] * sv).astype(mix_ref.dtype)


N_CAST = 4


def _mixer_kernel(x_ref, xprev_ref, g1_ref, w_in_ref, vgain_ref, wsp_ref, bsp_ref, sink_ref,
                  rel_ref, w_out_ref, *rest, tiles_per_seq):
    cast_src, o_ref = rest[0:N_CAST], rest[N_CAST]
    cast_dst, scratch = rest[N_CAST + 1:2 * N_CAST + 1], rest[2 * N_CAST + 1:]
    slots = (scratch[0:5], scratch[5:10])
    mix_ref, bias_ref = scratch[10:12]
    ts = x_ref.shape[0]
    t = pl.program_id(0)

    @pl.when(t == 0)
    def _():
        _build_bias(rel_ref, bias_ref)
        for ref in slots[1]:
            ref[...] = jnp.zeros(ref.shape, ref.dtype)

    cur_starts = t % tiles_per_seq == 0
    prev_started = (t + tiles_per_seq - 1) % tiles_per_seq == 0

    def step(dst, src):
        project = _project_pieces(x_ref, g1_ref, w_in_ref, vgain_ref, dst)
        n_blk = ts // BLOCK
        attend = [_attend_pieces(i, src, prev_started, sink_ref, bias_ref, mix_ref)
                  for i in range(n_blk)]
        attend[0][0]()
        _spatial_gate(src, wsp_ref, bsp_ref, mix_ref)
        project.pop(0)()
        for i in range(n_blk):
            scores, softmax, values = attend[i]
            if i + 1 < n_blk:
                attend[i + 1][0]()
            softmax()
            project.pop(0)()
            values()
        assert not project
        o_ref[...] = xprev_ref[...] + _dot(mix_ref[...], w_out_ref[...])
        for s, d in zip(cast_src, cast_dst):
            d[...] = s[...].astype(d.dtype)
        for d, s in ((dst[3], src[3]), (dst[4], src[4])):
            halo = s[:, ts:ts + BLOCK, :]
            d[:, 0:BLOCK, :] = jnp.where(cur_starts, jnp.zeros_like(halo), halo)

    @pl.when(t % 2 == 0)
    def _():
        step(slots[0], slots[1])

    @pl.when(t % 2 == 1)
    def _():
        step(slots[1], slots[0])


def _ffn_kernel(h_ref, p_ref, g2_ref, w1_ref, w2_ref, wproj_ref, wgate_ref, gf_ref, o_ref, *,
                final_norm):
    tm = h_ref.shape[0]
    halves = (slice(0, tm // 2), slice(tm // 2, tm))
    hn_halves = [_rms(h_ref[r, :], g2_ref[...]).astype(jnp.bfloat16) for r in halves]
    hn = jnp.concatenate(hn_halves, axis=0)
    acc = h_ref[...]
    hidden = []
    for c in range(D_FF // FF_CHUNK):
        cols = slice(c * FF_CHUNK, (c + 1) * FF_CHUNK)
        if c == 0:
            t = jnp.concatenate([_dot(part, w1_ref[:, cols]) for part in hn_halves], axis=0)
        else:
            t = _dot(hn, w1_ref[:, cols])
        t = jnp.maximum(t, 0.0)
        hidden.append((t * t).astype(jnp.bfloat16))
    acc = acc + _dot(jnp.concatenate(hidden, axis=1), w2_ref[...])
    acc_bf16 = acc.astype(jnp.bfloat16)
    gates = [1.0 / (1.0 + jnp.exp2(_dot(acc_bf16[r], wgate_ref[...]) * (-LOG2_E)))
             for r in halves]
    ple = _dot(p_ref[...].astype(jnp.bfloat16), wproj_ref[...])
    for r, gate in zip(halves, gates):
        out = acc[r] + gate * ple[r]
        o_ref[r, :] = _rms(out, gf_ref[...]) if final_norm else out


def _resident(shape):
    return pl.BlockSpec(shape, lambda *_: (0,) * len(shape), pipeline_mode=pl.Buffered(1))


def _cast_slice_spec(arr, axis, n_tiles):
    size = arr.shape[axis]
    min_thick = 16 if axis == 0 else LANES
    thick = max(size // n_tiles, min_thick)
    assert size % thick == 0
    last = size // thick - 1
    block = tuple(thick if a == axis else n for a, n in enumerate(arr.shape))
    return pl.BlockSpec(block, lambda t: tuple(jnp.minimum(t, last) if a == axis else 0
                                               for a in range(arr.ndim)))


def _mixer(x, g1, w_in, vgain, wsp, bsp_t, sinks, rel, w_out, cast):
    batch, seq, d = x.shape
    ts = SEQ_TILE
    assert ts // BLOCK == 4, "stage-1 pieces are paired one-to-one with attention blocks"
    assert len(cast) == N_CAST
    n_tiles = batch * seq // ts
    x2 = x.reshape(batch * seq, d)
    smem = pl.BlockSpec(memory_space=pltpu.SMEM)
    cast_specs = [_cast_slice_spec(w, axis, n_tiles) for w, axis in cast]
    stage_slot = [
        pltpu.VMEM((ts, D_GMLP), jnp.float32),
        pltpu.VMEM((ts, D_GMLP), jnp.bfloat16),
        pltpu.VMEM((ts, D_ATTN), jnp.bfloat16),
        pltpu.VMEM((4, BLOCK + ts, LANES), jnp.bfloat16),
        pltpu.VMEM((4, BLOCK + ts, LANES), jnp.bfloat16),
    ]
    out, *cast_out = pl.pallas_call(
        functools.partial(_mixer_kernel, tiles_per_seq=seq // ts),
        grid=(n_tiles + 1,),
        in_specs=[
            pl.BlockSpec((ts, d), lambda t: (jnp.minimum(t, n_tiles - 1), 0)),
            pl.BlockSpec((ts, d), lambda t: (jnp.maximum(t - 1, 0), 0)),
            _resident(g1.shape), _resident(w_in.shape), _resident(vgain.shape),
            _resident(wsp.shape), _resident(bsp_t.shape), smem, smem,
            _resident(w_out.shape),
        ] + cast_specs,
        out_specs=[pl.BlockSpec((ts, d), lambda t: (jnp.maximum(t - 1, 0), 0))] + cast_specs,
        out_shape=[jax.ShapeDtypeStruct(x2.shape, x2.dtype)]
        + [jax.ShapeDtypeStruct(w.shape, jnp.bfloat16) for w, _ in cast],
        scratch_shapes=stage_slot + stage_slot + [
            pltpu.VMEM((ts, D_MODEL), jnp.bfloat16),
            pltpu.VMEM((2, 4, 2, BLOCK, BLOCK), jnp.float32),
        ],
        compiler_params=pltpu.CompilerParams(
            dimension_semantics=("arbitrary",),
            vmem_limit_bytes=VMEM_LIMIT_BYTES),
        name="mixer",
    )(x2, x2, g1, w_in, vgain, wsp, bsp_t, sinks, rel, w_out, *[w for w, _ in cast])
    return out.reshape(batch, seq, d), cast_out


def _ffn(h, p, g2, w1, w2, wproj, wgate, gf, *, final_norm):
    rows, d = h.shape
    tm = ROW_TILE
    return pl.pallas_call(
        functools.partial(_ffn_kernel, final_norm=final_norm),
        grid=(rows // tm,),
        in_specs=[
            pl.BlockSpec((tm, d), lambda i: (i, 0)),
            pl.BlockSpec((tm, PLE_DIM), lambda i: (i, 0)),
            _resident(g2.shape), _resident(w1.shape), _resident(w2.shape),
            _resident(wproj.shape), _resident(wgate.shape), _resident(gf.shape),
        ],
        out_specs=pl.BlockSpec((tm, d), lambda i: (i, 0)),
        out_shape=jax.ShapeDtypeStruct(h.shape, h.dtype),
        compiler_params=pltpu.CompilerParams(
            dimension_semantics=("arbitrary",),
            vmem_limit_bytes=VMEM_LIMIT_BYTES),
        name="ffn",
    )(h, p, g2, w1, w2, wproj, wgate, gf)


def kernel(x, p, norm1_gain, w_in, gmlp_v_gain, w_spatial, b_spatial, attn_sinks, rel_bias_table, w_out, norm2_gain, w_ff1, w_ff2, w_ple_proj, w_ple_gate, final_gain):
    batch, seq, d = x.shape
    depth = w_in.shape[0]
    bf16 = jnp.bfloat16
    h = x
    for i in range(depth):
        h, (w1, w2, wproj, wgate) = _mixer(
            h, norm1_gain[i][None], w_in[i].astype(bf16), gmlp_v_gain[i][None],
            w_spatial[i], b_spatial[i].T, attn_sinks[i], rel_bias_table, w_out[i].astype(bf16),
            cast=[(w_ff1[i], 1), (w_ff2[i], 0), (w_ple_proj[i], 0), (w_ple_gate[i], 0)])
        h = _ffn(h.reshape(batch * seq, d), p[i].reshape(batch * seq, PLE_DIM),
                 norm2_gain[i][None], w1, w2, wproj, wgate,
                 final_gain[None], final_norm=(i == depth - 1)).reshape(batch, seq, d)
    return h
```

```python
import functools
import math

import jax
import jax.numpy as jnp
import numpy as np
from jax import lax
from jax.experimental import pallas as pl
from jax.experimental.pallas import tpu as pltpu

D_MODEL = 1024
PLE_DIM = 256
D_GMLP = 512
GMLP_GROUPS = 4
GROUP_DIM = D_GMLP // GMLP_GROUPS
CHUNK = 128
D_ATTN = 512
HEAD_DIM = 64
N_Q_HEADS = D_ATTN // HEAD_DIM
N_KV_HEADS = 2
Q_PER_KV = N_Q_HEADS // N_KV_HEADS
WINDOW = 128
BLOCK = WINDOW
REL_BUCKETS = 32
REL_MAX_DIST = 128
D_FF = 4 * D_MODEL
KV_W = N_KV_HEADS * HEAD_DIM
D_IN = 2 * D_GMLP + D_ATTN + 2 * KV_W
EPS = 1e-6
NEG_INF = -1e30
LOG2_E = math.log2(math.e)

LANES = 128
SEQ_TILE = 512
ROW_TILE = 1024
FF_CHUNK = 1024
FFN_ROW_PART = 256
VMEM_LIMIT_BYTES = 56 * 1024 * 1024


def _bucket_thresholds():
    max_exact = REL_BUCKETS // 2
    n = np.arange(WINDOW)
    nf = np.maximum(n, 1).astype(np.float32)
    large = max_exact + (np.log(nf / max_exact) / np.float32(math.log(REL_MAX_DIST / max_exact))
                         * (REL_BUCKETS - max_exact)).astype(np.int32)
    bucket = np.where(n < max_exact, n, np.minimum(large, REL_BUCKETS - 1))
    assert (np.diff(bucket) >= 0).all()
    return [(b, int(np.argmax(bucket >= b))) for b in range(1, REL_BUCKETS) if (bucket >= b).any()]


_BUCKET_THRESHOLDS = _bucket_thresholds()


def _rms(x, gain):
    return x * lax.rsqrt(jnp.mean(x * x, axis=-1, keepdims=True) + EPS) * gain


def _gelu_tanh(x):
    c = float(np.float32(np.sqrt(2 / np.pi)))
    poly = x * x * (-2.0 * c * 0.044715 * LOG2_E) + (-2.0 * c * LOG2_E)
    return x / (1.0 + jnp.exp2(x * poly))


def _dot(a, b):
    return jnp.dot(a, b, preferred_element_type=jnp.float32)


def _dot_nt(a, b):
    return lax.dot_general(a, b, (((1,), (1,)), ((), ())), preferred_element_type=jnp.float32)


def _build_bias(rel_ref, bias_ref):
    a = lax.broadcasted_iota(jnp.int32, (BLOCK, BLOCK), 0)
    j = lax.broadcasted_iota(jnp.int32, (BLOCK, BLOCK), 1)
    from_prev = j > a
    n = jnp.where(from_prev, BLOCK + a - j, a - j)

    def per_head(h, carry):
        val = jnp.full((BLOCK, BLOCK), rel_ref[0, h], jnp.float32)
        for b, thr in _BUCKET_THRESHOLDS:
            val = jnp.where(n >= thr, rel_ref[b, h], val)
        stack = 2 * (h // Q_PER_KV) + h % 2
        half = (h % Q_PER_KV) // 2
        val = val * LOG2_E
        bias_ref[0, stack, half] = val
        bias_ref[1, stack, half] = jnp.where(from_prev, NEG_INF, val)
        return carry

    lax.fori_loop(0, N_Q_HEADS, per_head, 0)


_VARIANT = ((0, 1), (2, 3))


def _project_pieces(x_ref, g1_ref, w_in_ref, vgain_ref, dst):
    u_ref, vn_ref, q_ref, kbuf, vbuf = dst
    ts = x_ref.shape[0]
    o = 2 * D_GMLP
    state = {}

    def norm():
        state["hn"] = _rms(x_ref[...], g1_ref[...]).astype(jnp.bfloat16)

    def gate_u():
        u_ref[...] = _gelu_tanh(_dot(state["hn"], w_in_ref[:, 0:D_GMLP]))

    def gate_v():
        vg = _gelu_tanh(_dot(state["hn"], w_in_ref[:, D_GMLP:2 * D_GMLP]))
        for g in range(GMLP_GROUPS):
            cols = slice(g * GROUP_DIM, (g + 1) * GROUP_DIM)
            vn_ref[:, cols] = _rms(vg[:, cols], vgain_ref[:, cols]).astype(vn_ref.dtype)

    def query():
        q = _dot(state["hn"], w_in_ref[:, o:o + D_ATTN]) * (HEAD_DIM ** -0.5 * LOG2_E)
        q_ref[...] = q.astype(q_ref.dtype)

    def key_value():
        kv = _dot(state["hn"], w_in_ref[:, o + D_ATTN:o + D_ATTN + 2 * KV_W])
        lo = lax.broadcasted_iota(jnp.int32, (ts, LANES), 1) < HEAD_DIM
        for buf, t in ((kbuf, kv[:, 0:KV_W]), (vbuf, kv[:, KV_W:2 * KV_W])):
            tr = pltpu.roll(t, HEAD_DIM, axis=1)
            buf[0, BLOCK:BLOCK + ts, :] = jnp.where(lo, t, 0.0).astype(buf.dtype)
            buf[1, BLOCK:BLOCK + ts, :] = jnp.where(lo, 0.0, tr).astype(buf.dtype)
            buf[2, BLOCK:BLOCK + ts, :] = jnp.where(lo, tr, 0.0).astype(buf.dtype)
            buf[3, BLOCK:BLOCK + ts, :] = jnp.where(lo, 0.0, t).astype(buf.dtype)

    return [norm, gate_u, gate_v, query, key_value]


def _attend_pieces(i, src, seq_start, sink_ref, bias_ref, mix_ref, s_scr, p_scr):
    _, _, q_ref, kbuf, vbuf = src
    rows = slice(i * BLOCK, (i + 1) * BLOCK)
    keys = slice(i * BLOCK, (i + 2) * BLOCK)
    first = jnp.where(seq_start, 1, 0) if i == 0 else 0
    upper = lax.broadcasted_iota(jnp.int32, (2 * BLOCK, 1), 0) < BLOCK
    query = lax.broadcasted_iota(jnp.int32, (2 * BLOCK, BLOCK), 0) % BLOCK
    from_prev = lax.broadcasted_iota(jnp.int32, (2 * BLOCK, BLOCK), 1) > query
    stacks = [(g, parity) for g in range(N_KV_HEADS) for parity in range(2)]
    state = {}

    def scores():
        for g, parity in stacks:
            qs = jnp.concatenate([q_ref[rows, 2 * g * LANES:(2 * g + 1) * LANES],
                                  q_ref[rows, (2 * g + 1) * LANES:(2 * g + 2) * LANES]], axis=0)
            s = _dot_nt(qs, kbuf[_VARIANT[g][parity], keys, :])
            s = jnp.where(from_prev, s[:, 0:BLOCK], s[:, BLOCK:2 * BLOCK])
            s_scr[4 * i + 2 * g + parity] = (
                s + bias_ref[first, 2 * g + parity].reshape(2 * BLOCK, BLOCK))

    def softmax():
        zero = jnp.zeros((2 * BLOCK, BLOCK), jnp.bfloat16)
        for g, parity in stacks:
            head = Q_PER_KV * g + parity
            sink = jnp.where(upper, sink_ref[head], sink_ref[head + 2]) * LOG2_E
            s = s_scr[4 * i + 2 * g + parity]
            m = jnp.maximum(jnp.max(s, axis=-1, keepdims=True), sink)
            e = jnp.exp2(s - m)
            denom = jnp.sum(e, axis=-1, keepdims=True) + jnp.exp2(sink - m)
            p = e.astype(jnp.bfloat16)
            p_scr[4 * i + 2 * g + parity] = jnp.concatenate(
                [jnp.where(from_prev, p, zero), jnp.where(from_prev, zero, p)], axis=1)
            state["inv", g, parity] = 1.0 / denom

    def values():
        low = lax.broadcasted_iota(jnp.int32, (2 * BLOCK, LANES), 1) < HEAD_DIM
        for g in range(N_KV_HEADS):
            pair0 = D_GMLP + 2 * g * LANES
            acc = (_dot(p_scr[4 * i + 2 * g], vbuf[_VARIANT[g][0], keys, :])
                   + _dot(p_scr[4 * i + 2 * g + 1], vbuf[_VARIANT[g][1], keys, :]))
            acc = acc * jnp.where(low, state["inv", g, 0], state["inv", g, 1])
            mix_ref[rows, pair0:pair0 + LANES] = acc[0:BLOCK].astype(mix_ref.dtype)
            mix_ref[rows, pair0 + LANES:pair0 + 2 * LANES] = acc[BLOCK:].astype(mix_ref.dtype)

    return scores, softmax, values


def _spatial_gate(src, wsp_ref, bsp_ref, mix_ref):
    u_ref, vn_ref = src[0:2]
    t_idx = lax.broadcasted_iota(jnp.int32, (CHUNK, CHUNK), 0)
    s_idx = lax.broadcasted_iota(jnp.int32, (CHUNK, CHUNK), 1)
    causal = s_idx <= t_idx
    for g in range(GMLP_GROUPS):
        cols = slice(g * GROUP_DIM, (g + 1) * GROUP_DIM)
        w_g = jnp.where(causal, wsp_ref[g], 0.0).astype(jnp.bfloat16)
        b_g = bsp_ref[:, g:g + 1]
        for c in range(u_ref.shape[0] // CHUNK):
            rows = slice(c * CHUNK, (c + 1) * CHUNK)
            sv = _dot(w_g, vn_ref[rows, cols]) + b_g
            mix_ref[rows, cols] = (u_ref[rows, cols] * sv).astype(mix_ref.dtype)


N_CAST = 4
SCORES_AHEAD = 2


def _mixer_kernel(x_ref, xprev_ref, g1_ref, w_in_ref, vgain_ref, wsp_ref, bsp_ref, sink_ref,
                  rel_ref, w_out_ref, *rest, tiles_per_seq, n_tiles):
    cast_src, o_ref = rest[0:N_CAST], rest[N_CAST]
    cast_dst, scratch = rest[N_CAST + 1:2 * N_CAST + 1], rest[2 * N_CAST + 1:]
    slots = (scratch[0:5], scratch[5:10])
    mix_ref, bias_ref, s_scr, p_scr = scratch[10:14]
    ts = x_ref.shape[0]
    t = pl.program_id(0)

    cur_starts = t % tiles_per_seq == 0
    prev_started = (t + tiles_per_seq - 1) % tiles_per_seq == 0
    n_blk = ts // BLOCK

    def step(dst, src):
        project = _project_pieces(x_ref, g1_ref, w_in_ref, vgain_ref, dst) if dst else []
        attend = [_attend_pieces(i, src, prev_started, sink_ref, bias_ref, mix_ref, s_scr,
                                 p_scr) for i in range(n_blk)] if src else []
        if src:
            for i in range(SCORES_AHEAD):
                attend[i][0]()
            _spatial_gate(src, wsp_ref, bsp_ref, mix_ref)
        if dst:
            project.pop(0)()
        for i in range(n_blk):
            if src:
                if i + SCORES_AHEAD < n_blk:
                    attend[i + SCORES_AHEAD][0]()
                attend[i][1]()
            if dst:
                project.pop(0)()
            if src:
                attend[i][2]()
        assert not project
        if src:
            o_ref[...] = xprev_ref[...] + _dot(mix_ref[...], w_out_ref[...])
        for s, d in zip(cast_src, cast_dst):
            d[...] = s[...].astype(d.dtype)
        if dst:
            for k in (3, 4):
                zeros = jnp.zeros((4, BLOCK, LANES), dst[k].dtype)
                dst[k][:, 0:BLOCK, :] = (
                    jnp.where(cur_starts, zeros, src[k][:, ts:ts + BLOCK, :]) if src else zeros)

    @pl.when(t == 0)
    def _():
        _build_bias(rel_ref, bias_ref)
        step(slots[0], None)

    @pl.when(t == n_tiles)
    def _():
        step(None, slots[(n_tiles - 1) % 2])

    @pl.when((t > 0) & (t < n_tiles) & (t % 2 == 0))
    def _():
        step(slots[0], slots[1])

    @pl.when((t < n_tiles) & (t % 2 == 1))
    def _():
        step(slots[1], slots[0])


def _ffn_kernel(h_ref, p_ref, g2_ref, w1_ref, w2_ref, wproj_ref, wgate_ref, gf_ref, o_ref, *,
                final_norm):
    tm = h_ref.shape[0]
    halves = [slice(r, r + FFN_ROW_PART) for r in range(0, tm, FFN_ROW_PART)]
    hn_halves = [_rms(h_ref[r, :], g2_ref[...]).astype(jnp.bfloat16) for r in halves]
    hn = jnp.concatenate(hn_halves, axis=0)
    acc = h_ref[...]
    hidden = []
    for c in range(D_FF // FF_CHUNK):
        cols = slice(c * FF_CHUNK, (c + 1) * FF_CHUNK)
        if c == 0:
            t = jnp.concatenate([_dot(part, w1_ref[:, cols]) for part in hn_halves], axis=0)
        else:
            t = _dot(hn, w1_ref[:, cols])
        t = jnp.maximum(t, 0.0)
        hidden.append((t * t).astype(jnp.bfloat16))
    acc = acc + _dot(jnp.concatenate(hidden, axis=1), w2_ref[...])
    acc_bf16 = acc.astype(jnp.bfloat16)
    gates = [1.0 / (1.0 + jnp.exp2(_dot(acc_bf16[r], wgate_ref[...]) * (-LOG2_E)))
             for r in halves]
    ple = _dot(p_ref[...].astype(jnp.bfloat16), wproj_ref[...])
    for r, gate in zip(halves, gates):
        out = acc[r] + gate * ple[r]
        o_ref[r, :] = _rms(out, gf_ref[...]) if final_norm else out


def _resident(shape):
    return pl.BlockSpec(shape, lambda *_: (0,) * len(shape), pipeline_mode=pl.Buffered(1))


def _cast_slice_spec(arr, axis, n_tiles):
    size = arr.shape[axis]
    min_thick = 16 if axis == 0 else LANES
    thick = max(size // n_tiles, min_thick)
    assert size % thick == 0
    last = size // thick - 1
    block = tuple(thick if a == axis else n for a, n in enumerate(arr.shape))
    return pl.BlockSpec(block, lambda t: tuple(jnp.minimum(t, last) if a == axis else 0
                                               for a in range(arr.ndim)))


def _mixer(x, g1, w_in, vgain, wsp, bsp_t, sinks, rel, w_out, cast):
    batch, seq, d = x.shape
    ts = SEQ_TILE
    assert ts // BLOCK == 4, "stage-1 pieces are paired one-to-one with attention blocks"
    assert len(cast) == N_CAST
    n_tiles = batch * seq // ts
    x2 = x.reshape(batch * seq, d)
    smem = pl.BlockSpec(memory_space=pltpu.SMEM)
    cast_specs = [_cast_slice_spec(w, axis, n_tiles) for w, axis in cast]
    stage_slot = [
        pltpu.VMEM((ts, D_GMLP), jnp.float32),
        pltpu.VMEM((ts, D_GMLP), jnp.bfloat16),
        pltpu.VMEM((ts, D_ATTN), jnp.bfloat16),
        pltpu.VMEM((4, BLOCK + ts, LANES), jnp.bfloat16),
        pltpu.VMEM((4, BLOCK + ts, LANES), jnp.bfloat16),
    ]
    out, *cast_out = pl.pallas_call(
        functools.partial(_mixer_kernel, tiles_per_seq=seq // ts, n_tiles=n_tiles),
        grid=(n_tiles + 1,),
        in_specs=[
            pl.BlockSpec((ts, d), lambda t: (jnp.minimum(t, n_tiles - 1), 0)),
            pl.BlockSpec((ts, d), lambda t: (jnp.maximum(t - 1, 0), 0)),
            _resident(g1.shape), _resident(w_in.shape), _resident(vgain.shape),
            _resident(wsp.shape), _resident(bsp_t.shape), smem, smem,
            _resident(w_out.shape),
        ] + cast_specs,
        out_specs=[pl.BlockSpec((ts, d), lambda t: (jnp.maximum(t - 1, 0), 0))] + cast_specs,
        out_shape=[jax.ShapeDtypeStruct(x2.shape, x2.dtype)]
        + [jax.ShapeDtypeStruct(w.shape, jnp.bfloat16) for w, _ in cast],
        scratch_shapes=stage_slot + stage_slot + [
            pltpu.VMEM((ts, D_MODEL), jnp.bfloat16),
            pltpu.VMEM((2, 4, 2, BLOCK, BLOCK), jnp.float32),
            pltpu.VMEM((16, 2 * BLOCK, BLOCK), jnp.float32),
            pltpu.VMEM((16, 2 * BLOCK, 2 * BLOCK), jnp.bfloat16),
        ],
        compiler_params=pltpu.CompilerParams(
            dimension_semantics=("arbitrary",),
            vmem_limit_bytes=VMEM_LIMIT_BYTES),
        name="mixer",
    )(x2, x2, g1, w_in, vgain, wsp, bsp_t, sinks, rel, w_out, *[w for w, _ in cast])
    return out.reshape(batch, seq, d), cast_out


def _ffn(h, p, g2, w1, w2, wproj, wgate, gf, *, final_norm):
    rows, d = h.shape
    tm = ROW_TILE
    return pl.pallas_call(
        functools.partial(_ffn_kernel, final_norm=final_norm),
        grid=(rows // tm,),
        in_specs=[
            pl.BlockSpec((tm, d), lambda i: (i, 0)),
            pl.BlockSpec((tm, PLE_DIM), lambda i: (i, 0)),
            _resident(g2.shape), _resident(w1.shape), _resident(w2.shape),
            _resident(wproj.shape), _resident(wgate.shape), _resident(gf.shape),
        ],
        out_specs=pl.BlockSpec((tm, d), lambda i: (i, 0)),
        out_shape=jax.ShapeDtypeStruct(h.shape, h.dtype),
        compiler_params=pltpu.CompilerParams(
            dimension_semantics=("arbitrary",),
            vmem_limit_bytes=VMEM_LIMIT_BYTES),
        name="ffn",
    )(h, p, g2, w1, w2, wproj, wgate, gf)


def kernel(x, p, norm1_gain, w_in, gmlp_v_gain, w_spatial, b_spatial, attn_sinks, rel_bias_table, w_out, norm2_gain, w_ff1, w_ff2, w_ple_proj, w_ple_gate, final_gain):
    batch, seq, d = x.shape
    depth = w_in.shape[0]
    bf16 = jnp.bfloat16
    h = x
    for i in range(depth):
        h, (w1, w2, wproj, wgate) = _mixer(
            h, norm1_gain[i][None], w_in[i].astype(bf16), gmlp_v_gain[i][None],
            w_spatial[i], b_spatial[i].T, attn_sinks[i], rel_bias_table, w_out[i].astype(bf16),
            cast=[(w_ff1[i], 1), (w_ff2[i], 0), (w_ple_proj[i], 0), (w_ple_gate[i], 0)])
        h = _ffn(h.reshape(batch * seq, d), p[i].reshape(batch * seq, PLE_DIM),
                 norm2_gain[i][None], w1, w2, wproj, wgate,
                 final_gain[None], final_norm=(i == depth - 1)).reshape(batch, seq, d)
    return h
```

```python
import functools
import math

import jax
import jax.numpy as jnp
import numpy as np
from jax import lax
from jax.experimental import pallas as pl
from jax.experimental.pallas import tpu as pltpu

D_MODEL = 1024
PLE_DIM = 256
D_GMLP = 512
GMLP_GROUPS = 4
GROUP_DIM = D_GMLP // GMLP_GROUPS
CHUNK = 128
D_ATTN = 512
HEAD_DIM = 64
N_Q_HEADS = D_ATTN // HEAD_DIM
N_KV_HEADS = 2
Q_PER_KV = N_Q_HEADS // N_KV_HEADS
WINDOW = 128
BLOCK = WINDOW
REL_BUCKETS = 32
REL_MAX_DIST = 128
D_FF = 4 * D_MODEL
KV_W = N_KV_HEADS * HEAD_DIM
D_IN = 2 * D_GMLP + D_ATTN + 2 * KV_W
EPS = 1e-6
NEG_INF = -1e30
LOG2_E = math.log2(math.e)

LANES = 128
SEQ_TILE = 512
ROW_TILE = 1024
FF_CHUNK = 1024
FFN_ROW_PART = 256
VMEM_LIMIT_BYTES = 56 * 1024 * 1024


def _bucket_thresholds():
    max_exact = REL_BUCKETS // 2
    n = np.arange(WINDOW)
    nf = np.maximum(n, 1).astype(np.float32)
    large = max_exact + (np.log(nf / max_exact) / np.float32(math.log(REL_MAX_DIST / max_exact))
                         * (REL_BUCKETS - max_exact)).astype(np.int32)
    bucket = np.where(n < max_exact, n, np.minimum(large, REL_BUCKETS - 1))
    assert (np.diff(bucket) >= 0).all()
    return [(b, int(np.argmax(bucket >= b))) for b in range(1, REL_BUCKETS) if (bucket >= b).any()]


_BUCKET_THRESHOLDS = _bucket_thresholds()


def _rms(x, gain):
    return x * lax.rsqrt(jnp.mean(x * x, axis=-1, keepdims=True) + EPS) * gain


def _gelu_tanh(x):
    c = float(np.float32(np.sqrt(2 / np.pi)))
    poly = x * x * (-2.0 * c * 0.044715 * LOG2_E) + (-2.0 * c * LOG2_E)
    return x / (1.0 + jnp.exp2(x * poly))


def _dot(a, b):
    return jnp.dot(a, b, preferred_element_type=jnp.float32)


def _dot_nt(a, b):
    return lax.dot_general(a, b, (((1,), (1,)), ((), ())), preferred_element_type=jnp.float32)


def _build_bias(rel_ref, bias_ref):
    a = lax.broadcasted_iota(jnp.int32, (BLOCK, BLOCK), 0)
    j = lax.broadcasted_iota(jnp.int32, (BLOCK, BLOCK), 1)
    from_prev = j > a
    n = jnp.where(from_prev, BLOCK + a - j, a - j)

    def per_head(h, carry):
        val = jnp.full((BLOCK, BLOCK), rel_ref[0, h], jnp.float32)
        for b, thr in _BUCKET_THRESHOLDS:
            val = jnp.where(n >= thr, rel_ref[b, h], val)
        stack = 2 * (h // Q_PER_KV) + h % 2
        half = (h % Q_PER_KV) // 2
        val = val * LOG2_E
        bias_ref[0, stack, half] = val
        bias_ref[1, stack, half] = jnp.where(from_prev, NEG_INF, val)
        return carry

    lax.fori_loop(0, N_Q_HEADS, per_head, 0)


_VARIANT = ((0, 1), (2, 3))


def _project_pieces(x_ref, g1_ref, w_in_ref, vgain_ref, dst):
    u_ref, vn_ref, q_ref, kbuf, vbuf = dst
    ts = x_ref.shape[0]
    o = 2 * D_GMLP
    state = {}

    def norm():
        state["hn"] = _rms(x_ref[...], g1_ref[...]).astype(jnp.bfloat16)

    def gate_u():
        u_ref[...] = _gelu_tanh(_dot(state["hn"], w_in_ref[:, 0:D_GMLP]))

    def gate_v():
        vg = _gelu_tanh(_dot(state["hn"], w_in_ref[:, D_GMLP:2 * D_GMLP]))
        for g in range(GMLP_GROUPS):
            cols = slice(g * GROUP_DIM, (g + 1) * GROUP_DIM)
            vn_ref[:, cols] = _rms(vg[:, cols], vgain_ref[:, cols]).astype(vn_ref.dtype)

    def query():
        q = _dot(state["hn"], w_in_ref[:, o:o + D_ATTN]) * (HEAD_DIM ** -0.5 * LOG2_E)
        q_ref[...] = q.astype(q_ref.dtype)

    def key_value():
        kv = _dot(state["hn"], w_in_ref[:, o + D_ATTN:o + D_ATTN + 2 * KV_W])
        lo = lax.broadcasted_iota(jnp.int32, (ts, LANES), 1) < HEAD_DIM
        for buf, t in ((kbuf, kv[:, 0:KV_W]), (vbuf, kv[:, KV_W:2 * KV_W])):
            tr = pltpu.roll(t, HEAD_DIM, axis=1)
            buf[0, BLOCK:BLOCK + ts, :] = jnp.where(lo, t, 0.0).astype(buf.dtype)
            buf[1, BLOCK:BLOCK + ts, :] = jnp.where(lo, 0.0, tr).astype(buf.dtype)
            buf[2, BLOCK:BLOCK + ts, :] = jnp.where(lo, tr, 0.0).astype(buf.dtype)
            buf[3, BLOCK:BLOCK + ts, :] = jnp.where(lo, 0.0, t).astype(buf.dtype)

    return [norm, gate_u, gate_v, query, key_value]


def _attend_pieces(i, src, seq_start, sink_ref, bias_ref, mix_ref, s_scr, p_scr):
    _, _, q_ref, kbuf, vbuf = src
    rows = slice(i * BLOCK, (i + 1) * BLOCK)
    keys = slice(i * BLOCK, (i + 2) * BLOCK)
    first = jnp.where(seq_start, 1, 0) if i == 0 else 0
    upper = lax.broadcasted_iota(jnp.int32, (2 * BLOCK, 1), 0) < BLOCK
    query = lax.broadcasted_iota(jnp.int32, (2 * BLOCK, BLOCK), 0) % BLOCK
    from_prev = lax.broadcasted_iota(jnp.int32, (2 * BLOCK, BLOCK), 1) > query
    stacks = [(g, parity) for g in range(N_KV_HEADS) for parity in range(2)]
    state = {}

    def scores():
        for g, parity in stacks:
            qs = jnp.concatenate([q_ref[rows, 2 * g * LANES:(2 * g + 1) * LANES],
                                  q_ref[rows, (2 * g + 1) * LANES:(2 * g + 2) * LANES]], axis=0)
            s = _dot_nt(qs, kbuf[_VARIANT[g][parity], keys, :])
            s = jnp.where(from_prev, s[:, 0:BLOCK], s[:, BLOCK:2 * BLOCK])
            s_scr[4 * i + 2 * g + parity] = (
                s + bias_ref[first, 2 * g + parity].reshape(2 * BLOCK, BLOCK))

    def softmax():
        zero = jnp.zeros((2 * BLOCK, BLOCK), jnp.bfloat16)
        for g, parity in stacks:
            head = Q_PER_KV * g + parity
            sink = jnp.where(upper, sink_ref[head], sink_ref[head + 2]) * LOG2_E
            s = s_scr[4 * i + 2 * g + parity]
            m = jnp.maximum(jnp.max(s, axis=-1, keepdims=True), sink)
            e = jnp.exp2(s - m)
            denom = jnp.sum(e, axis=-1, keepdims=True) + jnp.exp2(sink - m)
            p = e.astype(jnp.bfloat16)
            p_scr[4 * i + 2 * g + parity] = jnp.concatenate(
                [jnp.where(from_prev, p, zero), jnp.where(from_prev, zero, p)], axis=1)
            state["inv", g, parity] = 1.0 / denom

    def values():
        low = lax.broadcasted_iota(jnp.int32, (2 * BLOCK, LANES), 1) < HEAD_DIM
        for g in range(N_KV_HEADS):
            pair0 = D_GMLP + 2 * g * LANES
            acc = (_dot(p_scr[4 * i + 2 * g], vbuf[_VARIANT[g][0], keys, :])
                   + _dot(p_scr[4 * i + 2 * g + 1], vbuf[_VARIANT[g][1], keys, :]))
            acc = acc * jnp.where(low, state["inv", g, 0], state["inv", g, 1])
            mix_ref[rows, pair0:pair0 + LANES] = acc[0:BLOCK].astype(mix_ref.dtype)
            mix_ref[rows, pair0 + LANES:pair0 + 2 * LANES] = acc[BLOCK:].astype(mix_ref.dtype)

    return scores, softmax, values


def _spatial_gate(src, wsp_ref, bsp_ref, mix_ref):
    u_ref, vn_ref = src[0:2]
    t_idx = lax.broadcasted_iota(jnp.int32, (CHUNK, CHUNK), 0)
    s_idx = lax.broadcasted_iota(jnp.int32, (CHUNK, CHUNK), 1)
    causal = s_idx <= t_idx
    for g in range(GMLP_GROUPS):
        cols = slice(g * GROUP_DIM, (g + 1) * GROUP_DIM)
        w_g = jnp.where(causal, wsp_ref[g], 0.0).astype(jnp.bfloat16)
        b_g = bsp_ref[:, g:g + 1]
        for c in range(u_ref.shape[0] // CHUNK):
            rows = slice(c * CHUNK, (c + 1) * CHUNK)
            sv = _dot(w_g, vn_ref[rows, cols]) + b_g
            mix_ref[rows, cols] = (u_ref[rows, cols] * sv).astype(mix_ref.dtype)


N_CAST = 4
SCORES_AHEAD = 2


def _mixer_kernel(x_ref, xprev_ref, g1_ref, w_in_ref, vgain_ref, wsp_ref, bsp_ref, sink_ref,
                  rel_ref, w_out_ref, *rest, tiles_per_seq, n_tiles):
    cast_src, o_ref = rest[0:N_CAST], rest[N_CAST]
    cast_dst, scratch = rest[N_CAST + 1:2 * N_CAST + 1], rest[2 * N_CAST + 1:]
    slots = (scratch[0:5], scratch[5:10])
    mix_ref, bias_ref, s_scr, p_scr = scratch[10:14]
    ts = x_ref.shape[0]
    t = pl.program_id(0)

    cur_starts = t % tiles_per_seq == 0
    prev_started = (t + tiles_per_seq - 1) % tiles_per_seq == 0
    n_blk = ts // BLOCK

    def step(dst, src):
        project = _project_pieces(x_ref, g1_ref, w_in_ref, vgain_ref, dst) if dst else []
        attend = [_attend_pieces(i, src, prev_started, sink_ref, bias_ref, mix_ref, s_scr,
                                 p_scr) for i in range(n_blk)] if src else []
        if src:
            for i in range(SCORES_AHEAD):
                attend[i][0]()
            _spatial_gate(src, wsp_ref, bsp_ref, mix_ref)
        if dst:
            project.pop(0)()
        for i in range(n_blk):
            if src:
                if i + SCORES_AHEAD < n_blk:
                    attend[i + SCORES_AHEAD][0]()
                attend[i][1]()
            if dst:
                project.pop(0)()
            if src:
                attend[i][2]()
        assert not project
        if src:
            o_ref[...] = xprev_ref[...] + _dot(mix_ref[...], w_out_ref[...])
        for s, d in zip(cast_src, cast_dst):
            d[...] = s[...].astype(d.dtype)
        if dst:
            for k in (3, 4):
                zeros = jnp.zeros((4, BLOCK, LANES), dst[k].dtype)
                dst[k][:, 0:BLOCK, :] = (
                    jnp.where(cur_starts, zeros, src[k][:, ts:ts + BLOCK, :]) if src else zeros)

    @pl.when(t == 0)
    def _():
        _build_bias(rel_ref, bias_ref)
        step(slots[0], None)

    @pl.when(t == n_tiles)
    def _():
        step(None, slots[(n_tiles - 1) % 2])

    @pl.when((t > 0) & (t < n_tiles) & (t % 2 == 0))
    def _():
        step(slots[0], slots[1])

    @pl.when((t < n_tiles) & (t % 2 == 1))
    def _():
        step(slots[1], slots[0])


def _ffn_kernel(h_ref, p_ref, g2_ref, w1_ref, w2_ref, wproj_ref, wgate_ref, gf_ref, o_ref,
                acc_ref, *, final_norm, n_tiles):
    tm = h_ref.shape[0]
    parts = [slice(r, r + FFN_ROW_PART) for r in range(0, tm, FFN_ROW_PART)]

    def step(expand, finish):
        hidden = []

        def expand_chunk(c, hn_parts):
            cols = slice(c * FF_CHUNK, (c + 1) * FF_CHUNK)
            if c == 0:
                t = jnp.concatenate([_dot(part, w1_ref[:, cols]) for part in hn_parts], axis=0)
            else:
                t = _dot(jnp.concatenate(hn_parts, axis=0), w1_ref[:, cols])
            t = jnp.maximum(t, 0.0)
            hidden.append((t * t).astype(jnp.bfloat16))

        n_chunks = D_FF // FF_CHUNK
        assert len(parts) == n_chunks, "one finishing row part is issued per expand chunk"
        if expand:
            hn_parts = [_rms(h_ref[r, :], g2_ref[...]).astype(jnp.bfloat16) for r in parts]
        if finish:
            ple = _dot(p_ref[...].astype(jnp.bfloat16), wproj_ref[...])
        for c in range(n_chunks):
            if expand:
                expand_chunk(c, hn_parts)
            if finish:
                r = parts[c]
                z = _dot(acc_ref[r, :].astype(jnp.bfloat16), wgate_ref[...])
                out = acc_ref[r, :] + ple[r] / (1.0 + jnp.exp2(z * (-LOG2_E)))
                o_ref[r, :] = _rms(out, gf_ref[...]) if final_norm else out
        if expand:
            acc_ref[...] = h_ref[...] + _dot(jnp.concatenate(hidden, axis=1), w2_ref[...])

    t = pl.program_id(0)

    @pl.when(t == 0)
    def _():
        step(True, False)

    @pl.when((t > 0) & (t < n_tiles))
    def _():
        step(True, True)

    @pl.when(t == n_tiles)
    def _():
        step(False, True)


def _resident(shape):
    return pl.BlockSpec(shape, lambda *_: (0,) * len(shape), pipeline_mode=pl.Buffered(1))


def _cast_slice_spec(arr, axis, n_tiles):
    size = arr.shape[axis]
    min_thick = 16 if axis == 0 else LANES
    thick = max(size // n_tiles, min_thick)
    assert size % thick == 0
    last = size // thick - 1
    block = tuple(thick if a == axis else n for a, n in enumerate(arr.shape))
    return pl.BlockSpec(block, lambda t: tuple(jnp.minimum(t, last) if a == axis else 0
                                               for a in range(arr.ndim)))


def _mixer(x, g1, w_in, vgain, wsp, bsp_t, sinks, rel, w_out, cast):
    batch, seq, d = x.shape
    ts = SEQ_TILE
    assert ts // BLOCK == 4, "stage-1 pieces are paired one-to-one with attention blocks"
    assert len(cast) == N_CAST
    n_tiles = batch * seq // ts
    x2 = x.reshape(batch * seq, d)
    smem = pl.BlockSpec(memory_space=pltpu.SMEM)
    cast_specs = [_cast_slice_spec(w, axis, n_tiles) for w, axis in cast]
    stage_slot = [
        pltpu.VMEM((ts, D_GMLP), jnp.float32),
        pltpu.VMEM((ts, D_GMLP), jnp.bfloat16),
        pltpu.VMEM((ts, D_ATTN), jnp.bfloat16),
        pltpu.VMEM((4, BLOCK + ts, LANES), jnp.bfloat16),
        pltpu.VMEM((4, BLOCK + ts, LANES), jnp.bfloat16),
    ]
    out, *cast_out = pl.pallas_call(
        functools.partial(_mixer_kernel, tiles_per_seq=seq // ts, n_tiles=n_tiles),
        grid=(n_tiles + 1,),
        in_specs=[
            pl.BlockSpec((ts, d), lambda t: (jnp.minimum(t, n_tiles - 1), 0)),
            pl.BlockSpec((ts, d), lambda t: (jnp.maximum(t - 1, 0), 0)),
            _resident(g1.shape), _resident(w_in.shape), _resident(vgain.shape),
            _resident(wsp.shape), _resident(bsp_t.shape), smem, smem,
            _resident(w_out.shape),
        ] + cast_specs,
        out_specs=[pl.BlockSpec((ts, d), lambda t: (jnp.maximum(t - 1, 0), 0))] + cast_specs,
        out_shape=[jax.ShapeDtypeStruct(x2.shape, x2.dtype)]
        + [jax.ShapeDtypeStruct(w.shape, jnp.bfloat16) for w, _ in cast],
        scratch_shapes=stage_slot + stage_slot + [
            pltpu.VMEM((ts, D_MODEL), jnp.bfloat16),
            pltpu.VMEM((2, 4, 2, BLOCK, BLOCK), jnp.float32),
            pltpu.VMEM((16, 2 * BLOCK, BLOCK), jnp.float32),
            pltpu.VMEM((16, 2 * BLOCK, 2 * BLOCK), jnp.bfloat16),
        ],
        compiler_params=pltpu.CompilerParams(
            dimension_semantics=("arbitrary",),
            vmem_limit_bytes=VMEM_LIMIT_BYTES),
        name="mixer",
    )(x2, x2, g1, w_in, vgain, wsp, bsp_t, sinks, rel, w_out, *[w for w, _ in cast])
    return out.reshape(batch, seq, d), cast_out


def _ffn(h, p, g2, w1, w2, wproj, wgate, gf, *, final_norm):
    rows, d = h.shape
    tm = ROW_TILE
    n_tiles = rows // tm
    return pl.pallas_call(
        functools.partial(_ffn_kernel, final_norm=final_norm, n_tiles=n_tiles),
        grid=(n_tiles + 1,),
        in_specs=[
            pl.BlockSpec((tm, d), lambda t: (jnp.minimum(t, n_tiles - 1), 0)),
            pl.BlockSpec((tm, PLE_DIM), lambda t: (jnp.maximum(t - 1, 0), 0)),
            _resident(g2.shape), _resident(w1.shape), _resident(w2.shape),
            _resident(wproj.shape), _resident(wgate.shape), _resident(gf.shape),
        ],
        out_specs=pl.BlockSpec((tm, d), lambda t: (jnp.maximum(t - 1, 0), 0)),
        out_shape=jax.ShapeDtypeStruct(h.shape, h.dtype),
        scratch_shapes=[pltpu.VMEM((tm, d), jnp.float32)],
        compiler_params=pltpu.CompilerParams(
            dimension_semantics=("arbitrary",),
            vmem_limit_bytes=VMEM_LIMIT_BYTES),
        name="ffn",
    )(h, p, g2, w1, w2, wproj, wgate, gf)


def kernel(x, p, norm1_gain, w_in, gmlp_v_gain, w_spatial, b_spatial, attn_sinks, rel_bias_table, w_out, norm2_gain, w_ff1, w_ff2, w_ple_proj, w_ple_gate, final_gain):
    batch, seq, d = x.shape
    depth = w_in.shape[0]
    bf16 = jnp.bfloat16
    h = x
    for i in range(depth):
        h, (w1, w2, wproj, wgate) = _mixer(
            h, norm1_gain[i][None], w_in[i].astype(bf16), gmlp_v_gain[i][None],
            w_spatial[i], b_spatial[i].T, attn_sinks[i], rel_bias_table, w_out[i].astype(bf16),
            cast=[(w_ff1[i], 1), (w_ff2[i], 0), (w_ple_proj[i], 0), (w_ple_gate[i], 0)])
        h = _ffn(h.reshape(batch * seq, d), p[i].reshape(batch * seq, PLE_DIM),
                 norm2_gain[i][None], w1, w2, wproj, wgate,
                 final_gain[None], final_norm=(i == depth - 1)).reshape(batch, seq, d)
    return h
```

```python
import functools
import math

import jax
import jax.numpy as jnp
import numpy as np
from jax import lax
from jax.experimental import pallas as pl
from jax.experimental.pallas import tpu as pltpu

D_MODEL = 1024
PLE_DIM = 256
D_GMLP = 512
GMLP_GROUPS = 4
GROUP_DIM = D_GMLP // GMLP_GROUPS
CHUNK = 128
D_ATTN = 512
HEAD_DIM = 64
N_Q_HEADS = D_ATTN // HEAD_DIM
N_KV_HEADS = 2
Q_PER_KV = N_Q_HEADS // N_KV_HEADS
WINDOW = 128
BLOCK = WINDOW
REL_BUCKETS = 32
REL_MAX_DIST = 128
D_FF = 4 * D_MODEL
KV_W = N_KV_HEADS * HEAD_DIM
D_IN = 2 * D_GMLP + D_ATTN + 2 * KV_W
EPS = 1e-6
NEG_INF = -1e30
LOG2_E = math.log2(math.e)

LANES = 128
SEQ_TILE = 512
ROW_TILE = 1024
FF_CHUNK = 1024
FFN_ROW_PART = 256
WEIGHT_STAGE_ROWS = 256
VMEM_LIMIT_BYTES = 56 * 1024 * 1024


def _bucket_thresholds():
    max_exact = REL_BUCKETS // 2
    n = np.arange(WINDOW)
    nf = np.maximum(n, 1).astype(np.float32)
    large = max_exact + (np.log(nf / max_exact) / np.float32(math.log(REL_MAX_DIST / max_exact))
                         * (REL_BUCKETS - max_exact)).astype(np.int32)
    bucket = np.where(n < max_exact, n, np.minimum(large, REL_BUCKETS - 1))
    assert (np.diff(bucket) >= 0).all()
    return [(b, int(np.argmax(bucket >= b))) for b in range(1, REL_BUCKETS) if (bucket >= b).any()]


_BUCKET_THRESHOLDS = _bucket_thresholds()


def _rms(x, gain):
    return x * lax.rsqrt(jnp.mean(x * x, axis=-1, keepdims=True) + EPS) * gain


def _gelu_tanh(x):
    c = float(np.float32(np.sqrt(2 / np.pi)))
    poly = x * x * (-2.0 * c * 0.044715 * LOG2_E) + (-2.0 * c * LOG2_E)
    return x / (1.0 + jnp.exp2(x * poly))


def _dot(a, b):
    return jnp.dot(a, b, preferred_element_type=jnp.float32)


def _dot_nt(a, b):
    return lax.dot_general(a, b, (((1,), (1,)), ((), ())), preferred_element_type=jnp.float32)


def _build_bias(rel_ref, bias_ref):
    a = lax.broadcasted_iota(jnp.int32, (BLOCK, BLOCK), 0)
    j = lax.broadcasted_iota(jnp.int32, (BLOCK, BLOCK), 1)
    from_prev = j > a
    n = jnp.where(from_prev, BLOCK + a - j, a - j)

    def entry(b, h):
        return jnp.broadcast_to(rel_ref[b:b + 1, h:h + 1], (BLOCK, BLOCK))

    for h in range(N_Q_HEADS):
        val = entry(0, h)
        for b, thr in _BUCKET_THRESHOLDS:
            val = jnp.where(n >= thr, entry(b, h), val)
        stack = 2 * (h // Q_PER_KV) + h % 2
        half = (h % Q_PER_KV) // 2
        val = val * LOG2_E
        bias_ref[0, stack, half] = val
        bias_ref[1, stack, half] = jnp.where(from_prev, NEG_INF, val)


def _build_spatial_bias(bsp_ref, bm_ref):
    row = lax.broadcasted_iota(jnp.int32, (CHUNK, CHUNK), 0)
    padded = jnp.zeros((CHUNK, CHUNK), jnp.float32)
    for g in range(GMLP_GROUPS):
        padded = jnp.where(row == g, jnp.broadcast_to(bsp_ref[g:g + 1, :], (CHUNK, CHUNK)), padded)
    bm_ref[...] = padded.T


def _load_as_bf16(src_hbm, dst_ref, stage_ref, sem_ref, between=None):
    n_chunks = src_hbm.shape[0] // stage_ref.shape[1]
    rows = stage_ref.shape[1]

    def copy(k):
        return pltpu.make_async_copy(src_hbm.at[pl.ds(k * rows, rows), :],
                                     stage_ref.at[k % 2], sem_ref.at[k % 2])

    copy(0).start()
    if between is not None:
        between()
    for k in range(n_chunks):
        if k + 1 < n_chunks:
            copy(k + 1).start()
        copy(k).wait()
        dst_ref[k * rows:(k + 1) * rows, :] = stage_ref[k % 2].astype(dst_ref.dtype)


_VARIANT = ((0, 1), (2, 3))


def _project_pieces(x_ref, g1_ref, w_in_ref, vgain_ref, dst):
    u_ref, vn_ref, q_ref, kbuf, vbuf = dst
    ts = x_ref.shape[0]
    o = 2 * D_GMLP
    state = {}

    def norm():
        state["hn"] = _rms(x_ref[...], g1_ref[...]).astype(jnp.bfloat16)

    def gate_u():
        u_ref[...] = _gelu_tanh(_dot(state["hn"], w_in_ref[:, 0:D_GMLP]))

    def gate_v():
        vg = _gelu_tanh(_dot(state["hn"], w_in_ref[:, D_GMLP:2 * D_GMLP]))
        for g in range(GMLP_GROUPS):
            cols = slice(g * GROUP_DIM, (g + 1) * GROUP_DIM)
            vn_ref[:, cols] = _rms(vg[:, cols], vgain_ref[:, cols]).astype(vn_ref.dtype)

    def query():
        q = _dot(state["hn"], w_in_ref[:, o:o + D_ATTN]) * (HEAD_DIM ** -0.5 * LOG2_E)
        q_ref[...] = q.astype(q_ref.dtype)

    def key_value():
        kv = _dot(state["hn"], w_in_ref[:, o + D_ATTN:o + D_ATTN + 2 * KV_W])
        lo = lax.broadcasted_iota(jnp.int32, (ts, LANES), 1) < HEAD_DIM
        for buf, t in ((kbuf, kv[:, 0:KV_W]), (vbuf, kv[:, KV_W:2 * KV_W])):
            tr = pltpu.roll(t, HEAD_DIM, axis=1)
            buf[0, BLOCK:BLOCK + ts, :] = jnp.where(lo, t, 0.0).astype(buf.dtype)
            buf[1, BLOCK:BLOCK + ts, :] = jnp.where(lo, 0.0, tr).astype(buf.dtype)
            buf[2, BLOCK:BLOCK + ts, :] = jnp.where(lo, tr, 0.0).astype(buf.dtype)
            buf[3, BLOCK:BLOCK + ts, :] = jnp.where(lo, 0.0, t).astype(buf.dtype)

    return [norm, gate_u, gate_v, query, key_value]


def _attend_pieces(i, src, seq_start, sink_ref, bias_ref, mix_ref, s_scr, p_scr):
    _, _, q_ref, kbuf, vbuf = src
    rows = slice(i * BLOCK, (i + 1) * BLOCK)
    keys = slice(i * BLOCK, (i + 2) * BLOCK)
    first = jnp.where(seq_start, 1, 0) if i == 0 else 0
    upper = lax.broadcasted_iota(jnp.int32, (2 * BLOCK, 1), 0) < BLOCK
    query = lax.broadcasted_iota(jnp.int32, (2 * BLOCK, BLOCK), 0) % BLOCK
    from_prev = lax.broadcasted_iota(jnp.int32, (2 * BLOCK, BLOCK), 1) > query
    stacks = [(g, parity) for g in range(N_KV_HEADS) for parity in range(2)]
    state = {}

    def scores():
        for g, parity in stacks:
            qs = jnp.concatenate([q_ref[rows, 2 * g * LANES:(2 * g + 1) * LANES],
                                  q_ref[rows, (2 * g + 1) * LANES:(2 * g + 2) * LANES]], axis=0)
            s = _dot_nt(qs, kbuf[_VARIANT[g][parity], keys, :])
            s = jnp.where(from_prev, s[:, 0:BLOCK], s[:, BLOCK:2 * BLOCK])
            s_scr[4 * i + 2 * g + parity] = (
                s + bias_ref[first, 2 * g + parity].reshape(2 * BLOCK, BLOCK))

    def softmax():
        zero = jnp.zeros((2 * BLOCK, BLOCK), jnp.bfloat16)
        for g, parity in stacks:
            head = Q_PER_KV * g + parity
            sink = jnp.where(upper, sink_ref[head], sink_ref[head + 2]) * LOG2_E
            s = s_scr[4 * i + 2 * g + parity]
            m = jnp.maximum(jnp.max(s, axis=-1, keepdims=True), sink)
            e = jnp.exp2(s - m)
            denom = jnp.sum(e, axis=-1, keepdims=True) + jnp.exp2(sink - m)
            p = e.astype(jnp.bfloat16)
            p_scr[4 * i + 2 * g + parity] = jnp.concatenate(
                [jnp.where(from_prev, p, zero), jnp.where(from_prev, zero, p)], axis=1)
            state["inv", g, parity] = 1.0 / denom

    def values():
        low = lax.broadcasted_iota(jnp.int32, (2 * BLOCK, LANES), 1) < HEAD_DIM
        for g in range(N_KV_HEADS):
            pair0 = D_GMLP + 2 * g * LANES
            acc = (_dot(p_scr[4 * i + 2 * g], vbuf[_VARIANT[g][0], keys, :])
                   + _dot(p_scr[4 * i + 2 * g + 1], vbuf[_VARIANT[g][1], keys, :]))
            acc = acc * jnp.where(low, state["inv", g, 0], state["inv", g, 1])
            mix_ref[rows, pair0:pair0 + LANES] = acc[0:BLOCK].astype(mix_ref.dtype)
            mix_ref[rows, pair0 + LANES:pair0 + 2 * LANES] = acc[BLOCK:].astype(mix_ref.dtype)

    return scores, softmax, values


def _spatial_gate(src, wsp_ref, bm_ref, mix_ref):
    u_ref, vn_ref = src[0:2]
    t_idx = lax.broadcasted_iota(jnp.int32, (CHUNK, CHUNK), 0)
    s_idx = lax.broadcasted_iota(jnp.int32, (CHUNK, CHUNK), 1)
    causal = s_idx <= t_idx
    for g in range(GMLP_GROUPS):
        cols = slice(g * GROUP_DIM, (g + 1) * GROUP_DIM)
        w_g = jnp.where(causal, wsp_ref[g], 0.0).astype(jnp.bfloat16)
        b_g = bm_ref[:, g:g + 1]
        for c in range(u_ref.shape[0] // CHUNK):
            rows = slice(c * CHUNK, (c + 1) * CHUNK)
            sv = _dot(w_g, vn_ref[rows, cols]) + b_g
            mix_ref[rows, cols] = (u_ref[rows, cols] * sv).astype(mix_ref.dtype)


N_CAST = 4
SCORES_AHEAD = 2


def _mixer_kernel(x_ref, xprev_ref, g1_ref, w_in_hbm, vgain_ref, wsp_ref, bsp_ref, sink_ref,
                  rel_ref, w_out_hbm, *rest, tiles_per_seq, n_tiles):
    cast_src, o_ref = rest[0:N_CAST], rest[N_CAST]
    cast_dst, scratch = rest[N_CAST + 1:2 * N_CAST + 1], rest[2 * N_CAST + 1:]
    slots = (scratch[0:5], scratch[5:10])
    mix_ref, bias_ref, s_scr, p_scr = scratch[10:14]
    w_in_ref, w_out_ref, stage_in, stage_out, bm_ref, dma_sem = scratch[14:20]
    ts = x_ref.shape[0]
    t = pl.program_id(0)

    cur_starts = t % tiles_per_seq == 0
    prev_started = (t + tiles_per_seq - 1) % tiles_per_seq == 0
    n_blk = ts // BLOCK

    def step(dst, src):
        project = _project_pieces(x_ref, g1_ref, w_in_ref, vgain_ref, dst) if dst else []
        attend = [_attend_pieces(i, src, prev_started, sink_ref, bias_ref, mix_ref, s_scr,
                                 p_scr) for i in range(n_blk)] if src else []
        if src:
            for i in range(SCORES_AHEAD):
                attend[i][0]()
            _spatial_gate(src, wsp_ref, bm_ref, mix_ref)
        if dst:
            project.pop(0)()
        for i in range(n_blk):
            if src:
                if i + SCORES_AHEAD < n_blk:
                    attend[i + SCORES_AHEAD][0]()
                attend[i][1]()
            if dst:
                project.pop(0)()
            if src:
                attend[i][2]()
        assert not project
        if src:
            o_ref[...] = xprev_ref[...] + _dot(mix_ref[...], w_out_ref[...])
        for s, d in zip(cast_src, cast_dst):
            d[...] = s[...].astype(d.dtype)
        if dst:
            for k in (3, 4):
                zeros = jnp.zeros((4, BLOCK, LANES), dst[k].dtype)
                dst[k][:, 0:BLOCK, :] = (
                    jnp.where(cur_starts, zeros, src[k][:, ts:ts + BLOCK, :]) if src else zeros)

    @pl.when(t == 0)
    def _():
        def build_tables():
            _build_bias(rel_ref, bias_ref)
            _build_spatial_bias(bsp_ref, bm_ref)

        _load_as_bf16(w_in_hbm, w_in_ref, stage_in, dma_sem.at[0], between=build_tables)
        _load_as_bf16(w_out_hbm, w_out_ref, stage_out, dma_sem.at[1])
        step(slots[0], None)

    @pl.when(t == n_tiles)
    def _():
        step(None, slots[(n_tiles - 1) % 2])

    @pl.when((t > 0) & (t < n_tiles) & (t % 2 == 0))
    def _():
        step(slots[0], slots[1])

    @pl.when((t < n_tiles) & (t % 2 == 1))
    def _():
        step(slots[1], slots[0])


def _ffn_kernel(h_ref, p_ref, g2_ref, w1_ref, w2_ref, wproj_ref, wgate_ref, gf_ref, o_ref, *,
                final_norm):
    tm = h_ref.shape[0]
    halves = [slice(r, r + FFN_ROW_PART) for r in range(0, tm, FFN_ROW_PART)]
    hn_halves = [_rms(h_ref[r, :], g2_ref[...]).astype(jnp.bfloat16) for r in halves]
    hn = jnp.concatenate(hn_halves, axis=0)
    acc = h_ref[...]
    hidden = []
    for c in range(D_FF // FF_CHUNK):
        cols = slice(c * FF_CHUNK, (c + 1) * FF_CHUNK)
        if c == 0:
            t = jnp.concatenate([_dot(part, w1_ref[:, cols]) for part in hn_halves], axis=0)
        else:
            t = _dot(hn, w1_ref[:, cols])
        t = jnp.maximum(t, 0.0)
        hidden.append((t * t).astype(jnp.bfloat16))
    acc = acc + _dot(jnp.concatenate(hidden, axis=1), w2_ref[...])
    acc_bf16 = acc.astype(jnp.bfloat16)
    gates = [1.0 / (1.0 + jnp.exp2(_dot(acc_bf16[r], wgate_ref[...]) * (-LOG2_E)))
             for r in halves]
    ple = _dot(p_ref[...].astype(jnp.bfloat16), wproj_ref[...])
    for r, gate in zip(halves, gates):
        out = acc[r] + gate * ple[r]
        o_ref[r, :] = _rms(out, gf_ref[...]) if final_norm else out


def _resident(shape):
    return pl.BlockSpec(shape, lambda *_: (0,) * len(shape), pipeline_mode=pl.Buffered(1))


def _cast_slice_spec(arr, axis, n_tiles):
    size = arr.shape[axis]
    min_thick = 16 if axis == 0 else LANES
    thick = max(size // n_tiles, min_thick)
    assert size % thick == 0
    last = size // thick - 1
    block = tuple(thick if a == axis else n for a, n in enumerate(arr.shape))
    return pl.BlockSpec(block, lambda t: tuple(jnp.minimum(t, last) if a == axis else 0
                                               for a in range(arr.ndim)))


def _mixer(x, g1, w_in, vgain, wsp, bsp, sinks, rel, w_out, cast):
    batch, seq, d = x.shape
    ts = SEQ_TILE
    assert ts // BLOCK == 4, "stage-1 pieces are paired one-to-one with attention blocks"
    assert len(cast) == N_CAST
    n_tiles = batch * seq // ts
    x2 = x.reshape(batch * seq, d)
    smem = pl.BlockSpec(memory_space=pltpu.SMEM)
    hbm = pl.BlockSpec(memory_space=pl.ANY)
    cast_specs = [_cast_slice_spec(w, axis, n_tiles) for w, axis in cast]
    stage_slot = [
        pltpu.VMEM((ts, D_GMLP), jnp.float32),
        pltpu.VMEM((ts, D_GMLP), jnp.bfloat16),
        pltpu.VMEM((ts, D_ATTN), jnp.bfloat16),
        pltpu.VMEM((4, BLOCK + ts, LANES), jnp.bfloat16),
        pltpu.VMEM((4, BLOCK + ts, LANES), jnp.bfloat16),
    ]
    out, *cast_out = pl.pallas_call(
        functools.partial(_mixer_kernel, tiles_per_seq=seq // ts, n_tiles=n_tiles),
        grid=(n_tiles + 1,),
        in_specs=[
            pl.BlockSpec((ts, d), lambda t: (jnp.minimum(t, n_tiles - 1), 0)),
            pl.BlockSpec((ts, d), lambda t: (jnp.maximum(t - 1, 0), 0)),
            _resident(g1.shape), hbm, _resident(vgain.shape),
            _resident(wsp.shape), _resident(bsp.shape), smem, _resident(rel.shape),
            hbm,
        ] + cast_specs,
        out_specs=[pl.BlockSpec((ts, d), lambda t: (jnp.maximum(t - 1, 0), 0))] + cast_specs,
        out_shape=[jax.ShapeDtypeStruct(x2.shape, x2.dtype)]
        + [jax.ShapeDtypeStruct(w.shape, jnp.bfloat16) for w, _ in cast],
        scratch_shapes=stage_slot + stage_slot + [
            pltpu.VMEM((ts, D_MODEL), jnp.bfloat16),
            pltpu.VMEM((2, 4, 2, BLOCK, BLOCK), jnp.float32),
            pltpu.VMEM((16, 2 * BLOCK, BLOCK), jnp.float32),
            pltpu.VMEM((16, 2 * BLOCK, 2 * BLOCK), jnp.bfloat16),
            pltpu.VMEM(w_in.shape, jnp.bfloat16),
            pltpu.VMEM(w_out.shape, jnp.bfloat16),
            pltpu.VMEM((2, WEIGHT_STAGE_ROWS, w_in.shape[1]), jnp.float32),
            pltpu.VMEM((2, WEIGHT_STAGE_ROWS, w_out.shape[1]), jnp.float32),
            pltpu.VMEM((CHUNK, LANES), jnp.float32),
            pltpu.SemaphoreType.DMA((2, 2)),
        ],
        compiler_params=pltpu.CompilerParams(
            dimension_semantics=("arbitrary",),
            vmem_limit_bytes=VMEM_LIMIT_BYTES),
        name="mixer",
    )(x2, x2, g1, w_in, vgain, wsp, bsp, sinks, rel, w_out, *[w for w, _ in cast])
    return out.reshape(batch, seq, d), cast_out


def _ffn(h, p, g2, w1, w2, wproj, wgate, gf, *, final_norm):
    rows, d = h.shape
    tm = ROW_TILE
    return pl.pallas_call(
        functools.partial(_ffn_kernel, final_norm=final_norm),
        grid=(rows // tm,),
        in_specs=[
            pl.BlockSpec((tm, d), lambda i: (i, 0)),
            pl.BlockSpec((tm, PLE_DIM), lambda i: (i, 0)),
            _resident(g2.shape), _resident(w1.shape), _resident(w2.shape),
            _resident(wproj.shape), _resident(wgate.shape), _resident(gf.shape),
        ],
        out_specs=pl.BlockSpec((tm, d), lambda i: (i, 0)),
        out_shape=jax.ShapeDtypeStruct(h.shape, h.dtype),
        compiler_params=pltpu.CompilerParams(
            dimension_semantics=("arbitrary",),
            vmem_limit_bytes=VMEM_LIMIT_BYTES),
        name="ffn",
    )(h, p, g2, w1, w2, wproj, wgate, gf)


def kernel(x, p, norm1_gain, w_in, gmlp_v_gain, w_spatial, b_spatial, attn_sinks, rel_bias_table, w_out, norm2_gain, w_ff1, w_ff2, w_ple_proj, w_ple_gate, final_gain):
    batch, seq, d = x.shape
    depth = w_in.shape[0]
    bf16 = jnp.bfloat16
    h = x
    for i in range(depth):
        h, (w1, w2, wproj, wgate) = _mixer(
            h, norm1_gain[i][None], w_in[i], gmlp_v_gain[i][None],
            w_spatial[i], b_spatial[i], attn_sinks[i], rel_bias_table, w_out[i],
            cast=[(w_ff1[i], 1), (w_ff2[i], 0), (w_ple_proj[i], 0), (w_ple_gate[i], 0)])
        h = _ffn(h.reshape(batch * seq, d), p[i].reshape(batch * seq, PLE_DIM),
                 norm2_gain[i][None], w1, w2, wproj, wgate,
                 final_gain[None], final_norm=(i == depth - 1)).reshape(batch, seq, d)
    return h
```

```python
import functools
import math

import jax
import jax.numpy as jnp
import numpy as np
from jax import lax
from jax.experimental import pallas as pl
from jax.experimental.pallas import tpu as pltpu

D_MODEL = 1024
PLE_DIM = 256
D_GMLP = 512
GMLP_GROUPS = 4
GROUP_DIM = D_GMLP // GMLP_GROUPS
CHUNK = 128
D_ATTN = 512
HEAD_DIM = 64
N_Q_HEADS = D_ATTN // HEAD_DIM
N_KV_HEADS = 2
Q_PER_KV = N_Q_HEADS // N_KV_HEADS
WINDOW = 128
BLOCK = WINDOW
REL_BUCKETS = 32
REL_MAX_DIST = 128
D_FF = 4 * D_MODEL
KV_W = N_KV_HEADS * HEAD_DIM
D_IN = 2 * D_GMLP + D_ATTN + 2 * KV_W
EPS = 1e-6
NEG_INF = -1e30
LOG2_E = math.log2(math.e)

LANES = 128
SEQ_TILE = 512
ROW_TILE = 1024
FF_CHUNK = 1024
FFN_ROW_PART = 256
WEIGHT_STAGE_ROWS = 256
VMEM_LIMIT_BYTES = 56 * 1024 * 1024


def _bucket_thresholds():
    max_exact = REL_BUCKETS // 2
    n = np.arange(WINDOW)
    nf = np.maximum(n, 1).astype(np.float32)
    large = max_exact + (np.log(nf / max_exact) / np.float32(math.log(REL_MAX_DIST / max_exact))
                         * (REL_BUCKETS - max_exact)).astype(np.int32)
    bucket = np.where(n < max_exact, n, np.minimum(large, REL_BUCKETS - 1))
    assert (np.diff(bucket) >= 0).all()
    return [(b, int(np.argmax(bucket >= b))) for b in range(1, REL_BUCKETS) if (bucket >= b).any()]


_BUCKET_THRESHOLDS = _bucket_thresholds()


def _rms(x, gain):
    return x * lax.rsqrt(jnp.mean(x * x, axis=-1, keepdims=True) + EPS) * gain


def _gelu_tanh(x):
    c = float(np.float32(np.sqrt(2 / np.pi)))
    poly = x * x * (-2.0 * c * 0.044715 * LOG2_E) + (-2.0 * c * LOG2_E)
    return x / (1.0 + jnp.exp2(x * poly))


def _dot(a, b):
    return jnp.dot(a, b, preferred_element_type=jnp.float32)


def _dot_nt(a, b):
    return lax.dot_general(a, b, (((1,), (1,)), ((), ())), preferred_element_type=jnp.float32)


def _build_bias(rel_ref, bias_ref):
    a = lax.broadcasted_iota(jnp.int32, (BLOCK, BLOCK), 0)
    j = lax.broadcasted_iota(jnp.int32, (BLOCK, BLOCK), 1)
    from_prev = j > a
    n = jnp.where(from_prev, BLOCK + a - j, a - j)

    def entry(b, h):
        return jnp.broadcast_to(rel_ref[h:h + 1, b:b + 1], (BLOCK, BLOCK))

    for h in range(N_Q_HEADS):
        val = entry(0, h)
        for b, thr in _BUCKET_THRESHOLDS:
            val = jnp.where(n >= thr, entry(b, h), val)
        stack = 2 * (h // Q_PER_KV) + h % 2
        half = (h % Q_PER_KV) // 2
        val = val * LOG2_E
        bias_ref[0, stack, half] = val
        bias_ref[1, stack, half] = jnp.where(from_prev, NEG_INF, val)


def _build_spatial_bias(bsp_ref, bm_ref):
    row = lax.broadcasted_iota(jnp.int32, (CHUNK, CHUNK), 0)
    padded = jnp.zeros((CHUNK, CHUNK), jnp.float32)
    for g in range(GMLP_GROUPS):
        padded = jnp.where(row == g, jnp.broadcast_to(bsp_ref[g:g + 1, :], (CHUNK, CHUNK)), padded)
    bm_ref[...] = padded.T


def _load_as_bf16(src_hbm, dst_ref, stage_ref, sem_ref, sem_base, between=None):
    n_chunks = src_hbm.shape[0] // stage_ref.shape[1]
    rows = stage_ref.shape[1]

    def copy(k):
        return pltpu.make_async_copy(src_hbm.at[pl.ds(k * rows, rows), :],
                                     stage_ref.at[k % 2], sem_ref.at[sem_base + k % 2])

    copy(0).start()
    if between is not None:
        between()
    for k in range(n_chunks):
        if k + 1 < n_chunks:
            copy(k + 1).start()
        copy(k).wait()
        dst_ref[k * rows:(k + 1) * rows, :] = stage_ref[k % 2].astype(dst_ref.dtype)


_VARIANT = ((0, 1), (2, 3))


def _project_pieces(x_ref, g1_ref, w_in_ref, vgain_ref, dst):
    u_ref, vn_ref, q_ref, kbuf, vbuf = dst
    ts = x_ref.shape[0]
    o = 2 * D_GMLP
    state = {}

    def norm():
        state["hn"] = _rms(x_ref[...], g1_ref[...]).astype(jnp.bfloat16)

    def gate_u():
        u_ref[...] = _gelu_tanh(_dot(state["hn"], w_in_ref[:, 0:D_GMLP]))

    def gate_v():
        vg = _gelu_tanh(_dot(state["hn"], w_in_ref[:, D_GMLP:2 * D_GMLP]))
        for g in range(GMLP_GROUPS):
            cols = slice(g * GROUP_DIM, (g + 1) * GROUP_DIM)
            vn_ref[:, cols] = _rms(vg[:, cols], vgain_ref[:, cols]).astype(vn_ref.dtype)

    def query():
        q = _dot(state["hn"], w_in_ref[:, o:o + D_ATTN]) * (HEAD_DIM ** -0.5 * LOG2_E)
        q_ref[...] = q.astype(q_ref.dtype)

    def key_value():
        kv = _dot(state["hn"], w_in_ref[:, o + D_ATTN:o + D_ATTN + 2 * KV_W])
        lo = lax.broadcasted_iota(jnp.int32, (ts, LANES), 1) < HEAD_DIM
        for buf, t in ((kbuf, kv[:, 0:KV_W]), (vbuf, kv[:, KV_W:2 * KV_W])):
            tr = pltpu.roll(t, HEAD_DIM, axis=1)
            buf[0, BLOCK:BLOCK + ts, :] = jnp.where(lo, t, 0.0).astype(buf.dtype)
            buf[1, BLOCK:BLOCK + ts, :] = jnp.where(lo, 0.0, tr).astype(buf.dtype)
            buf[2, BLOCK:BLOCK + ts, :] = jnp.where(lo, tr, 0.0).astype(buf.dtype)
            buf[3, BLOCK:BLOCK + ts, :] = jnp.where(lo, 0.0, t).astype(buf.dtype)

    return [norm, gate_u, gate_v, query, key_value]


def _attend_pieces(i, src, seq_start, sink_ref, bias_ref, mix_ref, s_scr, p_scr):
    _, _, q_ref, kbuf, vbuf = src
    rows = slice(i * BLOCK, (i + 1) * BLOCK)
    keys = slice(i * BLOCK, (i + 2) * BLOCK)
    first = jnp.where(seq_start, 1, 0) if i == 0 else 0
    upper = lax.broadcasted_iota(jnp.int32, (2 * BLOCK, 1), 0) < BLOCK
    query = lax.broadcasted_iota(jnp.int32, (2 * BLOCK, BLOCK), 0) % BLOCK
    from_prev = lax.broadcasted_iota(jnp.int32, (2 * BLOCK, BLOCK), 1) > query
    stacks = [(g, parity) for g in range(N_KV_HEADS) for parity in range(2)]
    state = {}

    def scores():
        for g, parity in stacks:
            qs = jnp.concatenate([q_ref[rows, 2 * g * LANES:(2 * g + 1) * LANES],
                                  q_ref[rows, (2 * g + 1) * LANES:(2 * g + 2) * LANES]], axis=0)
            s = _dot_nt(qs, kbuf[_VARIANT[g][parity], keys, :])
            s = jnp.where(from_prev, s[:, 0:BLOCK], s[:, BLOCK:2 * BLOCK])
            s_scr[4 * i + 2 * g + parity] = (
                s + bias_ref[first, 2 * g + parity].reshape(2 * BLOCK, BLOCK))

    def softmax():
        zero = jnp.zeros((2 * BLOCK, BLOCK), jnp.bfloat16)
        for g, parity in stacks:
            head = Q_PER_KV * g + parity
            sink = jnp.where(upper, sink_ref[head], sink_ref[head + 2]) * LOG2_E
            s = s_scr[4 * i + 2 * g + parity]
            m = jnp.maximum(jnp.max(s, axis=-1, keepdims=True), sink)
            e = jnp.exp2(s - m)
            denom = jnp.sum(e, axis=-1, keepdims=True) + jnp.exp2(sink - m)
            p = e.astype(jnp.bfloat16)
            p_scr[4 * i + 2 * g + parity] = jnp.concatenate(
                [jnp.where(from_prev, p, zero), jnp.where(from_prev, zero, p)], axis=1)
            state["inv", g, parity] = 1.0 / denom

    def values():
        low = lax.broadcasted_iota(jnp.int32, (2 * BLOCK, LANES), 1) < HEAD_DIM
        for g in range(N_KV_HEADS):
            pair0 = D_GMLP + 2 * g * LANES
            acc = (_dot(p_scr[4 * i + 2 * g], vbuf[_VARIANT[g][0], keys, :])
                   + _dot(p_scr[4 * i + 2 * g + 1], vbuf[_VARIANT[g][1], keys, :]))
            acc = acc * jnp.where(low, state["inv", g, 0], state["inv", g, 1])
            mix_ref[rows, pair0:pair0 + LANES] = acc[0:BLOCK].astype(mix_ref.dtype)
            mix_ref[rows, pair0 + LANES:pair0 + 2 * LANES] = acc[BLOCK:].astype(mix_ref.dtype)

    return scores, softmax, values


def _spatial_gate(src, wsp_ref, bm_ref, mix_ref):
    u_ref, vn_ref = src[0:2]
    t_idx = lax.broadcasted_iota(jnp.int32, (CHUNK, CHUNK), 0)
    s_idx = lax.broadcasted_iota(jnp.int32, (CHUNK, CHUNK), 1)
    causal = s_idx <= t_idx
    for g in range(GMLP_GROUPS):
        cols = slice(g * GROUP_DIM, (g + 1) * GROUP_DIM)
        w_g = jnp.where(causal, wsp_ref[g], 0.0).astype(jnp.bfloat16)
        b_g = bm_ref[:, g:g + 1]
        for c in range(u_ref.shape[0] // CHUNK):
            rows = slice(c * CHUNK, (c + 1) * CHUNK)
            sv = _dot(w_g, vn_ref[rows, cols]) + b_g
            mix_ref[rows, cols] = (u_ref[rows, cols] * sv).astype(mix_ref.dtype)


N_CAST = 4
SCORES_AHEAD = 2


def _mixer_kernel(x_ref, xprev_ref, g1_ref, w_in_hbm, vgain_ref, wsp_ref, bsp_ref, sink_ref,
                  rel_ref, w_out_hbm, *rest, tiles_per_seq, n_tiles):
    cast_src, o_ref = rest[0:N_CAST], rest[N_CAST]
    cast_dst, scratch = rest[N_CAST + 1:2 * N_CAST + 1], rest[2 * N_CAST + 1:]
    slots = (scratch[0:5], scratch[5:10])
    mix_ref, bias_ref, s_scr, p_scr = scratch[10:14]
    w_in_ref, w_out_ref, stage_in, stage_out, bm_ref, dma_sem = scratch[14:20]
    ts = x_ref.shape[0]
    t = pl.program_id(0)

    cur_starts = t % tiles_per_seq == 0
    prev_started = (t + tiles_per_seq - 1) % tiles_per_seq == 0
    n_blk = ts // BLOCK

    def step(dst, src):
        project = _project_pieces(x_ref, g1_ref, w_in_ref, vgain_ref, dst) if dst else []
        attend = [_attend_pieces(i, src, prev_started, sink_ref, bias_ref, mix_ref, s_scr,
                                 p_scr) for i in range(n_blk)] if src else []
        if src:
            for i in range(SCORES_AHEAD):
                attend[i][0]()
            _spatial_gate(src, wsp_ref, bm_ref, mix_ref)
        if dst:
            project.pop(0)()
        for i in range(n_blk):
            if src:
                if i + SCORES_AHEAD < n_blk:
                    attend[i + SCORES_AHEAD][0]()
                attend[i][1]()
            if dst:
                project.pop(0)()
            if src:
                attend[i][2]()
        assert not project
        if src:
            o_ref[...] = xprev_ref[...] + _dot(mix_ref[...], w_out_ref[...])
        for s, d in zip(cast_src, cast_dst):
            d[...] = s[...].astype(d.dtype)
        if dst:
            for k in (3, 4):
                zeros = jnp.zeros((4, BLOCK, LANES), dst[k].dtype)
                dst[k][:, 0:BLOCK, :] = (
                    jnp.where(cur_starts, zeros, src[k][:, ts:ts + BLOCK, :]) if src else zeros)

    @pl.when(t == 0)
    def _():
        def build_tables():
            _build_bias(rel_ref, bias_ref)
            _build_spatial_bias(bsp_ref, bm_ref)

        _load_as_bf16(w_in_hbm, w_in_ref, stage_in, dma_sem, 0, between=build_tables)
        _load_as_bf16(w_out_hbm, w_out_ref, stage_out, dma_sem, 2)
        step(slots[0], None)

    @pl.when(t == n_tiles)
    def _():
        step(None, slots[(n_tiles - 1) % 2])

    @pl.when((t > 0) & (t < n_tiles) & (t % 2 == 0))
    def _():
        step(slots[0], slots[1])

    @pl.when((t < n_tiles) & (t % 2 == 1))
    def _():
        step(slots[1], slots[0])


def _ffn_kernel(h_ref, p_ref, g2_ref, w1_ref, w2_ref, wproj_ref, wgate_ref, gf_ref, o_ref, *,
                final_norm):
    tm = h_ref.shape[0]
    halves = [slice(r, r + FFN_ROW_PART) for r in range(0, tm, FFN_ROW_PART)]
    hn_halves = [_rms(h_ref[r, :], g2_ref[...]).astype(jnp.bfloat16) for r in halves]
    hn = jnp.concatenate(hn_halves, axis=0)
    acc = h_ref[...]
    hidden = []
    for c in range(D_FF // FF_CHUNK):
        cols = slice(c * FF_CHUNK, (c + 1) * FF_CHUNK)
        if c == 0:
            t = jnp.concatenate([_dot(part, w1_ref[:, cols]) for part in hn_halves], axis=0)
        else:
            t = _dot(hn, w1_ref[:, cols])
        t = jnp.maximum(t, 0.0)
        hidden.append((t * t).astype(jnp.bfloat16))
    acc = acc + _dot(jnp.concatenate(hidden, axis=1), w2_ref[...])
    acc_bf16 = acc.astype(jnp.bfloat16)
    gates = [1.0 / (1.0 + jnp.exp2(_dot(acc_bf16[r], wgate_ref[...]) * (-LOG2_E)))
             for r in halves]
    ple = _dot(p_ref[...].astype(jnp.bfloat16), wproj_ref[...])
    for r, gate in zip(halves, gates):
        out = acc[r] + gate * ple[r]
        o_ref[r, :] = _rms(out, gf_ref[...]) if final_norm else out


def _resident(shape):
    return pl.BlockSpec(shape, lambda *_: (0,) * len(shape), pipeline_mode=pl.Buffered(1))


def _cast_slice_spec(arr, axis, n_tiles):
    size = arr.shape[axis]
    min_thick = 16 if axis == 0 else LANES
    thick = max(size // n_tiles, min_thick)
    assert size % thick == 0
    last = size // thick - 1
    block = tuple(thick if a == axis else n for a, n in enumerate(arr.shape))
    return pl.BlockSpec(block, lambda t: tuple(jnp.minimum(t, last) if a == axis else 0
                                               for a in range(arr.ndim)))


def _mixer(x, g1, w_in, vgain, wsp, bsp, sinks, rel, w_out, cast):
    batch, seq, d = x.shape
    ts = SEQ_TILE
    assert ts // BLOCK == 4, "stage-1 pieces are paired one-to-one with attention blocks"
    assert len(cast) == N_CAST
    n_tiles = batch * seq // ts
    x2 = x.reshape(batch * seq, d)
    smem = pl.BlockSpec(memory_space=pltpu.SMEM)
    hbm = pl.BlockSpec(memory_space=pl.ANY)
    cast_specs = [_cast_slice_spec(w, axis, n_tiles) for w, axis in cast]
    stage_slot = [
        pltpu.VMEM((ts, D_GMLP), jnp.float32),
        pltpu.VMEM((ts, D_GMLP), jnp.bfloat16),
        pltpu.VMEM((ts, D_ATTN), jnp.bfloat16),
        pltpu.VMEM((4, BLOCK + ts, LANES), jnp.bfloat16),
        pltpu.VMEM((4, BLOCK + ts, LANES), jnp.bfloat16),
    ]
    out, *cast_out = pl.pallas_call(
        functools.partial(_mixer_kernel, tiles_per_seq=seq // ts, n_tiles=n_tiles),
        grid=(n_tiles + 1,),
        in_specs=[
            pl.BlockSpec((ts, d), lambda t: (jnp.minimum(t, n_tiles - 1), 0)),
            pl.BlockSpec((ts, d), lambda t: (jnp.maximum(t - 1, 0), 0)),
            _resident(g1.shape), hbm, _resident(vgain.shape),
            _resident(wsp.shape), _resident(bsp.shape), smem, _resident(rel.shape),
            hbm,
        ] + cast_specs,
        out_specs=[pl.BlockSpec((ts, d), lambda t: (jnp.maximum(t - 1, 0), 0))] + cast_specs,
        out_shape=[jax.ShapeDtypeStruct(x2.shape, x2.dtype)]
        + [jax.ShapeDtypeStruct(w.shape, jnp.bfloat16) for w, _ in cast],
        scratch_shapes=stage_slot + stage_slot + [
            pltpu.VMEM((ts, D_MODEL), jnp.bfloat16),
            pltpu.VMEM((2, 4, 2, BLOCK, BLOCK), jnp.float32),
            pltpu.VMEM((16, 2 * BLOCK, BLOCK), jnp.float32),
            pltpu.VMEM((16, 2 * BLOCK, 2 * BLOCK), jnp.bfloat16),
            pltpu.VMEM(w_in.shape, jnp.bfloat16),
            pltpu.VMEM(w_out.shape, jnp.bfloat16),
            pltpu.VMEM((2, WEIGHT_STAGE_ROWS, w_in.shape[1]), jnp.float32),
            pltpu.VMEM((2, WEIGHT_STAGE_ROWS, w_out.shape[1]), jnp.float32),
            pltpu.VMEM((CHUNK, LANES), jnp.float32),
            pltpu.SemaphoreType.DMA((4,)),
        ],
        compiler_params=pltpu.CompilerParams(
            dimension_semantics=("arbitrary",),
            vmem_limit_bytes=VMEM_LIMIT_BYTES),
        name="mixer",
    )(x2, x2, g1, w_in, vgain, wsp, bsp, sinks, rel, w_out, *[w for w, _ in cast])
    return out.reshape(batch, seq, d), cast_out


def _ffn(h, p, g2, w1, w2, wproj, wgate, gf, *, final_norm):
    rows, d = h.shape
    tm = ROW_TILE
    return pl.pallas_call(
        functools.partial(_ffn_kernel, final_norm=final_norm),
        grid=(rows // tm,),
        in_specs=[
            pl.BlockSpec((tm, d), lambda i: (i, 0)),
            pl.BlockSpec((tm, PLE_DIM), lambda i: (i, 0)),
            _resident(g2.shape), _resident(w1.shape), _resident(w2.shape),
            _resident(wproj.shape), _resident(wgate.shape), _resident(gf.shape),
        ],
        out_specs=pl.BlockSpec((tm, d), lambda i: (i, 0)),
        out_shape=jax.ShapeDtypeStruct(h.shape, h.dtype),
        compiler_params=pltpu.CompilerParams(
            dimension_semantics=("arbitrary",),
            vmem_limit_bytes=VMEM_LIMIT_BYTES),
        name="ffn",
    )(h, p, g2, w1, w2, wproj, wgate, gf)


def kernel(x, p, norm1_gain, w_in, gmlp_v_gain, w_spatial, b_spatial, attn_sinks, rel_bias_table, w_out, norm2_gain, w_ff1, w_ff2, w_ple_proj, w_ple_gate, final_gain):
    batch, seq, d = x.shape
    depth = w_in.shape[0]
    bf16 = jnp.bfloat16
    h = x
    for i in range(depth):
        h, (w1, w2, wproj, wgate) = _mixer(
            h, norm1_gain[i][None], w_in[i], gmlp_v_gain[i][None],
            w_spatial[i], b_spatial[i], attn_sinks[i], rel_bias_table.T, w_out[i],
            cast=[(w_ff1[i], 1), (w_ff2[i], 0), (w_ple_proj[i], 0), (w_ple_gate[i], 0)])
        h = _ffn(h.reshape(batch * seq, d), p[i].reshape(batch * seq, PLE_DIM),
                 norm2_gain[i][None], w1, w2, wproj, wgate,
                 final_gain[None], final_norm=(i == depth - 1)).reshape(batch, seq, d)
    return h
```

```python
import functools
import math

import jax
import jax.numpy as jnp
import numpy as np
from jax import lax
from jax.experimental import pallas as pl
from jax.experimental.pallas import tpu as pltpu

D_MODEL = 1024
PLE_DIM = 256
D_GMLP = 512
GMLP_GROUPS = 4
GROUP_DIM = D_GMLP // GMLP_GROUPS
CHUNK = 128
D_ATTN = 512
HEAD_DIM = 64
N_Q_HEADS = D_ATTN // HEAD_DIM
N_KV_HEADS = 2
Q_PER_KV = N_Q_HEADS // N_KV_HEADS
WINDOW = 128
BLOCK = WINDOW
REL_BUCKETS = 32
REL_MAX_DIST = 128
D_FF = 4 * D_MODEL
KV_W = N_KV_HEADS * HEAD_DIM
D_IN = 2 * D_GMLP + D_ATTN + 2 * KV_W
EPS = 1e-6
NEG_INF = -1e30
LOG2_E = math.log2(math.e)

LANES = 128
SEQ_TILE = 512
ROW_TILE = 512
FF_CHUNK = 2048
FFN_ROW_PART = 256
WEIGHT_STAGE_ROWS = 256
VMEM_LIMIT_BYTES = 56 * 1024 * 1024


def _bucket_thresholds():
    max_exact = REL_BUCKETS // 2
    n = np.arange(WINDOW)
    nf = np.maximum(n, 1).astype(np.float32)
    large = max_exact + (np.log(nf / max_exact) / np.float32(math.log(REL_MAX_DIST / max_exact))
                         * (REL_BUCKETS - max_exact)).astype(np.int32)
    bucket = np.where(n < max_exact, n, np.minimum(large, REL_BUCKETS - 1))
    assert (np.diff(bucket) >= 0).all()
    return [(b, int(np.argmax(bucket >= b))) for b in range(1, REL_BUCKETS) if (bucket >= b).any()]


_BUCKET_THRESHOLDS = _bucket_thresholds()


def _rms(x, gain):
    return x * lax.rsqrt(jnp.mean(x * x, axis=-1, keepdims=True) + EPS) * gain


def _gelu_tanh(x):
    c = float(np.float32(np.sqrt(2 / np.pi)))
    poly = x * x * (-2.0 * c * 0.044715 * LOG2_E) + (-2.0 * c * LOG2_E)
    return x / (1.0 + jnp.exp2(x * poly))


def _dot(a, b):
    return jnp.dot(a, b, preferred_element_type=jnp.float32)


def _dot_nt(a, b):
    return lax.dot_general(a, b, (((1,), (1,)), ((), ())), preferred_element_type=jnp.float32)


def _build_bias(rel_ref, bias_ref):
    a = lax.broadcasted_iota(jnp.int32, (BLOCK, BLOCK), 0)
    j = lax.broadcasted_iota(jnp.int32, (BLOCK, BLOCK), 1)
    from_prev = j > a
    n = jnp.where(from_prev, BLOCK + a - j, a - j)

    def entry(b, h):
        return jnp.broadcast_to(rel_ref[h:h + 1, b:b + 1], (BLOCK, BLOCK))

    for h in range(N_Q_HEADS):
        val = entry(0, h)
        for b, thr in _BUCKET_THRESHOLDS:
            val = jnp.where(n >= thr, entry(b, h), val)
        stack = 2 * (h // Q_PER_KV) + h % 2
        half = (h % Q_PER_KV) // 2
        val = val * LOG2_E
        bias_ref[0, stack, half] = val
        bias_ref[1, stack, half] = jnp.where(from_prev, NEG_INF, val)


def _build_spatial_bias(bsp_ref, bm_ref):
    row = lax.broadcasted_iota(jnp.int32, (CHUNK, CHUNK), 0)
    padded = jnp.zeros((CHUNK, CHUNK), jnp.float32)
    for g in range(GMLP_GROUPS):
        padded = jnp.where(row == g, jnp.broadcast_to(bsp_ref[g:g + 1, :], (CHUNK, CHUNK)), padded)
    bm_ref[...] = padded.T


def _load_as_bf16(src_hbm, dst_ref, stage_ref, sem_ref, sem_base, between=None):
    n_chunks = src_hbm.shape[0] // stage_ref.shape[1]
    rows = stage_ref.shape[1]

    def copy(k):
        return pltpu.make_async_copy(src_hbm.at[pl.ds(k * rows, rows), :],
                                     stage_ref.at[k % 2], sem_ref.at[sem_base + k % 2])

    copy(0).start()
    if between is not None:
        between()
    for k in range(n_chunks):
        if k + 1 < n_chunks:
            copy(k + 1).start()
        copy(k).wait()
        dst_ref[k * rows:(k + 1) * rows, :] = stage_ref[k % 2].astype(dst_ref.dtype)


_VARIANT = ((0, 1), (2, 3))


def _project_pieces(x_ref, g1_ref, w_in_ref, vgain_ref, dst):
    u_ref, vn_ref, q_ref, kbuf, vbuf = dst
    ts = x_ref.shape[0]
    o = 2 * D_GMLP
    state = {}

    def norm():
        state["hn"] = _rms(x_ref[...], g1_ref[...]).astype(jnp.bfloat16)

    def gate_u():
        u_ref[...] = _gelu_tanh(_dot(state["hn"], w_in_ref[:, 0:D_GMLP]))

    def gate_v():
        vg = _gelu_tanh(_dot(state["hn"], w_in_ref[:, D_GMLP:2 * D_GMLP]))
        for g in range(GMLP_GROUPS):
            cols = slice(g * GROUP_DIM, (g + 1) * GROUP_DIM)
            vn_ref[:, cols] = _rms(vg[:, cols], vgain_ref[:, cols]).astype(vn_ref.dtype)

    def query():
        q = _dot(state["hn"], w_in_ref[:, o:o + D_ATTN]) * (HEAD_DIM ** -0.5 * LOG2_E)
        q_ref[...] = q.astype(q_ref.dtype)

    def key_value():
        kv = _dot(state["hn"], w_in_ref[:, o + D_ATTN:o + D_ATTN + 2 * KV_W])
        lo = lax.broadcasted_iota(jnp.int32, (ts, LANES), 1) < HEAD_DIM
        for buf, t in ((kbuf, kv[:, 0:KV_W]), (vbuf, kv[:, KV_W:2 * KV_W])):
            tr = pltpu.roll(t, HEAD_DIM, axis=1)
            buf[0, BLOCK:BLOCK + ts, :] = jnp.where(lo, t, 0.0).astype(buf.dtype)
            buf[1, BLOCK:BLOCK + ts, :] = jnp.where(lo, 0.0, tr).astype(buf.dtype)
            buf[2, BLOCK:BLOCK + ts, :] = jnp.where(lo, tr, 0.0).astype(buf.dtype)
            buf[3, BLOCK:BLOCK + ts, :] = jnp.where(lo, 0.0, t).astype(buf.dtype)

    return [norm, gate_u, gate_v, query, key_value]


def _attend_pieces(i, src, seq_start, sink_ref, bias_ref, mix_ref, s_scr, p_scr):
    _, _, q_ref, kbuf, vbuf = src
    rows = slice(i * BLOCK, (i + 1) * BLOCK)
    keys = slice(i * BLOCK, (i + 2) * BLOCK)
    first = jnp.where(seq_start, 1, 0) if i == 0 else 0
    upper = lax.broadcasted_iota(jnp.int32, (2 * BLOCK, 1), 0) < BLOCK
    query = lax.broadcasted_iota(jnp.int32, (2 * BLOCK, BLOCK), 0) % BLOCK
    from_prev = lax.broadcasted_iota(jnp.int32, (2 * BLOCK, BLOCK), 1) > query
    stacks = [(g, parity) for g in range(N_KV_HEADS) for parity in range(2)]
    state = {}

    def scores():
        for g, parity in stacks:
            qs = jnp.concatenate([q_ref[rows, 2 * g * LANES:(2 * g + 1) * LANES],
                                  q_ref[rows, (2 * g + 1) * LANES:(2 * g + 2) * LANES]], axis=0)
            s = _dot_nt(qs, kbuf[_VARIANT[g][parity], keys, :])
            s = jnp.where(from_prev, s[:, 0:BLOCK], s[:, BLOCK:2 * BLOCK])
            s_scr[4 * i + 2 * g + parity] = (
                s + bias_ref[first, 2 * g + parity].reshape(2 * BLOCK, BLOCK))

    def softmax():
        zero = jnp.zeros((2 * BLOCK, BLOCK), jnp.bfloat16)
        for g, parity in stacks:
            head = Q_PER_KV * g + parity
            sink = jnp.where(upper, sink_ref[head], sink_ref[head + 2]) * LOG2_E
            s = s_scr[4 * i + 2 * g + parity]
            m = jnp.maximum(jnp.max(s, axis=-1, keepdims=True), sink)
            e = jnp.exp2(s - m)
            denom = jnp.sum(e, axis=-1, keepdims=True) + jnp.exp2(sink - m)
            p = e.astype(jnp.bfloat16)
            p_scr[4 * i + 2 * g + parity] = jnp.concatenate(
                [jnp.where(from_prev, p, zero), jnp.where(from_prev, zero, p)], axis=1)
            state["inv", g, parity] = 1.0 / denom

    def values():
        low = lax.broadcasted_iota(jnp.int32, (2 * BLOCK, LANES), 1) < HEAD_DIM
        for g in range(N_KV_HEADS):
            pair0 = D_GMLP + 2 * g * LANES
            acc = (_dot(p_scr[4 * i + 2 * g], vbuf[_VARIANT[g][0], keys, :])
                   + _dot(p_scr[4 * i + 2 * g + 1], vbuf[_VARIANT[g][1], keys, :]))
            acc = acc * jnp.where(low, state["inv", g, 0], state["inv", g, 1])
            mix_ref[rows, pair0:pair0 + LANES] = acc[0:BLOCK].astype(mix_ref.dtype)
            mix_ref[rows, pair0 + LANES:pair0 + 2 * LANES] = acc[BLOCK:].astype(mix_ref.dtype)

    return scores, softmax, values


def _spatial_gate(src, wsp_ref, bm_ref, mix_ref):
    u_ref, vn_ref = src[0:2]
    t_idx = lax.broadcasted_iota(jnp.int32, (CHUNK, CHUNK), 0)
    s_idx = lax.broadcasted_iota(jnp.int32, (CHUNK, CHUNK), 1)
    causal = s_idx <= t_idx
    for g in range(GMLP_GROUPS):
        cols = slice(g * GROUP_DIM, (g + 1) * GROUP_DIM)
        w_g = jnp.where(causal, wsp_ref[g], 0.0).astype(jnp.bfloat16)
        b_g = bm_ref[:, g:g + 1]
        for c in range(u_ref.shape[0] // CHUNK):
            rows = slice(c * CHUNK, (c + 1) * CHUNK)
            sv = _dot(w_g, vn_ref[rows, cols]) + b_g
            mix_ref[rows, cols] = (u_ref[rows, cols] * sv).astype(mix_ref.dtype)


N_CAST = 4
SCORES_AHEAD = 2


def _mixer_kernel(x_ref, xprev_ref, g1_ref, w_in_hbm, vgain_ref, wsp_ref, bsp_ref, sink_ref,
                  rel_ref, w_out_hbm, *rest, tiles_per_seq, n_tiles):
    cast_src, o_ref = rest[0:N_CAST], rest[N_CAST]
    cast_dst, scratch = rest[N_CAST + 1:2 * N_CAST + 1], rest[2 * N_CAST + 1:]
    slots = (scratch[0:5], scratch[5:10])
    mix_ref, bias_ref, s_scr, p_scr = scratch[10:14]
    w_in_ref, w_out_ref, stage_in, stage_out, bm_ref, dma_sem = scratch[14:20]
    ts = x_ref.shape[0]
    t = pl.program_id(0)

    cur_starts = t % tiles_per_seq == 0
    prev_started = (t + tiles_per_seq - 1) % tiles_per_seq == 0
    n_blk = ts // BLOCK

    def step(dst, src):
        project = _project_pieces(x_ref, g1_ref, w_in_ref, vgain_ref, dst) if dst else []
        attend = [_attend_pieces(i, src, prev_started, sink_ref, bias_ref, mix_ref, s_scr,
                                 p_scr) for i in range(n_blk)] if src else []
        if src:
            for i in range(SCORES_AHEAD):
                attend[i][0]()
            _spatial_gate(src, wsp_ref, bm_ref, mix_ref)
        if dst:
            project.pop(0)()
        for i in range(n_blk):
            if src:
                if i + SCORES_AHEAD < n_blk:
                    attend[i + SCORES_AHEAD][0]()
                attend[i][1]()
            if dst:
                project.pop(0)()
            if src:
                attend[i][2]()
        assert not project
        if src:
            o_ref[...] = xprev_ref[...] + _dot(mix_ref[...], w_out_ref[...])
        for s, d in zip(cast_src, cast_dst):
            d[...] = s[...].astype(d.dtype)
        if dst:
            for k in (3, 4):
                zeros = jnp.zeros((4, BLOCK, LANES), dst[k].dtype)
                dst[k][:, 0:BLOCK, :] = (
                    jnp.where(cur_starts, zeros, src[k][:, ts:ts + BLOCK, :]) if src else zeros)

    @pl.when(t == 0)
    def _():
        def build_tables():
            _build_bias(rel_ref, bias_ref)
            _build_spatial_bias(bsp_ref, bm_ref)

        _load_as_bf16(w_in_hbm, w_in_ref, stage_in, dma_sem, 0, between=build_tables)
        _load_as_bf16(w_out_hbm, w_out_ref, stage_out, dma_sem, 2)
        step(slots[0], None)

    @pl.when(t == n_tiles)
    def _():
        step(None, slots[(n_tiles - 1) % 2])

    @pl.when((t > 0) & (t < n_tiles) & (t % 2 == 0))
    def _():
        step(slots[0], slots[1])

    @pl.when((t < n_tiles) & (t % 2 == 1))
    def _():
        step(slots[1], slots[0])


def _ffn_kernel(h_ref, p_ref, g2_ref, w1_ref, w2_ref, wproj_ref, wgate_ref, gf_ref, o_ref,
                acc_ref, *, final_norm, n_tiles):
    tm = h_ref.shape[0]
    parts = [slice(r, r + FFN_ROW_PART) for r in range(0, tm, FFN_ROW_PART)]

    def step(expand, finish):
        hidden = []

        def expand_chunk(c, hn_parts):
            cols = slice(c * FF_CHUNK, (c + 1) * FF_CHUNK)
            if c == 0:
                t = jnp.concatenate([_dot(part, w1_ref[:, cols]) for part in hn_parts], axis=0)
            else:
                t = _dot(jnp.concatenate(hn_parts, axis=0), w1_ref[:, cols])
            t = jnp.maximum(t, 0.0)
            hidden.append((t * t).astype(jnp.bfloat16))

        n_chunks = D_FF // FF_CHUNK
        assert len(parts) == n_chunks, "one finishing row part is issued per expand chunk"
        if expand:
            hn_parts = [_rms(h_ref[r, :], g2_ref[...]).astype(jnp.bfloat16) for r in parts]
        if finish:
            ple = _dot(p_ref[...].astype(jnp.bfloat16), wproj_ref[...])
        for c in range(n_chunks):
            if expand:
                expand_chunk(c, hn_parts)
            if finish:
                r = parts[c]
                z = _dot(acc_ref[r, :].astype(jnp.bfloat16), wgate_ref[...])
                out = acc_ref[r, :] + ple[r] / (1.0 + jnp.exp2(z * (-LOG2_E)))
                o_ref[r, :] = _rms(out, gf_ref[...]) if final_norm else out
        if expand:
            acc_ref[...] = h_ref[...] + _dot(jnp.concatenate(hidden, axis=1), w2_ref[...])

    t = pl.program_id(0)

    @pl.when(t == 0)
    def _():
        step(True, False)

    @pl.when((t > 0) & (t < n_tiles))
    def _():
        step(True, True)

    @pl.when(t == n_tiles)
    def _():
        step(False, True)


def _resident(shape):
    return pl.BlockSpec(shape, lambda *_: (0,) * len(shape), pipeline_mode=pl.Buffered(1))


def _cast_slice_spec(arr, axis, n_tiles):
    size = arr.shape[axis]
    min_thick = 16 if axis == 0 else LANES
    thick = max(size // n_tiles, min_thick)
    assert size % thick == 0
    last = size // thick - 1
    block = tuple(thick if a == axis else n for a, n in enumerate(arr.shape))
    return pl.BlockSpec(block, lambda t: tuple(jnp.minimum(t, last) if a == axis else 0
                                               for a in range(arr.ndim)))


def _mixer(x, g1, w_in, vgain, wsp, bsp, sinks, rel, w_out, cast):
    batch, seq, d = x.shape
    ts = SEQ_TILE
    assert ts // BLOCK == 4, "stage-1 pieces are paired one-to-one with attention blocks"
    assert len(cast) == N_CAST
    n_tiles = batch * seq // ts
    x2 = x.reshape(batch * seq, d)
    smem = pl.BlockSpec(memory_space=pltpu.SMEM)
    hbm = pl.BlockSpec(memory_space=pl.ANY)
    cast_specs = [_cast_slice_spec(w, axis, n_tiles) for w, axis in cast]
    stage_slot = [
        pltpu.VMEM((ts, D_GMLP), jnp.float32),
        pltpu.VMEM((ts, D_GMLP), jnp.bfloat16),
        pltpu.VMEM((ts, D_ATTN), jnp.bfloat16),
        pltpu.VMEM((4, BLOCK + ts, LANES), jnp.bfloat16),
        pltpu.VMEM((4, BLOCK + ts, LANES), jnp.bfloat16),
    ]
    out, *cast_out = pl.pallas_call(
        functools.partial(_mixer_kernel, tiles_per_seq=seq // ts, n_tiles=n_tiles),
        grid=(n_tiles + 1,),
        in_specs=[
            pl.BlockSpec((ts, d), lambda t: (jnp.minimum(t, n_tiles - 1), 0)),
            pl.BlockSpec((ts, d), lambda t: (jnp.maximum(t - 1, 0), 0)),
            _resident(g1.shape), hbm, _resident(vgain.shape),
            _resident(wsp.shape), _resident(bsp.shape), smem, _resident(rel.shape),
            hbm,
        ] + cast_specs,
        out_specs=[pl.BlockSpec((ts, d), lambda t: (jnp.maximum(t - 1, 0), 0))] + cast_specs,
        out_shape=[jax.ShapeDtypeStruct(x2.shape, x2.dtype)]
        + [jax.ShapeDtypeStruct(w.shape, jnp.bfloat16) for w, _ in cast],
        scratch_shapes=stage_slot + stage_slot + [
            pltpu.VMEM((ts, D_MODEL), jnp.bfloat16),
            pltpu.VMEM((2, 4, 2, BLOCK, BLOCK), jnp.float32),
            pltpu.VMEM((16, 2 * BLOCK, BLOCK), jnp.float32),
            pltpu.VMEM((16, 2 * BLOCK, 2 * BLOCK), jnp.bfloat16),
            pltpu.VMEM(w_in.shape, jnp.bfloat16),
            pltpu.VMEM(w_out.shape, jnp.bfloat16),
            pltpu.VMEM((2, WEIGHT_STAGE_ROWS, w_in.shape[1]), jnp.float32),
            pltpu.VMEM((2, WEIGHT_STAGE_ROWS, w_out.shape[1]), jnp.float32),
            pltpu.VMEM((CHUNK, LANES), jnp.float32),
            pltpu.SemaphoreType.DMA((4,)),
        ],
        compiler_params=pltpu.CompilerParams(
            dimension_semantics=("arbitrary",),
            vmem_limit_bytes=VMEM_LIMIT_BYTES),
        name="mixer",
    )(x2, x2, g1, w_in, vgain, wsp, bsp, sinks, rel, w_out, *[w for w, _ in cast])
    return out.reshape(batch, seq, d), cast_out


def _ffn(h, p, g2, w1, w2, wproj, wgate, gf, *, final_norm):
    rows, d = h.shape
    tm = ROW_TILE
    n_tiles = rows // tm
    return pl.pallas_call(
        functools.partial(_ffn_kernel, final_norm=final_norm, n_tiles=n_tiles),
        grid=(n_tiles + 1,),
        in_specs=[
            pl.BlockSpec((tm, d), lambda t: (jnp.minimum(t, n_tiles - 1), 0)),
            pl.BlockSpec((tm, PLE_DIM), lambda t: (jnp.maximum(t - 1, 0), 0)),
            _resident(g2.shape), _resident(w1.shape), _resident(w2.shape),
            _resident(wproj.shape), _resident(wgate.shape), _resident(gf.shape),
        ],
        out_specs=pl.BlockSpec((tm, d), lambda t: (jnp.maximum(t - 1, 0), 0)),
        out_shape=jax.ShapeDtypeStruct(h.shape, h.dtype),
        scratch_shapes=[pltpu.VMEM((tm, d), jnp.float32)],
        compiler_params=pltpu.CompilerParams(
            dimension_semantics=("arbitrary",),
            vmem_limit_bytes=VMEM_LIMIT_BYTES),
        name="ffn",
    )(h, p, g2, w1, w2, wproj, wgate, gf)


def kernel(x, p, norm1_gain, w_in, gmlp_v_gain, w_spatial, b_spatial, attn_sinks, rel_bias_table, w_out, norm2_gain, w_ff1, w_ff2, w_ple_proj, w_ple_gate, final_gain):
    batch, seq, d = x.shape
    depth = w_in.shape[0]
    bf16 = jnp.bfloat16
    h = x
    for i in range(depth):
        h, (w1, w2, wproj, wgate) = _mixer(
            h, norm1_gain[i][None], w_in[i], gmlp_v_gain[i][None],
            w_spatial[i], b_spatial[i], attn_sinks[i], rel_bias_table.T, w_out[i],
            cast=[(w_ff1[i], 1), (w_ff2[i], 0), (w_ple_proj[i], 0), (w_ple_gate[i], 0)])
        h = _ffn(h.reshape(batch * seq, d), p[i].reshape(batch * seq, PLE_DIM),
                 norm2_gain[i][None], w1, w2, wproj, wgate,
                 final_gain[None], final_norm=(i == depth - 1)).reshape(batch, seq, d)
    return h
```

```python
import functools
import math

import jax
import jax.numpy as jnp
import numpy as np
from jax import lax
from jax.experimental import pallas as pl
from jax.experimental.pallas import tpu as pltpu

D_MODEL = 1024
PLE_DIM = 256
D_GMLP = 512
GMLP_GROUPS = 4
GROUP_DIM = D_GMLP // GMLP_GROUPS
CHUNK = 128
D_ATTN = 512
HEAD_DIM = 64
N_Q_HEADS = D_ATTN // HEAD_DIM
N_KV_HEADS = 2
Q_PER_KV = N_Q_HEADS // N_KV_HEADS
WINDOW = 128
BLOCK = WINDOW
REL_BUCKETS = 32
REL_MAX_DIST = 128
D_FF = 4 * D_MODEL
KV_W = N_KV_HEADS * HEAD_DIM
D_IN = 2 * D_GMLP + D_ATTN + 2 * KV_W
EPS = 1e-6
NEG_INF = -1e30
LOG2_E = math.log2(math.e)

LANES = 128
SEQ_TILE = 512
ROW_TILE = 1024
FF_CHUNK = 1024
FFN_ROW_PART = 256
WEIGHT_STAGE_ROWS = 256
VMEM_LIMIT_BYTES = 56 * 1024 * 1024


def _bucket_thresholds():
    max_exact = REL_BUCKETS // 2
    n = np.arange(WINDOW)
    nf = np.maximum(n, 1).astype(np.float32)
    large = max_exact + (np.log(nf / max_exact) / np.float32(math.log(REL_MAX_DIST / max_exact))
                         * (REL_BUCKETS - max_exact)).astype(np.int32)
    bucket = np.where(n < max_exact, n, np.minimum(large, REL_BUCKETS - 1))
    assert (np.diff(bucket) >= 0).all()
    return [(b, int(np.argmax(bucket >= b))) for b in range(1, REL_BUCKETS) if (bucket >= b).any()]


_BUCKET_THRESHOLDS = _bucket_thresholds()


def _rms(x, gain):
    return x * lax.rsqrt(jnp.mean(x * x, axis=-1, keepdims=True) + EPS) * gain


def _gelu_tanh(x):
    c = float(np.float32(np.sqrt(2 / np.pi)))
    poly = x * x * (-2.0 * c * 0.044715 * LOG2_E) + (-2.0 * c * LOG2_E)
    return x / (1.0 + jnp.exp2(x * poly))


def _dot(a, b):
    return jnp.dot(a, b, preferred_element_type=jnp.float32)


def _dot_nt(a, b):
    return lax.dot_general(a, b, (((1,), (1,)), ((), ())), preferred_element_type=jnp.float32)


def _build_bias(rel_ref, bias_ref):
    a = lax.broadcasted_iota(jnp.int32, (BLOCK, BLOCK), 0)
    j = lax.broadcasted_iota(jnp.int32, (BLOCK, BLOCK), 1)
    from_prev = j > a
    n = jnp.where(from_prev, BLOCK + a - j, a - j)

    def entry(b, h):
        return jnp.broadcast_to(rel_ref[h:h + 1, b:b + 1], (BLOCK, BLOCK))

    for h in range(N_Q_HEADS):
        val = entry(0, h)
        for b, thr in _BUCKET_THRESHOLDS:
            val = jnp.where(n >= thr, entry(b, h), val)
        stack = 2 * (h // Q_PER_KV) + h % 2
        half = (h % Q_PER_KV) // 2
        val = val * LOG2_E
        bias_ref[0, stack, half] = val
        bias_ref[1, stack, half] = jnp.where(from_prev, NEG_INF, val)


def _build_spatial_bias(bsp_ref, bm_ref):
    row = lax.broadcasted_iota(jnp.int32, (CHUNK, CHUNK), 0)
    padded = jnp.zeros((CHUNK, CHUNK), jnp.float32)
    for g in range(GMLP_GROUPS):
        padded = jnp.where(row == g, jnp.broadcast_to(bsp_ref[g:g + 1, :], (CHUNK, CHUNK)), padded)
    bm_ref[...] = padded.T


def _load_as_bf16(src_hbm, dst_ref, stage_ref, sem_ref, sem_base, between=None):
    n_chunks = src_hbm.shape[0] // stage_ref.shape[1]
    rows = stage_ref.shape[1]

    def copy(k):
        return pltpu.make_async_copy(src_hbm.at[pl.ds(k * rows, rows), :],
                                     stage_ref.at[k % 2], sem_ref.at[sem_base + k % 2])

    copy(0).start()
    if between is not None:
        between()
    for k in range(n_chunks):
        if k + 1 < n_chunks:
            copy(k + 1).start()
        copy(k).wait()
        dst_ref[k * rows:(k + 1) * rows, :] = stage_ref[k % 2].astype(dst_ref.dtype)


_VARIANT = ((0, 1), (2, 3))


def _project_pieces(x_ref, g1_ref, w_in_ref, vgain_ref, dst):
    u_ref, vn_ref, q_ref, kbuf, vbuf = dst
    ts = x_ref.shape[0]
    o = 2 * D_GMLP
    state = {}

    def norm():
        state["hn"] = _rms(x_ref[...], g1_ref[...]).astype(jnp.bfloat16)

    def gate_u():
        u_ref[...] = _gelu_tanh(_dot(state["hn"], w_in_ref[:, 0:D_GMLP]))

    def gate_v():
        vg = _gelu_tanh(_dot(state["hn"], w_in_ref[:, D_GMLP:2 * D_GMLP]))
        for g in range(GMLP_GROUPS):
            cols = slice(g * GROUP_DIM, (g + 1) * GROUP_DIM)
            vn_ref[:, cols] = _rms(vg[:, cols], vgain_ref[:, cols]).astype(vn_ref.dtype)

    def query():
        q = _dot(state["hn"], w_in_ref[:, o:o + D_ATTN]) * (HEAD_DIM ** -0.5 * LOG2_E)
        q_ref[...] = q.astype(q_ref.dtype)

    def key_value():
        kv = _dot(state["hn"], w_in_ref[:, o + D_ATTN:o + D_ATTN + 2 * KV_W])
        lo = lax.broadcasted_iota(jnp.int32, (ts, LANES), 1) < HEAD_DIM
        for buf, t in ((kbuf, kv[:, 0:KV_W]), (vbuf, kv[:, KV_W:2 * KV_W])):
            tr = pltpu.roll(t, HEAD_DIM, axis=1)
            buf[0, BLOCK:BLOCK + ts, :] = jnp.where(lo, t, 0.0).astype(buf.dtype)
            buf[1, BLOCK:BLOCK + ts, :] = jnp.where(lo, 0.0, tr).astype(buf.dtype)
            buf[2, BLOCK:BLOCK + ts, :] = jnp.where(lo, tr, 0.0).astype(buf.dtype)
            buf[3, BLOCK:BLOCK + ts, :] = jnp.where(lo, 0.0, t).astype(buf.dtype)

    return [norm, gate_u, gate_v, query, key_value]


def _attend_pieces(i, src, seq_start, sink_ref, bias_ref, mix_ref, s_scr, p_scr):
    _, _, q_ref, kbuf, vbuf = src
    rows = slice(i * BLOCK, (i + 1) * BLOCK)
    keys = slice(i * BLOCK, (i + 2) * BLOCK)
    first = jnp.where(seq_start, 1, 0) if i == 0 else 0
    upper = lax.broadcasted_iota(jnp.int32, (2 * BLOCK, 1), 0) < BLOCK
    query = lax.broadcasted_iota(jnp.int32, (2 * BLOCK, BLOCK), 0) % BLOCK
    from_prev = lax.broadcasted_iota(jnp.int32, (2 * BLOCK, BLOCK), 1) > query
    stacks = [(g, parity) for g in range(N_KV_HEADS) for parity in range(2)]
    state = {}

    def scores():
        for g, parity in stacks:
            qs = jnp.concatenate([q_ref[rows, 2 * g * LANES:(2 * g + 1) * LANES],
                                  q_ref[rows, (2 * g + 1) * LANES:(2 * g + 2) * LANES]], axis=0)
            s = _dot_nt(qs, kbuf[_VARIANT[g][parity], keys, :])
            s = jnp.where(from_prev, s[:, 0:BLOCK], s[:, BLOCK:2 * BLOCK])
            s_scr[4 * i + 2 * g + parity] = (
                s + bias_ref[first, 2 * g + parity].reshape(2 * BLOCK, BLOCK))

    def softmax():
        zero = jnp.zeros((2 * BLOCK, BLOCK), jnp.bfloat16)
        for g, parity in stacks:
            head = Q_PER_KV * g + parity
            sink = jnp.where(upper, sink_ref[head], sink_ref[head + 2]) * LOG2_E
            s = s_scr[4 * i + 2 * g + parity]
            m = jnp.maximum(jnp.max(s, axis=-1, keepdims=True), sink)
            e = jnp.exp2(s - m)
            denom = jnp.sum(e, axis=-1, keepdims=True) + jnp.exp2(sink - m)
            p = e.astype(jnp.bfloat16)
            p_scr[4 * i + 2 * g + parity] = jnp.concatenate(
                [jnp.where(from_prev, p, zero), jnp.where(from_prev, zero, p)], axis=1)
            state["inv", g, parity] = 1.0 / denom

    def values():
        low = lax.broadcasted_iota(jnp.int32, (2 * BLOCK, LANES), 1) < HEAD_DIM
        for g in range(N_KV_HEADS):
            pair0 = D_GMLP + 2 * g * LANES
            acc = (_dot(p_scr[4 * i + 2 * g], vbuf[_VARIANT[g][0], keys, :])
                   + _dot(p_scr[4 * i + 2 * g + 1], vbuf[_VARIANT[g][1], keys, :]))
            acc = acc * jnp.where(low, state["inv", g, 0], state["inv", g, 1])
            mix_ref[rows, pair0:pair0 + LANES] = acc[0:BLOCK].astype(mix_ref.dtype)
            mix_ref[rows, pair0 + LANES:pair0 + 2 * LANES] = acc[BLOCK:].astype(mix_ref.dtype)

    return scores, softmax, values


def _spatial_gate(src, wsp_ref, bm_ref, mix_ref):
    u_ref, vn_ref = src[0:2]
    t_idx = lax.broadcasted_iota(jnp.int32, (CHUNK, CHUNK), 0)
    s_idx = lax.broadcasted_iota(jnp.int32, (CHUNK, CHUNK), 1)
    causal = s_idx <= t_idx
    for g in range(GMLP_GROUPS):
        cols = slice(g * GROUP_DIM, (g + 1) * GROUP_DIM)
        w_g = jnp.where(causal, wsp_ref[g], 0.0).astype(jnp.bfloat16)
        b_g = bm_ref[:, g:g + 1]
        for c in range(u_ref.shape[0] // CHUNK):
            rows = slice(c * CHUNK, (c + 1) * CHUNK)
            sv = _dot(w_g, vn_ref[rows, cols]) + b_g
            mix_ref[rows, cols] = (u_ref[rows, cols] * sv).astype(mix_ref.dtype)


N_CAST = 4
SCORES_AHEAD = 2


def _mixer_kernel(x_ref, xprev_ref, g1_ref, w_in_hbm, vgain_ref, wsp_ref, bsp_ref, sink_ref,
                  rel_ref, w_out_hbm, *rest, tiles_per_seq, n_tiles):
    cast_src, o_ref = rest[0:N_CAST], rest[N_CAST]
    cast_dst, scratch = rest[N_CAST + 1:2 * N_CAST + 1], rest[2 * N_CAST + 1:]
    slots = (scratch[0:5], scratch[5:10])
    mix_ref, bias_ref, s_scr, p_scr = scratch[10:14]
    w_in_ref, w_out_ref, stage_in, stage_out, bm_ref, dma_sem = scratch[14:20]
    ts = x_ref.shape[0]
    t = pl.program_id(0)

    cur_starts = t % tiles_per_seq == 0
    prev_started = (t + tiles_per_seq - 1) % tiles_per_seq == 0
    n_blk = ts // BLOCK

    def step(dst, src):
        project = _project_pieces(x_ref, g1_ref, w_in_ref, vgain_ref, dst) if dst else []
        attend = [_attend_pieces(i, src, prev_started, sink_ref, bias_ref, mix_ref, s_scr,
                                 p_scr) for i in range(n_blk)] if src else []
        if src:
            for i in range(SCORES_AHEAD):
                attend[i][0]()
            _spatial_gate(src, wsp_ref, bm_ref, mix_ref)
        if dst:
            project.pop(0)()
        for i in range(n_blk):
            if src:
                if i + SCORES_AHEAD < n_blk:
                    attend[i + SCORES_AHEAD][0]()
                attend[i][1]()
            if dst:
                project.pop(0)()
            if src:
                attend[i][2]()
        assert not project
        if src:
            o_ref[...] = xprev_ref[...] + _dot(mix_ref[...], w_out_ref[...])
        for s, d in zip(cast_src, cast_dst):
            d[...] = s[...].astype(d.dtype)
        if dst:
            for k in (3, 4):
                zeros = jnp.zeros((4, BLOCK, LANES), dst[k].dtype)
                dst[k][:, 0:BLOCK, :] = (
                    jnp.where(cur_starts, zeros, src[k][:, ts:ts + BLOCK, :]) if src else zeros)

    @pl.when(t == 0)
    def _():
        def build_tables():
            _build_bias(rel_ref, bias_ref)
            _build_spatial_bias(bsp_ref, bm_ref)

        _load_as_bf16(w_in_hbm, w_in_ref, stage_in, dma_sem, 0, between=build_tables)
        _load_as_bf16(w_out_hbm, w_out_ref, stage_out, dma_sem, 2)
        step(slots[0], None)

    @pl.when(t == n_tiles)
    def _():
        step(None, slots[(n_tiles - 1) % 2])

    @pl.when((t > 0) & (t < n_tiles) & (t % 2 == 0))
    def _():
        step(slots[0], slots[1])

    @pl.when((t < n_tiles) & (t % 2 == 1))
    def _():
        step(slots[1], slots[0])


def _ffn_kernel(h_ref, p_ref, g2_ref, w1_ref, w2_ref, wproj_ref, wgate_ref, gf_ref, o_ref,
                acc_ref, *, final_norm, n_tiles):
    tm = h_ref.shape[0]
    parts = [slice(r, r + FFN_ROW_PART) for r in range(0, tm, FFN_ROW_PART)]

    def step(expand, finish):
        hidden = []

        def expand_chunk(c, hn_parts):
            cols = slice(c * FF_CHUNK, (c + 1) * FF_CHUNK)
            if c == 0:
                t = jnp.concatenate([_dot(part, w1_ref[:, cols]) for part in hn_parts], axis=0)
            else:
                t = _dot(jnp.concatenate(hn_parts, axis=0), w1_ref[:, cols])
            t = jnp.maximum(t, 0.0)
            hidden.append((t * t).astype(jnp.bfloat16))

        n_chunks = D_FF // FF_CHUNK
        assert len(parts) == n_chunks, "one finishing row part is issued per expand chunk"
        if expand:
            hn_parts = [_rms(h_ref[r, :], g2_ref[...]).astype(jnp.bfloat16) for r in parts]
        if finish:
            ple = _dot(p_ref[...].astype(jnp.bfloat16), wproj_ref[...])
        for c in range(n_chunks):
            if expand:
                expand_chunk(c, hn_parts)
            if finish:
                r = parts[c]
                z = _dot(acc_ref[r, :].astype(jnp.bfloat16), wgate_ref[...])
                out = acc_ref[r, :] + ple[r] / (1.0 + jnp.exp2(z * (-LOG2_E)))
                o_ref[r, :] = _rms(out, gf_ref[...]) if final_norm else out
        if expand:
            acc_ref[...] = h_ref[...] + _dot(jnp.concatenate(hidden, axis=1), w2_ref[...])

    t = pl.program_id(0)

    @pl.when(t == 0)
    def _():
        acc_ref[...] = jnp.zeros(acc_ref.shape, acc_ref.dtype)

    @pl.when(t < n_tiles)
    def _():
        step(True, True)

    @pl.when(t == n_tiles)
    def _():
        step(False, True)


def _resident(shape):
    return pl.BlockSpec(shape, lambda *_: (0,) * len(shape), pipeline_mode=pl.Buffered(1))


def _cast_slice_spec(arr, axis, n_tiles):
    size = arr.shape[axis]
    min_thick = 16 if axis == 0 else LANES
    thick = max(size // n_tiles, min_thick)
    assert size % thick == 0
    last = size // thick - 1
    block = tuple(thick if a == axis else n for a, n in enumerate(arr.shape))
    return pl.BlockSpec(block, lambda t: tuple(jnp.minimum(t, last) if a == axis else 0
                                               for a in range(arr.ndim)))


def _mixer(x, g1, w_in, vgain, wsp, bsp, sinks, rel, w_out, cast):
    batch, seq, d = x.shape
    ts = SEQ_TILE
    assert ts // BLOCK == 4, "stage-1 pieces are paired one-to-one with attention blocks"
    assert len(cast) == N_CAST
    n_tiles = batch * seq // ts
    x2 = x.reshape(batch * seq, d)
    smem = pl.BlockSpec(memory_space=pltpu.SMEM)
    hbm = pl.BlockSpec(memory_space=pl.ANY)
    cast_specs = [_cast_slice_spec(w, axis, n_tiles) for w, axis in cast]
    stage_slot = [
        pltpu.VMEM((ts, D_GMLP), jnp.float32),
        pltpu.VMEM((ts, D_GMLP), jnp.bfloat16),
        pltpu.VMEM((ts, D_ATTN), jnp.bfloat16),
        pltpu.VMEM((4, BLOCK + ts, LANES), jnp.bfloat16),
        pltpu.VMEM((4, BLOCK + ts, LANES), jnp.bfloat16),
    ]
    out, *cast_out = pl.pallas_call(
        functools.partial(_mixer_kernel, tiles_per_seq=seq // ts, n_tiles=n_tiles),
        grid=(n_tiles + 1,),
        in_specs=[
            pl.BlockSpec((ts, d), lambda t: (jnp.minimum(t, n_tiles - 1), 0)),
            pl.BlockSpec((ts, d), lambda t: (jnp.maximum(t - 1, 0), 0)),
            _resident(g1.shape), hbm, _resident(vgain.shape),
            _resident(wsp.shape), _resident(bsp.shape), smem, _resident(rel.shape),
            hbm,
        ] + cast_specs,
        out_specs=[pl.BlockSpec((ts, d), lambda t: (jnp.maximum(t - 1, 0), 0))] + cast_specs,
        out_shape=[jax.ShapeDtypeStruct(x2.shape, x2.dtype)]
        + [jax.ShapeDtypeStruct(w.shape, jnp.bfloat16) for w, _ in cast],
        scratch_shapes=stage_slot + stage_slot + [
            pltpu.VMEM((ts, D_MODEL), jnp.bfloat16),
            pltpu.VMEM((2, 4, 2, BLOCK, BLOCK), jnp.float32),
            pltpu.VMEM((16, 2 * BLOCK, BLOCK), jnp.float32),
            pltpu.VMEM((16, 2 * BLOCK, 2 * BLOCK), jnp.bfloat16),
            pltpu.VMEM(w_in.shape, jnp.bfloat16),
            pltpu.VMEM(w_out.shape, jnp.bfloat16),
            pltpu.VMEM((2, WEIGHT_STAGE_ROWS, w_in.shape[1]), jnp.float32),
            pltpu.VMEM((2, WEIGHT_STAGE_ROWS, w_out.shape[1]), jnp.float32),
            pltpu.VMEM((CHUNK, LANES), jnp.float32),
            pltpu.SemaphoreType.DMA((4,)),
        ],
        compiler_params=pltpu.CompilerParams(
            dimension_semantics=("arbitrary",),
            vmem_limit_bytes=VMEM_LIMIT_BYTES),
        name="mixer",
    )(x2, x2, g1, w_in, vgain, wsp, bsp, sinks, rel, w_out, *[w for w, _ in cast])
    return out.reshape(batch, seq, d), cast_out


def _ffn(h, p, g2, w1, w2, wproj, wgate, gf, *, final_norm):
    rows, d = h.shape
    tm = ROW_TILE
    n_tiles = rows // tm
    return pl.pallas_call(
        functools.partial(_ffn_kernel, final_norm=final_norm, n_tiles=n_tiles),
        grid=(n_tiles + 1,),
        in_specs=[
            pl.BlockSpec((tm, d), lambda t: (jnp.minimum(t, n_tiles - 1), 0)),
            pl.BlockSpec((tm, PLE_DIM), lambda t: (jnp.maximum(t - 1, 0), 0)),
            _resident(g2.shape), _resident(w1.shape), _resident(w2.shape),
            _resident(wproj.shape), _resident(wgate.shape), _resident(gf.shape),
        ],
        out_specs=pl.BlockSpec((tm, d), lambda t: (jnp.maximum(t - 1, 0), 0)),
        out_shape=jax.ShapeDtypeStruct(h.shape, h.dtype),
        scratch_shapes=[pltpu.VMEM((tm, d), jnp.float32)],
        compiler_params=pltpu.CompilerParams(
            dimension_semantics=("arbitrary",),
            vmem_limit_bytes=VMEM_LIMIT_BYTES),
        name="ffn",
    )(h, p, g2, w1, w2, wproj, wgate, gf)


def kernel(x, p, norm1_gain, w_in, gmlp_v_gain, w_spatial, b_spatial, attn_sinks, rel_bias_table, w_out, norm2_gain, w_ff1, w_ff2, w_ple_proj, w_ple_gate, final_gain):
    batch, seq, d = x.shape
    depth = w_in.shape[0]
    bf16 = jnp.bfloat16
    h = x
    for i in range(depth):
        h, (w1, w2, wproj, wgate) = _mixer(
            h, norm1_gain[i][None], w_in[i], gmlp_v_gain[i][None],
            w_spatial[i], b_spatial[i], attn_sinks[i], rel_bias_table.T, w_out[i],
            cast=[(w_ff1[i], 1), (w_ff2[i], 0), (w_ple_proj[i], 0), (w_ple_gate[i], 0)])
        h = _ffn(h.reshape(batch * seq, d), p[i].reshape(batch * seq, PLE_DIM),
                 norm2_gain[i][None], w1, w2, wproj, wgate,
                 final_gain[None], final_norm=(i == depth - 1)).reshape(batch, seq, d)
    return h
```

```python
import functools
import math

import jax
import jax.numpy as jnp
import numpy as np
from jax import lax
from jax.experimental import pallas as pl
from jax.experimental.pallas import tpu as pltpu

D_MODEL = 1024
PLE_DIM = 256
D_GMLP = 512
GMLP_GROUPS = 4
GROUP_DIM = D_GMLP // GMLP_GROUPS
CHUNK = 128
D_ATTN = 512
HEAD_DIM = 64
N_Q_HEADS = D_ATTN // HEAD_DIM
N_KV_HEADS = 2
Q_PER_KV = N_Q_HEADS // N_KV_HEADS
WINDOW = 128
BLOCK = WINDOW
REL_BUCKETS = 32
REL_MAX_DIST = 128
D_FF = 4 * D_MODEL
KV_W = N_KV_HEADS * HEAD_DIM
EPS = 1e-6
NEG_INF = -1e30
LOG2_E = math.log2(math.e)

LANES = 128
BF16_SUBLANES = 16
N_VARIANTS = 2 * N_KV_HEADS
N_STACKS = N_Q_HEADS // 2
SEQ_TILE = 512
ROW_TILE = 1024
FF_CHUNK = 1024
FFN_ROW_PART = 256
WEIGHT_STAGE_ROWS = 256
VMEM_LIMIT_BYTES = 56 * 1024 * 1024


def _bucket_thresholds():
    max_exact = REL_BUCKETS // 2
    n = np.arange(WINDOW)
    nf = np.maximum(n, 1).astype(np.float32)
    large = max_exact + (np.log(nf / max_exact) / np.float32(math.log(REL_MAX_DIST / max_exact))
                         * (REL_BUCKETS - max_exact)).astype(np.int32)
    bucket = np.where(n < max_exact, n, np.minimum(large, REL_BUCKETS - 1))
    assert (np.diff(bucket) >= 0).all()
    return [(b, int(np.argmax(bucket >= b))) for b in range(1, REL_BUCKETS) if (bucket >= b).any()]


_BUCKET_THRESHOLDS = _bucket_thresholds()


def _rms(x, gain):
    return x * lax.rsqrt(jnp.mean(x * x, axis=-1, keepdims=True) + EPS) * gain


def _gelu_tanh(x):
    c = float(np.float32(np.sqrt(2 / np.pi)))
    poly = x * x * (-2.0 * c * 0.044715 * LOG2_E) + (-2.0 * c * LOG2_E)
    return x / (1.0 + jnp.exp2(x * poly))


def _dot(a, b):
    return jnp.dot(a, b, preferred_element_type=jnp.float32)


def _build_bias(rel_ref, bias_ref):
    a = lax.broadcasted_iota(jnp.int32, (BLOCK, BLOCK), 0)
    j = lax.broadcasted_iota(jnp.int32, (BLOCK, BLOCK), 1)
    from_prev = j > a
    n = jnp.where(from_prev, BLOCK + a - j, a - j)

    def entry(b, h):
        return jnp.broadcast_to(rel_ref[h:h + 1, b:b + 1], (BLOCK, BLOCK))

    for h in range(N_Q_HEADS):
        val = entry(0, h)
        for b, thr in _BUCKET_THRESHOLDS:
            val = jnp.where(n >= thr, entry(b, h), val)
        stack = 2 * (h // Q_PER_KV) + h % 2
        half = (h % Q_PER_KV) // 2
        val = val * LOG2_E
        bias_ref[0, stack, half] = val
        bias_ref[1, stack, half] = jnp.where(from_prev, NEG_INF, val)


def _build_spatial_bias(bsp_ref, bm_ref):
    row = lax.broadcasted_iota(jnp.int32, (CHUNK, CHUNK), 0)
    padded = jnp.zeros((CHUNK, CHUNK), jnp.float32)
    for g in range(GMLP_GROUPS):
        padded = jnp.where(row == g, jnp.broadcast_to(bsp_ref[g:g + 1, :], (CHUNK, CHUNK)), padded)
    bm_ref[...] = padded.T


def _load_as_bf16(src_hbm, dst_ref, stage_ref, sem_ref, sem_base, between=None):
    n_chunks = src_hbm.shape[0] // stage_ref.shape[1]
    rows = stage_ref.shape[1]

    def copy(k):
        return pltpu.make_async_copy(src_hbm.at[pl.ds(k * rows, rows), :],
                                     stage_ref.at[k % 2], sem_ref.at[sem_base + k % 2])

    copy(0).start()
    if between is not None:
        between()
    for k in range(n_chunks):
        if k + 1 < n_chunks:
            copy(k + 1).start()
        copy(k).wait()
        dst_ref[k * rows:(k + 1) * rows, :] = stage_ref[k % 2].astype(dst_ref.dtype)


_VARIANT = ((0, 1), (2, 3))


def _project_pieces(x_ref, g1_ref, w_in_ref, vgain_ref, dst):
    u_ref, vn_ref, q_ref, kbuf, vbuf = dst
    ts = x_ref.shape[0]
    o = 2 * D_GMLP
    state = {}

    def norm():
        state["hn"] = _rms(x_ref[...], g1_ref[...]).astype(jnp.bfloat16)

    def gate_u():
        u_ref[...] = _gelu_tanh(_dot(state["hn"], w_in_ref[:, 0:D_GMLP]))

    def gate_v():
        vg = _gelu_tanh(_dot(state["hn"], w_in_ref[:, D_GMLP:2 * D_GMLP]))
        for g in range(GMLP_GROUPS):
            cols = slice(g * GROUP_DIM, (g + 1) * GROUP_DIM)
            vn_ref[:, cols] = _rms(vg[:, cols], vgain_ref[:, cols]).astype(vn_ref.dtype)

    def query():
        q = _dot(state["hn"], w_in_ref[:, o:o + D_ATTN]) * (HEAD_DIM ** -0.5 * LOG2_E)
        q_ref[...] = q.astype(q_ref.dtype)

    def key_value():
        kv = _dot(state["hn"], w_in_ref[:, o + D_ATTN:o + D_ATTN + 2 * KV_W])
        k_t = kv[:, 0:KV_W].T.astype(kbuf.dtype)
        zeros = jnp.zeros((HEAD_DIM, ts), kbuf.dtype)
        for variant, head in enumerate((0, 0, 1, 1)):
            k_head = k_t[head * HEAD_DIM:(head + 1) * HEAD_DIM]
            pair = [k_head, zeros] if variant % 2 == 0 else [zeros, k_head]
            kbuf[variant, :, BLOCK:BLOCK + ts] = jnp.concatenate(pair, axis=0)
        v = kv[:, KV_W:2 * KV_W]
        lo = lax.broadcasted_iota(jnp.int32, (ts, LANES), 1) < HEAD_DIM
        vr = pltpu.roll(v, HEAD_DIM, axis=1)
        vbuf[0, BLOCK:BLOCK + ts, :] = jnp.where(lo, v, 0.0).astype(vbuf.dtype)
        vbuf[1, BLOCK:BLOCK + ts, :] = jnp.where(lo, 0.0, vr).astype(vbuf.dtype)
        vbuf[2, BLOCK:BLOCK + ts, :] = jnp.where(lo, vr, 0.0).astype(vbuf.dtype)
        vbuf[3, BLOCK:BLOCK + ts, :] = jnp.where(lo, 0.0, v).astype(vbuf.dtype)

    return [norm, gate_u, gate_v, query, key_value]


def _attend_pieces(i, src, seq_start, sink_ref, bias_ref, mix_ref, s_scr, p_scr):
    _, _, q_ref, kbuf, vbuf = src
    rows = slice(i * BLOCK, (i + 1) * BLOCK)
    keys = slice(i * BLOCK, (i + 2) * BLOCK)
    first = jnp.where(seq_start, 1, 0) if i == 0 else 0
    upper = lax.broadcasted_iota(jnp.int32, (2 * BLOCK, 1), 0) < BLOCK
    query = lax.broadcasted_iota(jnp.int32, (2 * BLOCK, BLOCK), 0) % BLOCK
    from_prev = lax.broadcasted_iota(jnp.int32, (2 * BLOCK, BLOCK), 1) > query
    stacks = [(g, parity) for g in range(N_KV_HEADS) for parity in range(2)]
    state = {}

    def scores():
        for g, parity in stacks:
            qs = jnp.concatenate([q_ref[rows, 2 * g * LANES:(2 * g + 1) * LANES],
                                  q_ref[rows, (2 * g + 1) * LANES:(2 * g + 2) * LANES]], axis=0)
            s = _dot(qs, kbuf[_VARIANT[g][parity], :, keys])
            s = jnp.where(from_prev, s[:, 0:BLOCK], s[:, BLOCK:2 * BLOCK])
            s_scr[N_STACKS * i + 2 * g + parity] = (
                s + bias_ref[first, 2 * g + parity].reshape(2 * BLOCK, BLOCK))

    def softmax():
        zero = jnp.zeros((2 * BLOCK, BLOCK), jnp.bfloat16)
        for g, parity in stacks:
            head = Q_PER_KV * g + parity
            sink = jnp.where(upper, sink_ref[head], sink_ref[head + 2]) * LOG2_E
            s = s_scr[N_STACKS * i + 2 * g + parity]
            m = jnp.maximum(jnp.max(s, axis=-1, keepdims=True), sink)
            e = jnp.exp2(s - m)
            denom = jnp.sum(e, axis=-1, keepdims=True) + jnp.exp2(sink - m)
            p = e.astype(jnp.bfloat16)
            p_scr[N_STACKS * i + 2 * g + parity] = jnp.concatenate(
                [jnp.where(from_prev, p, zero), jnp.where(from_prev, zero, p)], axis=1)
            state["inv", g, parity] = 1.0 / denom

    def values():
        low = lax.broadcasted_iota(jnp.int32, (2 * BLOCK, LANES), 1) < HEAD_DIM
        for g in range(N_KV_HEADS):
            pair0 = D_GMLP + 2 * g * LANES
            acc = (_dot(p_scr[N_STACKS * i + 2 * g], vbuf[_VARIANT[g][0], keys, :])
                   + _dot(p_scr[N_STACKS * i + 2 * g + 1], vbuf[_VARIANT[g][1], keys, :]))
            acc = acc * jnp.where(low, state["inv", g, 0], state["inv", g, 1])
            mix_ref[rows, pair0:pair0 + LANES] = acc[0:BLOCK].astype(mix_ref.dtype)
            mix_ref[rows, pair0 + LANES:pair0 + 2 * LANES] = acc[BLOCK:].astype(mix_ref.dtype)

    return scores, softmax, values


def _spatial_gate(src, wsp_ref, bm_ref, mix_ref):
    u_ref, vn_ref = src[0:2]
    t_idx = lax.broadcasted_iota(jnp.int32, (CHUNK, CHUNK), 0)
    s_idx = lax.broadcasted_iota(jnp.int32, (CHUNK, CHUNK), 1)
    causal = s_idx <= t_idx
    for g in range(GMLP_GROUPS):
        cols = slice(g * GROUP_DIM, (g + 1) * GROUP_DIM)
        w_g = jnp.where(causal, wsp_ref[g], 0.0).astype(jnp.bfloat16)
        b_g = bm_ref[:, g:g + 1]
        for c in range(u_ref.shape[0] // CHUNK):
            rows = slice(c * CHUNK, (c + 1) * CHUNK)
            sv = _dot(w_g, vn_ref[rows, cols]) + b_g
            mix_ref[rows, cols] = (u_ref[rows, cols] * sv).astype(mix_ref.dtype)


N_CAST = 4
SCORES_AHEAD = 2


def _mixer_kernel(x_ref, xprev_ref, g1_ref, w_in_hbm, vgain_ref, wsp_ref, bsp_ref, sink_ref,
                  rel_ref, w_out_hbm, *rest, tiles_per_seq, n_tiles):
    cast_src, o_ref = rest[0:N_CAST], rest[N_CAST]
    cast_dst, scratch = rest[N_CAST + 1:2 * N_CAST + 1], rest[2 * N_CAST + 1:]
    slots = (scratch[0:5], scratch[5:10])
    mix_ref, bias_ref, s_scr, p_scr = scratch[10:14]
    w_in_ref, w_out_ref, stage_in, stage_out, bm_ref, dma_sem = scratch[14:20]
    ts = x_ref.shape[0]
    t = pl.program_id(0)

    cur_starts = t % tiles_per_seq == 0
    prev_started = (t + tiles_per_seq - 1) % tiles_per_seq == 0
    n_blk = ts // BLOCK

    def step(dst, src):
        project = _project_pieces(x_ref, g1_ref, w_in_ref, vgain_ref, dst) if dst else []
        attend = [_attend_pieces(i, src, prev_started, sink_ref, bias_ref, mix_ref, s_scr,
                                 p_scr) for i in range(n_blk)] if src else []
        if src:
            for i in range(SCORES_AHEAD):
                attend[i][0]()
            _spatial_gate(src, wsp_ref, bm_ref, mix_ref)
        if dst:
            project.pop(0)()
        for i in range(n_blk):
            if src:
                if i + SCORES_AHEAD < n_blk:
                    attend[i + SCORES_AHEAD][0]()
                attend[i][1]()
            if dst:
                project.pop(0)()
            if src:
                attend[i][2]()
        assert not project
        if src:
            o_ref[...] = xprev_ref[...] + _dot(mix_ref[...], w_out_ref[...])
        for s, d in zip(cast_src, cast_dst):
            d[...] = s[...].astype(d.dtype)
        if dst:
            zeros = jnp.zeros((N_VARIANTS, BLOCK, LANES), dst[3].dtype)
            dst[3][:, :, 0:BLOCK] = (
                jnp.where(cur_starts, zeros, src[3][:, :, ts:ts + BLOCK]) if src else zeros)
            dst[4][:, 0:BLOCK, :] = (
                jnp.where(cur_starts, zeros, src[4][:, ts:ts + BLOCK, :]) if src else zeros)

    @pl.when(t == 0)
    def _():
        def build_tables():
            _build_bias(rel_ref, bias_ref)
            _build_spatial_bias(bsp_ref, bm_ref)

        _load_as_bf16(w_in_hbm, w_in_ref, stage_in, dma_sem, 0, between=build_tables)
        _load_as_bf16(w_out_hbm, w_out_ref, stage_out, dma_sem, 2)
        step(slots[0], None)

    @pl.when(t == n_tiles)
    def _():
        step(None, slots[(n_tiles - 1) % 2])

    @pl.when((t > 0) & (t < n_tiles) & (t % 2 == 0))
    def _():
        step(slots[0], slots[1])

    @pl.when((t < n_tiles) & (t % 2 == 1))
    def _():
        step(slots[1], slots[0])


def _ffn_kernel(h_ref, p_ref, g2_ref, w1_ref, w2_ref, wproj_ref, wgate_ref, gf_ref, o_ref,
                acc_ref, *, final_norm, n_tiles):
    tm = h_ref.shape[0]
    parts = [slice(r, r + FFN_ROW_PART) for r in range(0, tm, FFN_ROW_PART)]

    def step(expand, finish):
        hidden = []

        def expand_chunk(c, hn_parts):
            cols = slice(c * FF_CHUNK, (c + 1) * FF_CHUNK)
            if c == 0:
                t = jnp.concatenate([_dot(part, w1_ref[:, cols]) for part in hn_parts], axis=0)
            else:
                t = _dot(jnp.concatenate(hn_parts, axis=0), w1_ref[:, cols])
            t = jnp.maximum(t, 0.0)
            hidden.append((t * t).astype(jnp.bfloat16))

        n_chunks = D_FF // FF_CHUNK
        assert len(parts) == n_chunks, "one finishing row part is issued per expand chunk"
        if expand:
            hn_parts = [_rms(h_ref[r, :], g2_ref[...]).astype(jnp.bfloat16) for r in parts]
        if finish:
            ple = _dot(p_ref[...].astype(jnp.bfloat16), wproj_ref[...])
        for c in range(n_chunks):
            if expand:
                expand_chunk(c, hn_parts)
            if finish:
                r = parts[c]
                z = _dot(acc_ref[r, :].astype(jnp.bfloat16), wgate_ref[...])
                out = acc_ref[r, :] + ple[r] / (1.0 + jnp.exp2(z * (-LOG2_E)))
                o_ref[r, :] = _rms(out, gf_ref[...]) if final_norm else out
        if expand:
            acc_ref[...] = h_ref[...] + _dot(jnp.concatenate(hidden, axis=1), w2_ref[...])

    t = pl.program_id(0)

    @pl.when(t == 0)
    def _():
        acc_ref[...] = jnp.zeros(acc_ref.shape, acc_ref.dtype)

    @pl.when(t < n_tiles)
    def _():
        step(True, True)

    @pl.when(t == n_tiles)
    def _():
        step(False, True)


def _resident(shape):
    return pl.BlockSpec(shape, lambda *_: (0,) * len(shape), pipeline_mode=pl.Buffered(1))


def _cast_slice_spec(arr, axis, n_tiles):
    size = arr.shape[axis]
    min_thick = BF16_SUBLANES if axis == 0 else LANES
    thick = max(size // n_tiles, min_thick)
    assert size % thick == 0
    last = size // thick - 1
    block = tuple(thick if a == axis else n for a, n in enumerate(arr.shape))
    return pl.BlockSpec(block, lambda t: tuple(jnp.minimum(t, last) if a == axis else 0
                                               for a in range(arr.ndim)))


def _mixer(x, g1, w_in, vgain, wsp, bsp, sinks, rel, w_out, cast):
    batch, seq, d = x.shape
    ts = SEQ_TILE
    n_blk = ts // BLOCK
    assert n_blk == 4, "stage-1 pieces are paired one-to-one with attention blocks"
    assert len(cast) == N_CAST
    n_tiles = batch * seq // ts
    x2 = x.reshape(batch * seq, d)
    smem = pl.BlockSpec(memory_space=pltpu.SMEM)
    hbm = pl.BlockSpec(memory_space=pl.ANY)
    cast_specs = [_cast_slice_spec(w, axis, n_tiles) for w, axis in cast]
    stage_slot = [
        pltpu.VMEM((ts, D_GMLP), jnp.float32),
        pltpu.VMEM((ts, D_GMLP), jnp.bfloat16),
        pltpu.VMEM((ts, D_ATTN), jnp.bfloat16),
        pltpu.VMEM((N_VARIANTS, LANES, BLOCK + ts), jnp.bfloat16),
        pltpu.VMEM((N_VARIANTS, BLOCK + ts, LANES), jnp.bfloat16),
    ]
    out, *cast_out = pl.pallas_call(
        functools.partial(_mixer_kernel, tiles_per_seq=seq // ts, n_tiles=n_tiles),
        grid=(n_tiles + 1,),
        in_specs=[
            pl.BlockSpec((ts, d), lambda t: (jnp.minimum(t, n_tiles - 1), 0)),
            pl.BlockSpec((ts, d), lambda t: (jnp.maximum(t - 1, 0), 0)),
            _resident(g1.shape), hbm, _resident(vgain.shape),
            _resident(wsp.shape), _resident(bsp.shape), smem, _resident(rel.shape),
            hbm,
        ] + cast_specs,
        out_specs=[pl.BlockSpec((ts, d), lambda t: (jnp.maximum(t - 1, 0), 0))] + cast_specs,
        out_shape=[jax.ShapeDtypeStruct(x2.shape, x2.dtype)]
        + [jax.ShapeDtypeStruct(w.shape, jnp.bfloat16) for w, _ in cast],
        scratch_shapes=stage_slot + stage_slot + [
            pltpu.VMEM((ts, D_MODEL), jnp.bfloat16),
            pltpu.VMEM((2, 4, 2, BLOCK, BLOCK), jnp.float32),
            pltpu.VMEM((n_blk * N_STACKS, 2 * BLOCK, BLOCK), jnp.float32),
            pltpu.VMEM((n_blk * N_STACKS, 2 * BLOCK, 2 * BLOCK), jnp.bfloat16),
            pltpu.VMEM(w_in.shape, jnp.bfloat16),
            pltpu.VMEM(w_out.shape, jnp.bfloat16),
            pltpu.VMEM((2, WEIGHT_STAGE_ROWS, w_in.shape[1]), jnp.float32),
            pltpu.VMEM((2, WEIGHT_STAGE_ROWS, w_out.shape[1]), jnp.float32),
            pltpu.VMEM((CHUNK, LANES), jnp.float32),
            pltpu.SemaphoreType.DMA((4,)),
        ],
        compiler_params=pltpu.CompilerParams(
            dimension_semantics=("arbitrary",),
            vmem_limit_bytes=VMEM_LIMIT_BYTES),
        name="mixer",
    )(x2, x2, g1, w_in, vgain, wsp, bsp, sinks, rel, w_out, *[w for w, _ in cast])
    return out.reshape(batch, seq, d), cast_out


def _ffn(h, p, g2, w1, w2, wproj, wgate, gf, *, final_norm):
    rows, d = h.shape
    tm = ROW_TILE
    n_tiles = rows // tm
    return pl.pallas_call(
        functools.partial(_ffn_kernel, final_norm=final_norm, n_tiles=n_tiles),
        grid=(n_tiles + 1,),
        in_specs=[
            pl.BlockSpec((tm, d), lambda t: (jnp.minimum(t, n_tiles - 1), 0)),
            pl.BlockSpec((tm, PLE_DIM), lambda t: (jnp.maximum(t - 1, 0), 0)),
            _resident(g2.shape), _resident(w1.shape), _resident(w2.shape),
            _resident(wproj.shape), _resident(wgate.shape), _resident(gf.shape),
        ],
        out_specs=pl.BlockSpec((tm, d), lambda t: (jnp.maximum(t - 1, 0), 0)),
        out_shape=jax.ShapeDtypeStruct(h.shape, h.dtype),
        scratch_shapes=[pltpu.VMEM((tm, d), jnp.float32)],
        compiler_params=pltpu.CompilerParams(
            dimension_semantics=("arbitrary",),
            vmem_limit_bytes=VMEM_LIMIT_BYTES),
        name="ffn",
    )(h, p, g2, w1, w2, wproj, wgate, gf)


def kernel(x, p, norm1_gain, w_in, gmlp_v_gain, w_spatial, b_spatial, attn_sinks, rel_bias_table, w_out, norm2_gain, w_ff1, w_ff2, w_ple_proj, w_ple_gate, final_gain):
    batch, seq, d = x.shape
    depth = w_in.shape[0]
    bf16 = jnp.bfloat16
    h = x
    for i in range(depth):
        h, (w1, w2, wproj, wgate) = _mixer(
            h, norm1_gain[i][None], w_in[i], gmlp_v_gain[i][None],
            w_spatial[i], b_spatial[i], attn_sinks[i], rel_bias_table.T, w_out[i],
            cast=[(w_ff1[i], 1), (w_ff2[i], 0), (w_ple_proj[i], 0), (w_ple_gate[i], 0)])
        h = _ffn(h.reshape(batch * seq, d), p[i].reshape(batch * seq, PLE_DIM),
                 norm2_gain[i][None], w1, w2, wproj, wgate,
                 final_gain[None], final_norm=(i == depth - 1)).reshape(batch, seq, d)
    return h
```

```python
import functools
import math

import jax
import jax.numpy as jnp
import numpy as np
from jax import lax
from jax.experimental import pallas as pl
from jax.experimental.pallas import tpu as pltpu

D_MODEL = 1024
PLE_DIM = 256
D_GMLP = 512
GMLP_GROUPS = 4
GROUP_DIM = D_GMLP // GMLP_GROUPS
CHUNK = 128
D_ATTN = 512
HEAD_DIM = 64
N_Q_HEADS = D_ATTN // HEAD_DIM
N_KV_HEADS = 2
Q_PER_KV = N_Q_HEADS // N_KV_HEADS
WINDOW = 128
BLOCK = WINDOW
REL_BUCKETS = 32
REL_MAX_DIST = 128
D_FF = 4 * D_MODEL
KV_W = N_KV_HEADS * HEAD_DIM
EPS = 1e-6
NEG_INF = -1e30
LOG2_E = math.log2(math.e)

LANES = 128
BF16_SUBLANES = 16
N_VARIANTS = 2 * N_KV_HEADS
N_STACKS = N_Q_HEADS // 2
SEQ_TILE = 512
ROW_TILE = 1024
FF_CHUNK = 1024
FFN_ROW_PART = 256
WEIGHT_STAGE_ROWS = 256
VMEM_LIMIT_BYTES = 56 * 1024 * 1024


def _bucket_thresholds():
    max_exact = REL_BUCKETS // 2
    n = np.arange(WINDOW)
    nf = np.maximum(n, 1).astype(np.float32)
    large = max_exact + (np.log(nf / max_exact) / np.float32(math.log(REL_MAX_DIST / max_exact))
                         * (REL_BUCKETS - max_exact)).astype(np.int32)
    bucket = np.where(n < max_exact, n, np.minimum(large, REL_BUCKETS - 1))
    assert (np.diff(bucket) >= 0).all()
    return [(b, int(np.argmax(bucket >= b))) for b in range(1, REL_BUCKETS) if (bucket >= b).any()]


_BUCKET_THRESHOLDS = _bucket_thresholds()


def _rms(x, gain):
    return x * lax.rsqrt(jnp.mean(x * x, axis=-1, keepdims=True) + EPS) * gain


def _gelu_tanh(x):
    c = float(np.float32(np.sqrt(2 / np.pi)))
    poly = x * x * (-2.0 * c * 0.044715 * LOG2_E) + (-2.0 * c * LOG2_E)
    return x / (1.0 + jnp.exp2(x * poly))


def _dot(a, b):
    return jnp.dot(a, b, preferred_element_type=jnp.float32)


def _dot_nt(a, b):
    return lax.dot_general(a, b, (((1,), (1,)), ((), ())), preferred_element_type=jnp.float32)


def _build_bias(rel_ref, bias_ref):
    a = lax.broadcasted_iota(jnp.int32, (BLOCK, BLOCK), 0)
    j = lax.broadcasted_iota(jnp.int32, (BLOCK, BLOCK), 1)
    from_prev = j > a
    n = jnp.where(from_prev, BLOCK + a - j, a - j)

    def entry(b, h):
        return jnp.broadcast_to(rel_ref[h:h + 1, b:b + 1], (BLOCK, BLOCK))

    for h in range(N_Q_HEADS):
        val = entry(0, h)
        for b, thr in _BUCKET_THRESHOLDS:
            val = jnp.where(n >= thr, entry(b, h), val)
        stack = 2 * (h // Q_PER_KV) + h % 2
        half = (h % Q_PER_KV) // 2
        val = val * LOG2_E
        bias_ref[0, stack, half] = val
        bias_ref[1, stack, half] = jnp.where(from_prev, NEG_INF, val)


def _build_spatial_bias(bsp_ref, bm_ref):
    row = lax.broadcasted_iota(jnp.int32, (CHUNK, CHUNK), 0)
    padded = jnp.zeros((CHUNK, CHUNK), jnp.float32)
    for g in range(GMLP_GROUPS):
        padded = jnp.where(row == g, jnp.broadcast_to(bsp_ref[g:g + 1, :], (CHUNK, CHUNK)), padded)
    bm_ref[...] = padded.T


def _load_as_bf16(src_hbm, dst_ref, stage_ref, sem_ref, sem_base, between=None):
    n_chunks = src_hbm.shape[0] // stage_ref.shape[1]
    rows = stage_ref.shape[1]

    def copy(k):
        return pltpu.make_async_copy(src_hbm.at[pl.ds(k * rows, rows), :],
                                     stage_ref.at[k % 2], sem_ref.at[sem_base + k % 2])

    copy(0).start()
    if between is not None:
        between()
    for k in range(n_chunks):
        if k + 1 < n_chunks:
            copy(k + 1).start()
        copy(k).wait()
        dst_ref[k * rows:(k + 1) * rows, :] = stage_ref[k % 2].astype(dst_ref.dtype)


_VARIANT = ((0, 1), (2, 3))


def _project_pieces(x_ref, g1_ref, w_in_ref, vgain_ref, dst):
    u_ref, vn_ref, q_ref, kbuf, vbuf = dst
    ts = x_ref.shape[0]
    o = 2 * D_GMLP
    state = {}

    def norm():
        state["hn"] = _rms(x_ref[...], g1_ref[...]).astype(jnp.bfloat16)

    def gate_u():
        u_ref[...] = _gelu_tanh(_dot(state["hn"], w_in_ref[:, 0:D_GMLP]))

    def gate_v():
        vg = _gelu_tanh(_dot(state["hn"], w_in_ref[:, D_GMLP:2 * D_GMLP]))
        for g in range(GMLP_GROUPS):
            cols = slice(g * GROUP_DIM, (g + 1) * GROUP_DIM)
            vn_ref[:, cols] = _rms(vg[:, cols], vgain_ref[:, cols]).astype(vn_ref.dtype)

    def query():
        q = _dot(state["hn"], w_in_ref[:, o:o + D_ATTN]) * (HEAD_DIM ** -0.5 * LOG2_E)
        q_ref[...] = q.astype(q_ref.dtype)

    def key_value():
        kv = _dot(state["hn"], w_in_ref[:, o + D_ATTN:o + D_ATTN + 2 * KV_W])
        lo = lax.broadcasted_iota(jnp.int32, (ts, LANES), 1) < HEAD_DIM
        for buf, t in ((kbuf, kv[:, 0:KV_W]), (vbuf, kv[:, KV_W:2 * KV_W])):
            tr = pltpu.roll(t, HEAD_DIM, axis=1)
            buf[0, BLOCK:BLOCK + ts, :] = jnp.where(lo, t, 0.0).astype(buf.dtype)
            buf[1, BLOCK:BLOCK + ts, :] = jnp.where(lo, 0.0, tr).astype(buf.dtype)
            buf[2, BLOCK:BLOCK + ts, :] = jnp.where(lo, tr, 0.0).astype(buf.dtype)
            buf[3, BLOCK:BLOCK + ts, :] = jnp.where(lo, 0.0, t).astype(buf.dtype)

    return [norm, gate_u, gate_v, query, key_value]


def _attend_pieces(i, src, seq_start, sink_ref, bias_ref, mix_ref, s_scr, p_scr):
    _, _, q_ref, kbuf, vbuf = src
    rows = slice(i * BLOCK, (i + 1) * BLOCK)
    keys = slice(i * BLOCK, (i + 2) * BLOCK)
    first = jnp.where(seq_start, 1, 0) if i == 0 else 0
    upper = lax.broadcasted_iota(jnp.int32, (2 * BLOCK, 1), 0) < BLOCK
    query = lax.broadcasted_iota(jnp.int32, (2 * BLOCK, BLOCK), 0) % BLOCK
    from_prev = lax.broadcasted_iota(jnp.int32, (2 * BLOCK, BLOCK), 1) > query
    stacks = [(g, parity) for g in range(N_KV_HEADS) for parity in range(2)]
    state = {}

    def scores():
        for g, parity in stacks:
            qs = jnp.concatenate([q_ref[rows, 2 * g * LANES:(2 * g + 1) * LANES],
                                  q_ref[rows, (2 * g + 1) * LANES:(2 * g + 2) * LANES]], axis=0)
            s = _dot_nt(qs, kbuf[_VARIANT[g][parity], keys, :])
            s = jnp.where(from_prev, s[:, 0:BLOCK], s[:, BLOCK:2 * BLOCK])
            s_scr[N_STACKS * i + 2 * g + parity] = (
                s + bias_ref[first, 2 * g + parity].reshape(2 * BLOCK, BLOCK))

    def softmax():
        zero = jnp.zeros((2 * BLOCK, BLOCK), jnp.bfloat16)
        for g, parity in stacks:
            head = Q_PER_KV * g + parity
            sink = jnp.where(upper, sink_ref[head], sink_ref[head + 2]) * LOG2_E
            s = s_scr[N_STACKS * i + 2 * g + parity]
            m = jnp.maximum(jnp.max(s, axis=-1, keepdims=True), sink)
            e = jnp.exp2(s - m)
            denom = jnp.sum(e, axis=-1, keepdims=True) + jnp.exp2(sink - m)
            p = e.astype(jnp.bfloat16)
            p_scr[N_STACKS * i + 2 * g + parity] = jnp.concatenate(
                [jnp.where(from_prev, p, zero), jnp.where(from_prev, zero, p)], axis=1)
            state["inv", g, parity] = 1.0 / denom

    def values():
        low = lax.broadcasted_iota(jnp.int32, (2 * BLOCK, LANES), 1) < HEAD_DIM
        for g in range(N_KV_HEADS):
            pair0 = D_GMLP + 2 * g * LANES
            acc = (_dot(p_scr[N_STACKS * i + 2 * g], vbuf[_VARIANT[g][0], keys, :])
                   + _dot(p_scr[N_STACKS * i + 2 * g + 1], vbuf[_VARIANT[g][1], keys, :]))
            acc = acc * jnp.where(low, state["inv", g, 0], state["inv", g, 1])
            mix_ref[rows, pair0:pair0 + LANES] = acc[0:BLOCK].astype(mix_ref.dtype)
            mix_ref[rows, pair0 + LANES:pair0 + 2 * LANES] = acc[BLOCK:].astype(mix_ref.dtype)

    return scores, softmax, values


def _spatial_gate(src, wsp_ref, bm_ref, mix_ref):
    u_ref, vn_ref = src[0:2]
    t_idx = lax.broadcasted_iota(jnp.int32, (CHUNK, CHUNK), 0)
    s_idx = lax.broadcasted_iota(jnp.int32, (CHUNK, CHUNK), 1)
    causal = s_idx <= t_idx
    for g in range(GMLP_GROUPS):
        cols = slice(g * GROUP_DIM, (g + 1) * GROUP_DIM)
        w_g = jnp.where(causal, wsp_ref[g], 0.0).astype(jnp.bfloat16)
        b_g = bm_ref[:, g:g + 1]
        for c in range(u_ref.shape[0] // CHUNK):
            rows = slice(c * CHUNK, (c + 1) * CHUNK)
            sv = _dot(w_g, vn_ref[rows, cols]) + b_g
            mix_ref[rows, cols] = (u_ref[rows, cols] * sv).astype(mix_ref.dtype)


N_CAST = 4
SCORES_AHEAD = 2


def _mixer_kernel(x_ref, xprev_ref, g1_ref, w_in_hbm, vgain_ref, wsp_ref, bsp_ref, sink_ref,
                  rel_ref, w_out_hbm, *rest, tiles_per_seq, n_tiles):
    cast_src, o_ref = rest[0:N_CAST], rest[N_CAST]
    cast_dst, scratch = rest[N_CAST + 1:2 * N_CAST + 1], rest[2 * N_CAST + 1:]
    slots = (scratch[0:5], scratch[5:10])
    mix_ref, bias_ref, s_scr, p_scr = scratch[10:14]
    w_in_ref, w_out_ref, stage_in, stage_out, bm_ref, dma_sem = scratch[14:20]
    ts = x_ref.shape[0]
    t = pl.program_id(0)

    cur_starts = t % tiles_per_seq == 0
    prev_started = (t + tiles_per_seq - 1) % tiles_per_seq == 0
    n_blk = ts // BLOCK

    def step(dst, src):
        project = _project_pieces(x_ref, g1_ref, w_in_ref, vgain_ref, dst) if dst else []
        attend = [_attend_pieces(i, src, prev_started, sink_ref, bias_ref, mix_ref, s_scr,
                                 p_scr) for i in range(n_blk)] if src else []
        if src:
            for i in range(SCORES_AHEAD):
                attend[i][0]()
            _spatial_gate(src, wsp_ref, bm_ref, mix_ref)
        if dst:
            project.pop(0)()
        for i in range(n_blk):
            if src:
                if i + SCORES_AHEAD < n_blk:
                    attend[i + SCORES_AHEAD][0]()
                attend[i][1]()
            if dst:
                project.pop(0)()
            if src:
                attend[i][2]()
        assert not project
        if src:
            o_ref[...] = xprev_ref[...] + _dot(mix_ref[...], w_out_ref[...])
        for s, d in zip(cast_src, cast_dst):
            d[...] = s[...].astype(d.dtype)
        if dst:
            zeros = jnp.zeros((N_VARIANTS, BLOCK, LANES), dst[3].dtype)
            for k in (3, 4):
                dst[k][:, 0:BLOCK, :] = (
                    jnp.where(cur_starts, zeros, src[k][:, ts:ts + BLOCK, :]) if src else zeros)

    @pl.when(t == 0)
    def _():
        def build_tables():
            _build_bias(rel_ref, bias_ref)
            _build_spatial_bias(bsp_ref, bm_ref)

        fetch_w_out = pltpu.make_async_copy(w_out_hbm, stage_out, dma_sem.at[2])
        fetch_w_out.start()
        _load_as_bf16(w_in_hbm, w_in_ref, stage_in, dma_sem, 0, between=build_tables)
        step(slots[0], None)
        fetch_w_out.wait()
        w_out_ref[...] = stage_out[...].astype(w_out_ref.dtype)

    @pl.when(t == n_tiles)
    def _():
        step(None, slots[(n_tiles - 1) % 2])

    @pl.when((t > 0) & (t < n_tiles) & (t % 2 == 0))
    def _():
        step(slots[0], slots[1])

    @pl.when((t < n_tiles) & (t % 2 == 1))
    def _():
        step(slots[1], slots[0])


def _ffn_kernel(h_ref, p_ref, g2_ref, w1_ref, w2_ref, wproj_ref, wgate_ref, gf_ref, o_ref,
                acc_ref, *, final_norm, n_tiles):
    tm = h_ref.shape[0]
    parts = [slice(r, r + FFN_ROW_PART) for r in range(0, tm, FFN_ROW_PART)]

    def step(expand, finish):
        hidden = []

        def expand_chunk(c, hn_parts):
            cols = slice(c * FF_CHUNK, (c + 1) * FF_CHUNK)
            if c == 0:
                t = jnp.concatenate([_dot(part, w1_ref[:, cols]) for part in hn_parts], axis=0)
            else:
                t = _dot(jnp.concatenate(hn_parts, axis=0), w1_ref[:, cols])
            t = jnp.maximum(t, 0.0)
            hidden.append((t * t).astype(jnp.bfloat16))

        n_chunks = D_FF // FF_CHUNK
        assert len(parts) == n_chunks, "one finishing row part is issued per expand chunk"
        if expand:
            hn_parts = [_rms(h_ref[r, :], g2_ref[...]).astype(jnp.bfloat16) for r in parts]
        if finish:
            ple = _dot(p_ref[...].astype(jnp.bfloat16), wproj_ref[...])
        for c in range(n_chunks):
            if expand:
                expand_chunk(c, hn_parts)
            if finish:
                r = parts[c]
                z = _dot(acc_ref[r, :].astype(jnp.bfloat16), wgate_ref[...])
                out = acc_ref[r, :] + ple[r] / (1.0 + jnp.exp2(z * (-LOG2_E)))
                o_ref[r, :] = _rms(out, gf_ref[...]) if final_norm else out
        if expand:
            acc_ref[...] = h_ref[...] + _dot(jnp.concatenate(hidden, axis=1), w2_ref[...])

    t = pl.program_id(0)

    @pl.when(t == 0)
    def _():
        acc_ref[...] = jnp.zeros(acc_ref.shape, acc_ref.dtype)

    @pl.when(t < n_tiles)
    def _():
        step(True, True)

    @pl.when(t == n_tiles)
    def _():
        step(False, True)


def _resident(shape):
    return pl.BlockSpec(shape, lambda *_: (0,) * len(shape), pipeline_mode=pl.Buffered(1))


def _cast_slice_spec(arr, axis, n_tiles):
    size = arr.shape[axis]
    min_thick = BF16_SUBLANES if axis == 0 else LANES
    thick = max(size // n_tiles, min_thick)
    assert size % thick == 0
    last = size // thick - 1
    block = tuple(thick if a == axis else n for a, n in enumerate(arr.shape))
    return pl.BlockSpec(block, lambda t: tuple(jnp.minimum(t, last) if a == axis else 0
                                               for a in range(arr.ndim)))


def _mixer(x, g1, w_in, vgain, wsp, bsp, sinks, rel, w_out, cast):
    batch, seq, d = x.shape
    ts = SEQ_TILE
    n_blk = ts // BLOCK
    assert n_blk == 4, "stage-1 pieces are paired one-to-one with attention blocks"
    assert len(cast) == N_CAST
    n_tiles = batch * seq // ts
    x2 = x.reshape(batch * seq, d)
    smem = pl.BlockSpec(memory_space=pltpu.SMEM)
    hbm = pl.BlockSpec(memory_space=pl.ANY)
    cast_specs = [_cast_slice_spec(w, axis, n_tiles) for w, axis in cast]
    stage_slot = [
        pltpu.VMEM((ts, D_GMLP), jnp.float32),
        pltpu.VMEM((ts, D_GMLP), jnp.bfloat16),
        pltpu.VMEM((ts, D_ATTN), jnp.bfloat16),
        pltpu.VMEM((N_VARIANTS, BLOCK + ts, LANES), jnp.bfloat16),
        pltpu.VMEM((N_VARIANTS, BLOCK + ts, LANES), jnp.bfloat16),
    ]
    out, *cast_out = pl.pallas_call(
        functools.partial(_mixer_kernel, tiles_per_seq=seq // ts, n_tiles=n_tiles),
        grid=(n_tiles + 1,),
        in_specs=[
            pl.BlockSpec((ts, d), lambda t: (jnp.minimum(t, n_tiles - 1), 0)),
            pl.BlockSpec((ts, d), lambda t: (jnp.maximum(t - 1, 0), 0)),
            _resident(g1.shape), hbm, _resident(vgain.shape),
            _resident(wsp.shape), _resident(bsp.shape), smem, _resident(rel.shape),
            hbm,
        ] + cast_specs,
        out_specs=[pl.BlockSpec((ts, d), lambda t: (jnp.maximum(t - 1, 0), 0))] + cast_specs,
        out_shape=[jax.ShapeDtypeStruct(x2.shape, x2.dtype)]
        + [jax.ShapeDtypeStruct(w.shape, jnp.bfloat16) for w, _ in cast],
        scratch_shapes=stage_slot + stage_slot + [
            pltpu.VMEM((ts, D_MODEL), jnp.bfloat16),
            pltpu.VMEM((2, 4, 2, BLOCK, BLOCK), jnp.float32),
            pltpu.VMEM((n_blk * N_STACKS, 2 * BLOCK, BLOCK), jnp.float32),
            pltpu.VMEM((n_blk * N_STACKS, 2 * BLOCK, 2 * BLOCK), jnp.bfloat16),
            pltpu.VMEM(w_in.shape, jnp.bfloat16),
            pltpu.VMEM(w_out.shape, jnp.bfloat16),
            pltpu.VMEM((2, WEIGHT_STAGE_ROWS, w_in.shape[1]), jnp.float32),
            pltpu.VMEM(w_out.shape, jnp.float32),
            pltpu.VMEM((CHUNK, LANES), jnp.float32),
            pltpu.SemaphoreType.DMA((3,)),
        ],
        compiler_params=pltpu.CompilerParams(
            dimension_semantics=("arbitrary",),
            vmem_limit_bytes=VMEM_LIMIT_BYTES),
        name="mixer",
    )(x2, x2, g1, w_in, vgain, wsp, bsp, sinks, rel, w_out, *[w for w, _ in cast])
    return out.reshape(batch, seq, d), cast_out


def _ffn(h, p, g2, w1, w2, wproj, wgate, gf, *, final_norm):
    rows, d = h.shape
    tm = ROW_TILE
    n_tiles = rows // tm
    return pl.pallas_call(
        functools.partial(_ffn_kernel, final_norm=final_norm, n_tiles=n_tiles),
        grid=(n_tiles + 1,),
        in_specs=[
            pl.BlockSpec((tm, d), lambda t: (jnp.minimum(t, n_tiles - 1), 0)),
            pl.BlockSpec((tm, PLE_DIM), lambda t: (jnp.maximum(t - 1, 0), 0)),
            _resident(g2.shape), _resident(w1.shape), _resident(w2.shape),
            _resident(wproj.shape), _resident(wgate.shape), _resident(gf.shape),
        ],
        out_specs=pl.BlockSpec((tm, d), lambda t: (jnp.maximum(t - 1, 0), 0)),
        out_shape=jax.ShapeDtypeStruct(h.shape, h.dtype),
        scratch_shapes=[pltpu.VMEM((tm, d), jnp.float32)],
        compiler_params=pltpu.CompilerParams(
            dimension_semantics=("arbitrary",),
            vmem_limit_bytes=VMEM_LIMIT_BYTES),
        name="ffn",
    )(h, p, g2, w1, w2, wproj, wgate, gf)


def kernel(x, p, norm1_gain, w_in, gmlp_v_gain, w_spatial, b_spatial, attn_sinks, rel_bias_table, w_out, norm2_gain, w_ff1, w_ff2, w_ple_proj, w_ple_gate, final_gain):
    batch, seq, d = x.shape
    depth = w_in.shape[0]
    bf16 = jnp.bfloat16
    h = x
    for i in range(depth):
        h, (w1, w2, wproj, wgate) = _mixer(
            h, norm1_gain[i][None], w_in[i], gmlp_v_gain[i][None],
            w_spatial[i], b_spatial[i], attn_sinks[i], rel_bias_table.T, w_out[i],
            cast=[(w_ff1[i], 1), (w_ff2[i], 0), (w_ple_proj[i], 0), (w_ple_gate[i], 0)])
        h = _ffn(h.reshape(batch * seq, d), p[i].reshape(batch * seq, PLE_DIM),
                 norm2_gain[i][None], w1, w2, wproj, wgate,
                 final_gain[None], final_norm=(i == depth - 1)).reshape(batch, seq, d)
    return h
```

```python
import functools
import math

import jax
import jax.numpy as jnp
import numpy as np
from jax import lax
from jax.experimental import pallas as pl
from jax.experimental.pallas import tpu as pltpu

D_MODEL = 1024
PLE_DIM = 256
D_GMLP = 512
GMLP_GROUPS = 4
GROUP_DIM = D_GMLP // GMLP_GROUPS
CHUNK = 128
D_ATTN = 512
HEAD_DIM = 64
N_Q_HEADS = D_ATTN // HEAD_DIM
N_KV_HEADS = 2
Q_PER_KV = N_Q_HEADS // N_KV_HEADS
WINDOW = 128
BLOCK = WINDOW
REL_BUCKETS = 32
REL_MAX_DIST = 128
D_FF = 4 * D_MODEL
KV_W = N_KV_HEADS * HEAD_DIM
EPS = 1e-6
NEG_INF = -1e30
LOG2_E = math.log2(math.e)

LANES = 128
BF16_SUBLANES = 16
N_VARIANTS = 2 * N_KV_HEADS
N_STACKS = N_Q_HEADS // 2
SEQ_TILE = 512
ROW_TILE = 1024
FF_CHUNK = 1024
FFN_ROW_PART = 256
WEIGHT_STAGE_ROWS = 256
VMEM_LIMIT_BYTES = 56 * 1024 * 1024


def _bucket_thresholds():
    max_exact = REL_BUCKETS // 2
    n = np.arange(WINDOW)
    nf = np.maximum(n, 1).astype(np.float32)
    large = max_exact + (np.log(nf / max_exact) / np.float32(math.log(REL_MAX_DIST / max_exact))
                         * (REL_BUCKETS - max_exact)).astype(np.int32)
    bucket = np.where(n < max_exact, n, np.minimum(large, REL_BUCKETS - 1))
    assert (np.diff(bucket) >= 0).all()
    return [(b, int(np.argmax(bucket >= b))) for b in range(1, REL_BUCKETS) if (bucket >= b).any()]


_BUCKET_THRESHOLDS = _bucket_thresholds()


def _rms(x, gain):
    return x * lax.rsqrt(jnp.mean(x * x, axis=-1, keepdims=True) + EPS) * gain


def _gelu_tanh(x):
    c = float(np.float32(np.sqrt(2 / np.pi)))
    poly = x * x * (-2.0 * c * 0.044715 * LOG2_E) + (-2.0 * c * LOG2_E)
    return x / (1.0 + jnp.exp2(x * poly))


def _dot(a, b):
    return jnp.dot(a, b, preferred_element_type=jnp.float32)


def _dot_nt(a, b):
    return lax.dot_general(a, b, (((1,), (1,)), ((), ())), preferred_element_type=jnp.float32)


def _build_bias(rel_ref, bias_ref):
    a = lax.broadcasted_iota(jnp.int32, (BLOCK, BLOCK), 0)
    j = lax.broadcasted_iota(jnp.int32, (BLOCK, BLOCK), 1)
    from_prev = j > a
    n = jnp.where(from_prev, BLOCK + a - j, a - j)

    def entry(b, h):
        return jnp.broadcast_to(rel_ref[h:h + 1, b:b + 1], (BLOCK, BLOCK))

    for h in range(N_Q_HEADS):
        val = entry(0, h)
        for b, thr in _BUCKET_THRESHOLDS:
            val = jnp.where(n >= thr, entry(b, h), val)
        stack = 2 * (h // Q_PER_KV) + h % 2
        half = (h % Q_PER_KV) // 2
        val = val * LOG2_E
        bias_ref[0, stack, half] = val
        bias_ref[1, stack, half] = jnp.where(from_prev, NEG_INF, val)


def _build_spatial_bias(bsp_ref, bm_ref):
    row = lax.broadcasted_iota(jnp.int32, (CHUNK, CHUNK), 0)
    padded = jnp.zeros((CHUNK, CHUNK), jnp.float32)
    for g in range(GMLP_GROUPS):
        padded = jnp.where(row == g, jnp.broadcast_to(bsp_ref[g:g + 1, :], (CHUNK, CHUNK)), padded)
    bm_ref[...] = padded.T


def _load_as_bf16(src_hbm, dst_ref, stage_ref, sem_ref, sem_base, between=None):
    n_chunks = src_hbm.shape[0] // stage_ref.shape[1]
    rows = stage_ref.shape[1]

    def copy(k):
        return pltpu.make_async_copy(src_hbm.at[pl.ds(k * rows, rows), :],
                                     stage_ref.at[k % 2], sem_ref.at[sem_base + k % 2])

    copy(0).start()
    if between is not None:
        between()
    for k in range(n_chunks):
        if k + 1 < n_chunks:
            copy(k + 1).start()
        copy(k).wait()
        dst_ref[k * rows:(k + 1) * rows, :] = stage_ref[k % 2].astype(dst_ref.dtype)


_VARIANT = ((0, 1), (2, 3))


def _project_pieces(x_ref, g1_ref, w_in_ref, vgain_ref, dst):
    u_ref, vn_ref, q_ref, kbuf, vbuf, x_keep = dst
    ts = x_ref.shape[0]
    o = 2 * D_GMLP
    state = {}

    def norm():
        x = x_ref[...]
        x_keep[...] = x
        state["hn"] = _rms(x, g1_ref[...]).astype(jnp.bfloat16)

    def gate_u():
        u_ref[...] = _gelu_tanh(_dot(state["hn"], w_in_ref[:, 0:D_GMLP]))

    def gate_v():
        vg = _gelu_tanh(_dot(state["hn"], w_in_ref[:, D_GMLP:2 * D_GMLP]))
        for g in range(GMLP_GROUPS):
            cols = slice(g * GROUP_DIM, (g + 1) * GROUP_DIM)
            vn_ref[:, cols] = _rms(vg[:, cols], vgain_ref[:, cols]).astype(vn_ref.dtype)

    def query():
        q = _dot(state["hn"], w_in_ref[:, o:o + D_ATTN]) * (HEAD_DIM ** -0.5 * LOG2_E)
        q_ref[...] = q.astype(q_ref.dtype)

    def key_value():
        kv = _dot(state["hn"], w_in_ref[:, o + D_ATTN:o + D_ATTN + 2 * KV_W])
        lo = lax.broadcasted_iota(jnp.int32, (ts, LANES), 1) < HEAD_DIM
        for buf, t in ((kbuf, kv[:, 0:KV_W]), (vbuf, kv[:, KV_W:2 * KV_W])):
            tr = pltpu.roll(t, HEAD_DIM, axis=1)
            buf[0, BLOCK:BLOCK + ts, :] = jnp.where(lo, t, 0.0).astype(buf.dtype)
            buf[1, BLOCK:BLOCK + ts, :] = jnp.where(lo, 0.0, tr).astype(buf.dtype)
            buf[2, BLOCK:BLOCK + ts, :] = jnp.where(lo, tr, 0.0).astype(buf.dtype)
            buf[3, BLOCK:BLOCK + ts, :] = jnp.where(lo, 0.0, t).astype(buf.dtype)

    return [norm, gate_u, gate_v, query, key_value]


def _attend_pieces(i, src, seq_start, sink_ref, bias_ref, mix_ref, s_scr, p_scr):
    q_ref, kbuf, vbuf = src[2:5]
    rows = slice(i * BLOCK, (i + 1) * BLOCK)
    keys = slice(i * BLOCK, (i + 2) * BLOCK)
    first = jnp.where(seq_start, 1, 0) if i == 0 else 0
    upper = lax.broadcasted_iota(jnp.int32, (2 * BLOCK, 1), 0) < BLOCK
    query = lax.broadcasted_iota(jnp.int32, (2 * BLOCK, BLOCK), 0) % BLOCK
    from_prev = lax.broadcasted_iota(jnp.int32, (2 * BLOCK, BLOCK), 1) > query
    stacks = [(g, parity) for g in range(N_KV_HEADS) for parity in range(2)]
    state = {}

    def scores():
        for g, parity in stacks:
            qs = jnp.concatenate([q_ref[rows, 2 * g * LANES:(2 * g + 1) * LANES],
                                  q_ref[rows, (2 * g + 1) * LANES:(2 * g + 2) * LANES]], axis=0)
            s = _dot_nt(qs, kbuf[_VARIANT[g][parity], keys, :])
            s = jnp.where(from_prev, s[:, 0:BLOCK], s[:, BLOCK:2 * BLOCK])
            s_scr[N_STACKS * i + 2 * g + parity] = (
                s + bias_ref[first, 2 * g + parity].reshape(2 * BLOCK, BLOCK))

    def softmax():
        zero = jnp.zeros((2 * BLOCK, BLOCK), jnp.bfloat16)
        for g, parity in stacks:
            head = Q_PER_KV * g + parity
            sink = jnp.where(upper, sink_ref[head], sink_ref[head + 2]) * LOG2_E
            s = s_scr[N_STACKS * i + 2 * g + parity]
            m = jnp.maximum(jnp.max(s, axis=-1, keepdims=True), sink)
            e = jnp.exp2(s - m)
            denom = jnp.sum(e, axis=-1, keepdims=True) + jnp.exp2(sink - m)
            p = e.astype(jnp.bfloat16)
            p_scr[N_STACKS * i + 2 * g + parity] = jnp.concatenate(
                [jnp.where(from_prev, p, zero), jnp.where(from_prev, zero, p)], axis=1)
            state["inv", g, parity] = 1.0 / denom

    def values():
        low = lax.broadcasted_iota(jnp.int32, (2 * BLOCK, LANES), 1) < HEAD_DIM
        for g in range(N_KV_HEADS):
            pair0 = D_GMLP + 2 * g * LANES
            acc = (_dot(p_scr[N_STACKS * i + 2 * g], vbuf[_VARIANT[g][0], keys, :])
                   + _dot(p_scr[N_STACKS * i + 2 * g + 1], vbuf[_VARIANT[g][1], keys, :]))
            acc = acc * jnp.where(low, state["inv", g, 0], state["inv", g, 1])
            mix_ref[rows, pair0:pair0 + LANES] = acc[0:BLOCK].astype(mix_ref.dtype)
            mix_ref[rows, pair0 + LANES:pair0 + 2 * LANES] = acc[BLOCK:].astype(mix_ref.dtype)

    return scores, softmax, values


def _spatial_gate(src, wsp_ref, bm_ref, mix_ref):
    u_ref, vn_ref = src[0:2]
    t_idx = lax.broadcasted_iota(jnp.int32, (CHUNK, CHUNK), 0)
    s_idx = lax.broadcasted_iota(jnp.int32, (CHUNK, CHUNK), 1)
    causal = s_idx <= t_idx
    for g in range(GMLP_GROUPS):
        cols = slice(g * GROUP_DIM, (g + 1) * GROUP_DIM)
        w_g = jnp.where(causal, wsp_ref[g], 0.0).astype(jnp.bfloat16)
        b_g = bm_ref[:, g:g + 1]
        for c in range(u_ref.shape[0] // CHUNK):
            rows = slice(c * CHUNK, (c + 1) * CHUNK)
            sv = _dot(w_g, vn_ref[rows, cols]) + b_g
            mix_ref[rows, cols] = (u_ref[rows, cols] * sv).astype(mix_ref.dtype)


N_CAST = 4
SCORES_AHEAD = 2


def _mixer_kernel(x_ref, g1_ref, w_in_hbm, vgain_ref, wsp_ref, bsp_ref, sink_ref,
                  rel_ref, w_out_hbm, *rest, tiles_per_seq, n_tiles):
    cast_src, o_ref = rest[0:N_CAST], rest[N_CAST]
    cast_dst, scratch = rest[N_CAST + 1:2 * N_CAST + 1], rest[2 * N_CAST + 1:]
    slots = (scratch[0:6], scratch[6:12])
    mix_ref, bias_ref, s_scr, p_scr = scratch[12:16]
    w_in_ref, w_out_ref, stage_in, stage_out, bm_ref, dma_sem = scratch[16:22]
    ts = x_ref.shape[0]
    t = pl.program_id(0)

    cur_starts = t % tiles_per_seq == 0
    prev_started = (t + tiles_per_seq - 1) % tiles_per_seq == 0
    n_blk = ts // BLOCK

    def step(dst, src):
        project = _project_pieces(x_ref, g1_ref, w_in_ref, vgain_ref, dst) if dst else []
        attend = [_attend_pieces(i, src, prev_started, sink_ref, bias_ref, mix_ref, s_scr,
                                 p_scr) for i in range(n_blk)] if src else []
        if src:
            for i in range(SCORES_AHEAD):
                attend[i][0]()
            _spatial_gate(src, wsp_ref, bm_ref, mix_ref)
        if dst:
            project.pop(0)()
        for i in range(n_blk):
            if src:
                if i + SCORES_AHEAD < n_blk:
                    attend[i + SCORES_AHEAD][0]()
                attend[i][1]()
            if dst:
                project.pop(0)()
            if src:
                attend[i][2]()
        assert not project
        if src:
            o_ref[...] = src[5][...] + _dot(mix_ref[...], w_out_ref[...])
        for s, d in zip(cast_src, cast_dst):
            d[...] = s[...].astype(d.dtype)
        if dst:
            zeros = jnp.zeros((N_VARIANTS, BLOCK, LANES), dst[3].dtype)
            for k in (3, 4):
                dst[k][:, 0:BLOCK, :] = (
                    jnp.where(cur_starts, zeros, src[k][:, ts:ts + BLOCK, :]) if src else zeros)

    @pl.when(t == 0)
    def _():
        def build_tables():
            _build_bias(rel_ref, bias_ref)
            _build_spatial_bias(bsp_ref, bm_ref)

        fetch_w_out = pltpu.make_async_copy(w_out_hbm, stage_out, dma_sem.at[2])
        fetch_w_out.start()
        _load_as_bf16(w_in_hbm, w_in_ref, stage_in, dma_sem, 0, between=build_tables)
        step(slots[0], None)
        fetch_w_out.wait()
        w_out_ref[...] = stage_out[...].astype(w_out_ref.dtype)

    @pl.when(t == n_tiles)
    def _():
        step(None, slots[(n_tiles - 1) % 2])

    @pl.when((t > 0) & (t < n_tiles) & (t % 2 == 0))
    def _():
        step(slots[0], slots[1])

    @pl.when((t < n_tiles) & (t % 2 == 1))
    def _():
        step(slots[1], slots[0])


def _ffn_kernel(h_ref, p_ref, g2_ref, w1_ref, w2_ref, wproj_ref, wgate_ref, gf_ref, o_ref,
                acc_ref, *, final_norm, n_tiles):
    tm = h_ref.shape[0]
    parts = [slice(r, r + FFN_ROW_PART) for r in range(0, tm, FFN_ROW_PART)]

    def step(expand, finish):
        hidden = []

        def expand_chunk(c, hn_parts):
            cols = slice(c * FF_CHUNK, (c + 1) * FF_CHUNK)
            if c == 0:
                t = jnp.concatenate([_dot(part, w1_ref[:, cols]) for part in hn_parts], axis=0)
            else:
                t = _dot(jnp.concatenate(hn_parts, axis=0), w1_ref[:, cols])
            t = jnp.maximum(t, 0.0)
            hidden.append((t * t).astype(jnp.bfloat16))

        n_chunks = D_FF // FF_CHUNK
        assert len(parts) == n_chunks, "one finishing row part is issued per expand chunk"
        if expand:
            hn_parts = [_rms(h_ref[r, :], g2_ref[...]).astype(jnp.bfloat16) for r in parts]
        if finish:
            ple = _dot(p_ref[...].astype(jnp.bfloat16), wproj_ref[...])
        for c in range(n_chunks):
            if expand:
                expand_chunk(c, hn_parts)
            if finish:
                r = parts[c]
                z = _dot(acc_ref[r, :].astype(jnp.bfloat16), wgate_ref[...])
                out = acc_ref[r, :] + ple[r] / (1.0 + jnp.exp2(z * (-LOG2_E)))
                o_ref[r, :] = _rms(out, gf_ref[...]) if final_norm else out
        if expand:
            acc_ref[...] = h_ref[...] + _dot(jnp.concatenate(hidden, axis=1), w2_ref[...])

    t = pl.program_id(0)

    @pl.when(t == 0)
    def _():
        acc_ref[...] = jnp.zeros(acc_ref.shape, acc_ref.dtype)

    @pl.when(t < n_tiles)
    def _():
        step(True, True)

    @pl.when(t == n_tiles)
    def _():
        step(False, True)


def _resident(shape):
    return pl.BlockSpec(shape, lambda *_: (0,) * len(shape), pipeline_mode=pl.Buffered(1))


def _cast_slice_spec(arr, axis, n_tiles):
    size = arr.shape[axis]
    min_thick = BF16_SUBLANES if axis == 0 else LANES
    thick = max(size // n_tiles, min_thick)
    assert size % thick == 0
    last = size // thick - 1
    block = tuple(thick if a == axis else n for a, n in enumerate(arr.shape))
    return pl.BlockSpec(block, lambda t: tuple(jnp.minimum(t, last) if a == axis else 0
                                               for a in range(arr.ndim)))


def _mixer(x, g1, w_in, vgain, wsp, bsp, sinks, rel, w_out, cast):
    batch, seq, d = x.shape
    ts = SEQ_TILE
    n_blk = ts // BLOCK
    assert n_blk == 4, "stage-1 pieces are paired one-to-one with attention blocks"
    assert len(cast) == N_CAST
    n_tiles = batch * seq // ts
    x2 = x.reshape(batch * seq, d)
    smem = pl.BlockSpec(memory_space=pltpu.SMEM)
    hbm = pl.BlockSpec(memory_space=pl.ANY)
    cast_specs = [_cast_slice_spec(w, axis, n_tiles) for w, axis in cast]
    stage_slot = [
        pltpu.VMEM((ts, D_GMLP), jnp.float32),
        pltpu.VMEM((ts, D_GMLP), jnp.bfloat16),
        pltpu.VMEM((ts, D_ATTN), jnp.bfloat16),
        pltpu.VMEM((N_VARIANTS, BLOCK + ts, LANES), jnp.bfloat16),
        pltpu.VMEM((N_VARIANTS, BLOCK + ts, LANES), jnp.bfloat16),
        pltpu.VMEM((ts, d), jnp.float32),
    ]
    out, *cast_out = pl.pallas_call(
        functools.partial(_mixer_kernel, tiles_per_seq=seq // ts, n_tiles=n_tiles),
        grid=(n_tiles + 1,),
        in_specs=[
            pl.BlockSpec((ts, d), lambda t: (jnp.minimum(t, n_tiles - 1), 0)),
            _resident(g1.shape), hbm, _resident(vgain.shape),
            _resident(wsp.shape), _resident(bsp.shape), smem, _resident(rel.shape),
            hbm,
        ] + cast_specs,
        out_specs=[pl.BlockSpec((ts, d), lambda t: (jnp.maximum(t - 1, 0), 0))] + cast_specs,
        out_shape=[jax.ShapeDtypeStruct(x2.shape, x2.dtype)]
        + [jax.ShapeDtypeStruct(w.shape, jnp.bfloat16) for w, _ in cast],
        scratch_shapes=stage_slot + stage_slot + [
            pltpu.VMEM((ts, D_MODEL), jnp.bfloat16),
            pltpu.VMEM((2, 4, 2, BLOCK, BLOCK), jnp.float32),
            pltpu.VMEM((n_blk * N_STACKS, 2 * BLOCK, BLOCK), jnp.float32),
            pltpu.VMEM((n_blk * N_STACKS, 2 * BLOCK, 2 * BLOCK), jnp.bfloat16),
            pltpu.VMEM(w_in.shape, jnp.bfloat16),
            pltpu.VMEM(w_out.shape, jnp.bfloat16),
            pltpu.VMEM((2, WEIGHT_STAGE_ROWS, w_in.shape[1]), jnp.float32),
            pltpu.VMEM(w_out.shape, jnp.float32),
            pltpu.VMEM((CHUNK, LANES), jnp.float32),
            pltpu.SemaphoreType.DMA((3,)),
        ],
        compiler_params=pltpu.CompilerParams(
            dimension_semantics=("arbitrary",),
            vmem_limit_bytes=VMEM_LIMIT_BYTES),
        name="mixer",
    )(x2, g1, w_in, vgain, wsp, bsp, sinks, rel, w_out, *[w for w, _ in cast])
    return out.reshape(batch, seq, d), cast_out


def _ffn(h, p, g2, w1, w2, wproj, wgate, gf, *, final_norm):
    rows, d = h.shape
    tm = ROW_TILE
    n_tiles = rows // tm
    return pl.pallas_call(
        functools.partial(_ffn_kernel, final_norm=final_norm, n_tiles=n_tiles),
        grid=(n_tiles + 1,),
        in_specs=[
            pl.BlockSpec((tm, d), lambda t: (jnp.minimum(t, n_tiles - 1), 0)),
            pl.BlockSpec((tm, PLE_DIM), lambda t: (jnp.maximum(t - 1, 0), 0)),
            _resident(g2.shape), _resident(w1.shape), _resident(w2.shape),
            _resident(wproj.shape), _resident(wgate.shape), _resident(gf.shape),
        ],
        out_specs=pl.BlockSpec((tm, d), lambda t: (jnp.maximum(t - 1, 0), 0)),
        out_shape=jax.ShapeDtypeStruct(h.shape, h.dtype),
        scratch_shapes=[pltpu.VMEM((tm, d), jnp.float32)],
        compiler_params=pltpu.CompilerParams(
            dimension_semantics=("arbitrary",),
            vmem_limit_bytes=VMEM_LIMIT_BYTES),
        name="ffn",
    )(h, p, g2, w1, w2, wproj, wgate, gf)


def kernel(x, p, norm1_gain, w_in, gmlp_v_gain, w_spatial, b_spatial, attn_sinks, rel_bias_table, w_out, norm2_gain, w_ff1, w_ff2, w_ple_proj, w_ple_gate, final_gain):
    batch, seq, d = x.shape
    depth = w_in.shape[0]
    bf16 = jnp.bfloat16
    h = x
    for i in range(depth):
        h, (w1, w2, wproj, wgate) = _mixer(
            h, norm1_gain[i][None], w_in[i], gmlp_v_gain[i][None],
            w_spatial[i], b_spatial[i], attn_sinks[i], rel_bias_table.T, w_out[i],
            cast=[(w_ff1[i], 1), (w_ff2[i], 0), (w_ple_proj[i], 0), (w_ple_gate[i], 0)])
        h = _ffn(h.reshape(batch * seq, d), p[i].reshape(batch * seq, PLE_DIM),
                 norm2_gain[i][None], w1, w2, wproj, wgate,
                 final_gain[None], final_norm=(i == depth - 1)).reshape(batch, seq, d)
    return h
```

```python
import functools
import math

import jax
import jax.numpy as jnp
import numpy as np
from jax import lax
from jax.experimental import pallas as pl
from jax.experimental.pallas import tpu as pltpu

D_MODEL = 1024
PLE_DIM = 256
D_GMLP = 512
GMLP_GROUPS = 4
GROUP_DIM = D_GMLP // GMLP_GROUPS
CHUNK = 128
D_ATTN = 512
HEAD_DIM = 64
N_Q_HEADS = D_ATTN // HEAD_DIM
N_KV_HEADS = 2
Q_PER_KV = N_Q_HEADS // N_KV_HEADS
WINDOW = 128
BLOCK = WINDOW
REL_BUCKETS = 32
REL_MAX_DIST = 128
D_FF = 4 * D_MODEL
KV_W = N_KV_HEADS * HEAD_DIM
EPS = 1e-6
NEG_INF = -1e30
LOG2_E = math.log2(math.e)

LANES = 128
BF16_SUBLANES = 16
N_VARIANTS = 2 * N_KV_HEADS
N_STACKS = N_Q_HEADS // 2
SEQ_TILE = 512
ROW_TILE = 1024
FF_CHUNK = 1024
FFN_ROW_PART = 256
WEIGHT_STAGE_ROWS = 256
VMEM_LIMIT_BYTES = 56 * 1024 * 1024


def _bucket_thresholds():
    max_exact = REL_BUCKETS // 2
    n = np.arange(WINDOW)
    nf = np.maximum(n, 1).astype(np.float32)
    large = max_exact + (np.log(nf / max_exact) / np.float32(math.log(REL_MAX_DIST / max_exact))
                         * (REL_BUCKETS - max_exact)).astype(np.int32)
    bucket = np.where(n < max_exact, n, np.minimum(large, REL_BUCKETS - 1))
    assert (np.diff(bucket) >= 0).all()
    return [(b, int(np.argmax(bucket >= b))) for b in range(1, REL_BUCKETS) if (bucket >= b).any()]


_BUCKET_THRESHOLDS = _bucket_thresholds()


def _rms(x, gain):
    return x * lax.rsqrt(jnp.mean(x * x, axis=-1, keepdims=True) + EPS) * gain


def _gelu_tanh(x):
    c = float(np.float32(np.sqrt(2 / np.pi)))
    poly = x * x * (-2.0 * c * 0.044715 * LOG2_E) + (-2.0 * c * LOG2_E)
    return x / (1.0 + jnp.exp2(x * poly))


def _dot(a, b):
    return jnp.dot(a, b, preferred_element_type=jnp.float32)


def _dot_nt(a, b):
    return lax.dot_general(a, b, (((1,), (1,)), ((), ())), preferred_element_type=jnp.float32)


def _build_bias(rel_ref, bias_ref):
    a = lax.broadcasted_iota(jnp.int32, (BLOCK, BLOCK), 0)
    j = lax.broadcasted_iota(jnp.int32, (BLOCK, BLOCK), 1)
    from_prev = j > a
    n = jnp.where(from_prev, BLOCK + a - j, a - j)

    def entry(b, h):
        return jnp.broadcast_to(rel_ref[h:h + 1, b:b + 1], (BLOCK, BLOCK))

    for h in range(N_Q_HEADS):
        val = entry(0, h)
        for b, thr in _BUCKET_THRESHOLDS:
            val = jnp.where(n >= thr, entry(b, h), val)
        stack = 2 * (h // Q_PER_KV) + h % 2
        half = (h % Q_PER_KV) // 2
        val = val * LOG2_E
        bias_ref[0, stack, half] = val
        bias_ref[1, stack, half] = jnp.where(from_prev, NEG_INF, val)


def _build_spatial_bias(bsp_ref, bm_ref):
    row = lax.broadcasted_iota(jnp.int32, (CHUNK, CHUNK), 0)
    padded = jnp.zeros((CHUNK, CHUNK), jnp.float32)
    for g in range(GMLP_GROUPS):
        padded = jnp.where(row == g, jnp.broadcast_to(bsp_ref[g:g + 1, :], (CHUNK, CHUNK)), padded)
    bm_ref[...] = padded.T


def _load_as_bf16(src_hbm, dst_ref, stage_ref, sem_ref, sem_base, between=None):
    n_chunks = src_hbm.shape[0] // stage_ref.shape[1]
    rows = stage_ref.shape[1]

    def copy(k):
        return pltpu.make_async_copy(src_hbm.at[pl.ds(k * rows, rows), :],
                                     stage_ref.at[k % 2], sem_ref.at[sem_base + k % 2])

    copy(0).start()
    if between is not None:
        between()
    for k in range(n_chunks):
        if k + 1 < n_chunks:
            copy(k + 1).start()
        copy(k).wait()
        dst_ref[k * rows:(k + 1) * rows, :] = stage_ref[k % 2].astype(dst_ref.dtype)


_VARIANT = ((0, 1), (2, 3))


def _project_pieces(x_ref, g1_ref, w_in_ref, vgain_ref, dst):
    u_ref, vn_ref, q_ref, kbuf, vbuf = dst
    ts = x_ref.shape[0]
    o = 2 * D_GMLP
    state = {}

    def norm():
        state["hn"] = _rms(x_ref[...], g1_ref[...]).astype(jnp.bfloat16)

    def gate_u():
        u_ref[...] = _gelu_tanh(_dot(state["hn"], w_in_ref[:, 0:D_GMLP]))

    def gate_v():
        vg = _gelu_tanh(_dot(state["hn"], w_in_ref[:, D_GMLP:2 * D_GMLP]))
        for g in range(GMLP_GROUPS):
            cols = slice(g * GROUP_DIM, (g + 1) * GROUP_DIM)
            vn_ref[:, cols] = _rms(vg[:, cols], vgain_ref[:, cols]).astype(vn_ref.dtype)

    def query():
        q = _dot(state["hn"], w_in_ref[:, o:o + D_ATTN]) * (HEAD_DIM ** -0.5 * LOG2_E)
        q_ref[...] = q.astype(q_ref.dtype)

    def key_value():
        kv = _dot(state["hn"], w_in_ref[:, o + D_ATTN:o + D_ATTN + 2 * KV_W])
        lo = lax.broadcasted_iota(jnp.int32, (ts, LANES), 1) < HEAD_DIM
        for buf, t in ((kbuf, kv[:, 0:KV_W]), (vbuf, kv[:, KV_W:2 * KV_W])):
            tr = pltpu.roll(t, HEAD_DIM, axis=1)
            buf[0, BLOCK:BLOCK + ts, :] = jnp.where(lo, t, 0.0).astype(buf.dtype)
            buf[1, BLOCK:BLOCK + ts, :] = jnp.where(lo, 0.0, tr).astype(buf.dtype)
            buf[2, BLOCK:BLOCK + ts, :] = jnp.where(lo, tr, 0.0).astype(buf.dtype)
            buf[3, BLOCK:BLOCK + ts, :] = jnp.where(lo, 0.0, t).astype(buf.dtype)

    return [norm, gate_u, gate_v, query, key_value]


def _attend_pieces(i, src, seq_start, sink_ref, bias_ref, mix_ref, s_scr, p_scr):
    _, _, q_ref, kbuf, vbuf = src
    rows = slice(i * BLOCK, (i + 1) * BLOCK)
    keys = slice(i * BLOCK, (i + 2) * BLOCK)
    first = jnp.where(seq_start, 1, 0) if i == 0 else 0
    upper = lax.broadcasted_iota(jnp.int32, (2 * BLOCK, 1), 0) < BLOCK
    query = lax.broadcasted_iota(jnp.int32, (2 * BLOCK, BLOCK), 0) % BLOCK
    from_prev = lax.broadcasted_iota(jnp.int32, (2 * BLOCK, BLOCK), 1) > query
    stacks = [(g, parity) for g in range(N_KV_HEADS) for parity in range(2)]
    state = {}

    def scores():
        for g, parity in stacks:
            qs = jnp.concatenate([q_ref[rows, 2 * g * LANES:(2 * g + 1) * LANES],
                                  q_ref[rows, (2 * g + 1) * LANES:(2 * g + 2) * LANES]], axis=0)
            s = _dot_nt(qs, kbuf[_VARIANT[g][parity], keys, :])
            s = jnp.where(from_prev, s[:, 0:BLOCK], s[:, BLOCK:2 * BLOCK])
            s_scr[N_STACKS * i + 2 * g + parity] = (
                s + bias_ref[first, 2 * g + parity].reshape(2 * BLOCK, BLOCK))

    def softmax():
        zero = jnp.zeros((2 * BLOCK, BLOCK), jnp.bfloat16)
        for g, parity in stacks:
            head = Q_PER_KV * g + parity
            sink = jnp.where(upper, sink_ref[head], sink_ref[head + 2]) * LOG2_E
            s = s_scr[N_STACKS * i + 2 * g + parity]
            m = jnp.maximum(jnp.max(s, axis=-1, keepdims=True), sink)
            e = jnp.exp2(s - m)
            denom = jnp.sum(e, axis=-1, keepdims=True) + jnp.exp2(sink - m)
            p = e.astype(jnp.bfloat16)
            p_scr[N_STACKS * i + 2 * g + parity] = jnp.concatenate(
                [jnp.where(from_prev, p, zero), jnp.where(from_prev, zero, p)], axis=1)
            state["inv", g, parity] = 1.0 / denom

    def values():
        low = lax.broadcasted_iota(jnp.int32, (2 * BLOCK, LANES), 1) < HEAD_DIM
        for g in range(N_KV_HEADS):
            pair0 = D_GMLP + 2 * g * LANES
            acc = (_dot(p_scr[N_STACKS * i + 2 * g], vbuf[_VARIANT[g][0], keys, :])
                   + _dot(p_scr[N_STACKS * i + 2 * g + 1], vbuf[_VARIANT[g][1], keys, :]))
            acc = acc * jnp.where(low, state["inv", g, 0], state["inv", g, 1])
            mix_ref[rows, pair0:pair0 + LANES] = acc[0:BLOCK].astype(mix_ref.dtype)
            mix_ref[rows, pair0 + LANES:pair0 + 2 * LANES] = acc[BLOCK:].astype(mix_ref.dtype)

    return scores, softmax, values


def _spatial_gate(src, wsp_ref, bm_ref, mix_ref):
    u_ref, vn_ref = src[0:2]
    t_idx = lax.broadcasted_iota(jnp.int32, (CHUNK, CHUNK), 0)
    s_idx = lax.broadcasted_iota(jnp.int32, (CHUNK, CHUNK), 1)
    causal = s_idx <= t_idx
    for g in range(GMLP_GROUPS):
        cols = slice(g * GROUP_DIM, (g + 1) * GROUP_DIM)
        w_g = jnp.where(causal, wsp_ref[g], 0.0).astype(jnp.bfloat16)
        b_g = bm_ref[:, g:g + 1]
        for c in range(u_ref.shape[0] // CHUNK):
            rows = slice(c * CHUNK, (c + 1) * CHUNK)
            sv = _dot(w_g, vn_ref[rows, cols]) + b_g
            mix_ref[rows, cols] = (u_ref[rows, cols] * sv).astype(mix_ref.dtype)


N_CAST = 4
SCORES_AHEAD = 2


def _mixer_kernel(x_ref, xprev_ref, g1_ref, w_in_hbm, vgain_ref, wsp_ref, bsp_ref, sink_ref,
                  rel_ref, w_out_hbm, *rest, tiles_per_seq, n_tiles):
    cast_src, o_ref = rest[0:N_CAST], rest[N_CAST]
    cast_dst, scratch = rest[N_CAST + 1:2 * N_CAST + 1], rest[2 * N_CAST + 1:]
    slots = (scratch[0:5], scratch[5:10])
    mix_ref, bias_ref, s_scr, p_scr = scratch[10:14]
    w_in_ref, w_out_ref, stage_in, stage_out, bm_ref, dma_sem = scratch[14:20]
    ts = x_ref.shape[0]
    t = pl.program_id(0)

    cur_starts = t % tiles_per_seq == 0
    prev_started = (t + tiles_per_seq - 1) % tiles_per_seq == 0
    n_blk = ts // BLOCK

    def step(dst, src):
        project = _project_pieces(x_ref, g1_ref, w_in_ref, vgain_ref, dst) if dst else []
        attend = [_attend_pieces(i, src, prev_started, sink_ref, bias_ref, mix_ref, s_scr,
                                 p_scr) for i in range(n_blk)] if src else []
        if src:
            for i in range(SCORES_AHEAD):
                attend[i][0]()
            _spatial_gate(src, wsp_ref, bm_ref, mix_ref)
        if dst:
            project.pop(0)()
        for i in range(n_blk):
            if src:
                if i + SCORES_AHEAD < n_blk:
                    attend[i + SCORES_AHEAD][0]()
                attend[i][1]()
            if dst:
                project.pop(0)()
            if src:
                attend[i][2]()
        assert not project
        if src:
            o_ref[...] = xprev_ref[...] + _dot(mix_ref[...], w_out_ref[...])
        for s, d in zip(cast_src, cast_dst):
            d[...] = s[...].astype(d.dtype)
        if dst:
            zeros = jnp.zeros((N_VARIANTS, BLOCK, LANES), dst[3].dtype)
            for k in (3, 4):
                dst[k][:, 0:BLOCK, :] = (
                    jnp.where(cur_starts, zeros, src[k][:, ts:ts + BLOCK, :]) if src else zeros)

    @pl.when(t == 0)
    def _():
        def build_tables():
            _build_bias(rel_ref, bias_ref)
            _build_spatial_bias(bsp_ref, bm_ref)

        fetch_w_out = pltpu.make_async_copy(w_out_hbm, stage_out, dma_sem.at[2])
        fetch_w_out.start()
        _load_as_bf16(w_in_hbm, w_in_ref, stage_in, dma_sem, 0, between=build_tables)
        step(slots[0], None)
        fetch_w_out.wait()
        w_out_ref[...] = stage_out[...].astype(w_out_ref.dtype)

    @pl.when(t == n_tiles)
    def _():
        step(None, slots[(n_tiles - 1) % 2])

    @pl.when((t > 0) & (t < n_tiles) & (t % 2 == 0))
    def _():
        step(slots[0], slots[1])

    @pl.when((t < n_tiles) & (t % 2 == 1))
    def _():
        step(slots[1], slots[0])


def _ffn_kernel(h_ref, p_ref, w1_ref, w2_ref, wproj_ref, wgate_ref, g2_ref, gf_ref, o_ref,
                acc_ref, *, final_norm, n_tiles):
    tm = h_ref.shape[0]
    parts = [slice(r, r + FFN_ROW_PART) for r in range(0, tm, FFN_ROW_PART)]

    def step(expand, finish):
        hidden = []

        def expand_chunk(c, hn_parts):
            cols = slice(c * FF_CHUNK, (c + 1) * FF_CHUNK)
            if c == 0:
                t = jnp.concatenate([_dot(part, w1_ref[:, cols]) for part in hn_parts], axis=0)
            else:
                t = _dot(jnp.concatenate(hn_parts, axis=0), w1_ref[:, cols])
            t = jnp.maximum(t, 0.0)
            hidden.append((t * t).astype(jnp.bfloat16))

        n_chunks = D_FF // FF_CHUNK
        assert len(parts) == n_chunks, "one finishing row part is issued per expand chunk"
        if expand:
            hn_parts = [_rms(h_ref[r, :], g2_ref[...]).astype(jnp.bfloat16) for r in parts]
        if finish:
            ple = _dot(p_ref[...].astype(jnp.bfloat16), wproj_ref[...])
        for c in range(n_chunks):
            if expand:
                expand_chunk(c, hn_parts)
            if finish:
                r = parts[c]
                z = _dot(acc_ref[r, :].astype(jnp.bfloat16), wgate_ref[...])
                out = acc_ref[r, :] + ple[r] / (1.0 + jnp.exp2(z * (-LOG2_E)))
                o_ref[r, :] = _rms(out, gf_ref[...]) if final_norm else out
        if expand:
            acc_ref[...] = h_ref[...] + _dot(jnp.concatenate(hidden, axis=1), w2_ref[...])

    t = pl.program_id(0)

    @pl.when(t == 0)
    def _():
        acc_ref[...] = jnp.zeros(acc_ref.shape, acc_ref.dtype)

    @pl.when(t < n_tiles)
    def _():
        step(True, True)

    @pl.when(t == n_tiles)
    def _():
        step(False, True)


def _resident(shape):
    return pl.BlockSpec(shape, lambda *_: (0,) * len(shape), pipeline_mode=pl.Buffered(1))


def _cast_slice_spec(arr, axis, n_tiles):
    size = arr.shape[axis]
    min_thick = BF16_SUBLANES if axis == 0 else LANES
    thick = max(size // n_tiles, min_thick)
    assert size % thick == 0
    last = size // thick - 1
    block = tuple(thick if a == axis else n for a, n in enumerate(arr.shape))
    return pl.BlockSpec(block, lambda t: tuple(jnp.minimum(t, last) if a == axis else 0
                                               for a in range(arr.ndim)))


def _mixer(x, g1, w_in, vgain, wsp, bsp, sinks, rel, w_out, cast):
    batch, seq, d = x.shape
    ts = SEQ_TILE
    n_blk = ts // BLOCK
    assert n_blk == 4, "stage-1 pieces are paired one-to-one with attention blocks"
    assert len(cast) == N_CAST
    n_tiles = batch * seq // ts
    x2 = x.reshape(batch * seq, d)
    smem = pl.BlockSpec(memory_space=pltpu.SMEM)
    hbm = pl.BlockSpec(memory_space=pl.ANY)
    cast_specs = [_cast_slice_spec(w, axis, n_tiles) for w, axis in cast]
    stage_slot = [
        pltpu.VMEM((ts, D_GMLP), jnp.float32),
        pltpu.VMEM((ts, D_GMLP), jnp.bfloat16),
        pltpu.VMEM((ts, D_ATTN), jnp.bfloat16),
        pltpu.VMEM((N_VARIANTS, BLOCK + ts, LANES), jnp.bfloat16),
        pltpu.VMEM((N_VARIANTS, BLOCK + ts, LANES), jnp.bfloat16),
    ]
    out, *cast_out = pl.pallas_call(
        functools.partial(_mixer_kernel, tiles_per_seq=seq // ts, n_tiles=n_tiles),
        grid=(n_tiles + 1,),
        in_specs=[
            pl.BlockSpec((ts, d), lambda t: (jnp.minimum(t, n_tiles - 1), 0)),
            pl.BlockSpec((ts, d), lambda t: (jnp.maximum(t - 1, 0), 0)),
            _resident(g1.shape), hbm, _resident(vgain.shape),
            _resident(wsp.shape), _resident(bsp.shape), smem, _resident(rel.shape),
            hbm,
        ] + cast_specs,
        out_specs=[pl.BlockSpec((ts, d), lambda t: (jnp.maximum(t - 1, 0), 0))] + cast_specs,
        out_shape=[jax.ShapeDtypeStruct(x2.shape, x2.dtype)]
        + [jax.ShapeDtypeStruct(w.shape, jnp.bfloat16) for w, _ in cast],
        scratch_shapes=stage_slot + stage_slot + [
            pltpu.VMEM((ts, D_MODEL), jnp.bfloat16),
            pltpu.VMEM((2, N_STACKS, 2, BLOCK, BLOCK), jnp.float32),
            pltpu.VMEM((n_blk * N_STACKS, 2 * BLOCK, BLOCK), jnp.float32),
            pltpu.VMEM((n_blk * N_STACKS, 2 * BLOCK, 2 * BLOCK), jnp.bfloat16),
            pltpu.VMEM(w_in.shape, jnp.bfloat16),
            pltpu.VMEM(w_out.shape, jnp.bfloat16),
            pltpu.VMEM((2, WEIGHT_STAGE_ROWS, w_in.shape[1]), jnp.float32),
            pltpu.VMEM(w_out.shape, jnp.float32),
            pltpu.VMEM((CHUNK, LANES), jnp.float32),
            pltpu.SemaphoreType.DMA((3,)),
        ],
        compiler_params=pltpu.CompilerParams(
            dimension_semantics=("arbitrary",),
            vmem_limit_bytes=VMEM_LIMIT_BYTES),
        name="mixer",
    )(x2, x2, g1, w_in, vgain, wsp, bsp, sinks, rel, w_out, *[w for w, _ in cast])
    return out.reshape(batch, seq, d), cast_out


def _ffn(h, p, g2, w1, w2, wproj, wgate, gf, *, final_norm):
    rows, d = h.shape
    tm = ROW_TILE
    n_tiles = rows // tm
    return pl.pallas_call(
        functools.partial(_ffn_kernel, final_norm=final_norm, n_tiles=n_tiles),
        grid=(n_tiles + 1,),
        in_specs=[
            pl.BlockSpec((tm, d), lambda t: (jnp.minimum(t, n_tiles - 1), 0)),
            pl.BlockSpec((tm, PLE_DIM), lambda t: (jnp.maximum(t - 1, 0), 0)),
            _resident(w1.shape), _resident(w2.shape), _resident(wproj.shape),
            _resident(wgate.shape), _resident(g2.shape), _resident(gf.shape),
        ],
        out_specs=pl.BlockSpec((tm, d), lambda t: (jnp.maximum(t - 1, 0), 0)),
        out_shape=jax.ShapeDtypeStruct(h.shape, h.dtype),
        scratch_shapes=[pltpu.VMEM((tm, d), jnp.float32)],
        compiler_params=pltpu.CompilerParams(
            dimension_semantics=("arbitrary",),
            vmem_limit_bytes=VMEM_LIMIT_BYTES),
        name="ffn",
    )(h, p, w1, w2, wproj, wgate, g2, gf)


def kernel(x, p, norm1_gain, w_in, gmlp_v_gain, w_spatial, b_spatial, attn_sinks, rel_bias_table, w_out, norm2_gain, w_ff1, w_ff2, w_ple_proj, w_ple_gate, final_gain):
    batch, seq, d = x.shape
    depth = w_in.shape[0]
    h = x
    for i in range(depth):
        h, (w1, w2, wproj, wgate) = _mixer(
            h, norm1_gain[i][None], w_in[i], gmlp_v_gain[i][None],
            w_spatial[i], b_spatial[i], attn_sinks[i], rel_bias_table.T, w_out[i],
            cast=[(w_ff1[i], 1), (w_ff2[i], 0), (w_ple_proj[i], 0), (w_ple_gate[i], 0)])
        h = _ffn(h.reshape(batch * seq, d), p[i].reshape(batch * seq, PLE_DIM),
                 norm2_gain[i][None], w1, w2, wproj, wgate,
                 final_gain[None], final_norm=(i == depth - 1)).reshape(batch, seq, d)
    return h
```

```python
import functools
import math

import jax
import jax.numpy as jnp
import numpy as np
from jax import lax
from jax.experimental import pallas as pl
from jax.experimental.pallas import tpu as pltpu

D_MODEL = 1024
PLE_DIM = 256
D_GMLP = 512
GMLP_GROUPS = 4
GROUP_DIM = D_GMLP // GMLP_GROUPS
CHUNK = 128
D_ATTN = 512
HEAD_DIM = 64
N_Q_HEADS = D_ATTN // HEAD_DIM
N_KV_HEADS = 2
Q_PER_KV = N_Q_HEADS // N_KV_HEADS
WINDOW = 128
BLOCK = WINDOW
REL_BUCKETS = 32
REL_MAX_DIST = 128
D_FF = 4 * D_MODEL
KV_W = N_KV_HEADS * HEAD_DIM
EPS = 1e-6
NEG_INF = -1e30
LOG2_E = math.log2(math.e)

LANES = 128
BF16_SUBLANES = 16
N_VARIANTS = 2 * N_KV_HEADS
N_STACKS = N_Q_HEADS // 2
SEQ_TILE = 512
ROW_TILE = 1024
FF_CHUNK = 1024
FFN_ROW_PART = 256
WEIGHT_STAGE_ROWS = 256
VMEM_LIMIT_BYTES = 56 * 1024 * 1024


def _bucket_thresholds():
    max_exact = REL_BUCKETS // 2
    n = np.arange(WINDOW)
    nf = np.maximum(n, 1).astype(np.float32)
    large = max_exact + (np.log(nf / max_exact) / np.float32(math.log(REL_MAX_DIST / max_exact))
                         * (REL_BUCKETS - max_exact)).astype(np.int32)
    bucket = np.where(n < max_exact, n, np.minimum(large, REL_BUCKETS - 1))
    assert (np.diff(bucket) >= 0).all()
    return [(b, int(np.argmax(bucket >= b))) for b in range(1, REL_BUCKETS) if (bucket >= b).any()]


_BUCKET_THRESHOLDS = _bucket_thresholds()


def _rms(x, gain):
    return x * lax.rsqrt(jnp.mean(x * x, axis=-1, keepdims=True) + EPS) * gain


def _gelu_tanh(x):
    c = float(np.float32(np.sqrt(2 / np.pi)))
    poly = x * x * (-2.0 * c * 0.044715 * LOG2_E) + (-2.0 * c * LOG2_E)
    return x / (1.0 + jnp.exp2(x * poly))


def _dot(a, b):
    return jnp.dot(a, b, preferred_element_type=jnp.float32)


def _dot_nt(a, b):
    return lax.dot_general(a, b, (((1,), (1,)), ((), ())), preferred_element_type=jnp.float32)


def _build_bias(rel_ref, bias_ref):
    a = lax.broadcasted_iota(jnp.int32, (BLOCK, BLOCK), 0)
    j = lax.broadcasted_iota(jnp.int32, (BLOCK, BLOCK), 1)
    from_prev = j > a
    n = jnp.where(from_prev, BLOCK + a - j, a - j)

    def entry(b, h):
        return jnp.broadcast_to(rel_ref[h:h + 1, b:b + 1], (BLOCK, BLOCK))

    for h in range(N_Q_HEADS):
        val = entry(0, h)
        for b, thr in _BUCKET_THRESHOLDS:
            val = jnp.where(n >= thr, entry(b, h), val)
        stack = 2 * (h // Q_PER_KV) + h % 2
        half = (h % Q_PER_KV) // 2
        val = val * LOG2_E
        bias_ref[0, stack, half] = val
        bias_ref[1, stack, half] = jnp.where(from_prev, NEG_INF, val)


def _build_spatial_bias(bsp_ref, bm_ref):
    row = lax.broadcasted_iota(jnp.int32, (CHUNK, CHUNK), 0)
    padded = jnp.zeros((CHUNK, CHUNK), jnp.float32)
    for g in range(GMLP_GROUPS):
        padded = jnp.where(row == g, jnp.broadcast_to(bsp_ref[g:g + 1, :], (CHUNK, CHUNK)), padded)
    bm_ref[...] = padded.T


def _load_as_bf16(src_hbm, dst_ref, stage_ref, sem_ref, sem_base, between=None):
    n_chunks = src_hbm.shape[0] // stage_ref.shape[1]
    rows = stage_ref.shape[1]

    def copy(k):
        return pltpu.make_async_copy(src_hbm.at[pl.ds(k * rows, rows), :],
                                     stage_ref.at[k % 2], sem_ref.at[sem_base + k % 2])

    copy(0).start()
    if between is not None:
        between()
    for k in range(n_chunks):
        if k + 1 < n_chunks:
            copy(k + 1).start()
        copy(k).wait()
        dst_ref[k * rows:(k + 1) * rows, :] = stage_ref[k % 2].astype(dst_ref.dtype)


_VARIANT = ((0, 1), (2, 3))


def _project_pieces(x_ref, g1_ref, w_in_ref, vgain_ref, dst):
    u_ref, vn_ref, q_ref, kbuf, vbuf = dst
    ts = x_ref.shape[0]
    o = 2 * D_GMLP
    state = {}

    def norm():
        state["hn"] = _rms(x_ref[...], g1_ref[...]).astype(jnp.bfloat16)

    def gate_u():
        u_ref[...] = _gelu_tanh(_dot(state["hn"], w_in_ref[:, 0:D_GMLP]))

    def gate_v():
        vg = _gelu_tanh(_dot(state["hn"], w_in_ref[:, D_GMLP:2 * D_GMLP]))
        for g in range(GMLP_GROUPS):
            cols = slice(g * GROUP_DIM, (g + 1) * GROUP_DIM)
            vn_ref[:, cols] = _rms(vg[:, cols], vgain_ref[:, cols]).astype(vn_ref.dtype)

    def query():
        q = _dot(state["hn"], w_in_ref[:, o:o + D_ATTN]) * (HEAD_DIM ** -0.5 * LOG2_E)
        q_ref[...] = q.astype(q_ref.dtype)

    def key_value():
        kv = _dot(state["hn"], w_in_ref[:, o + D_ATTN:o + D_ATTN + 2 * KV_W])
        lo = lax.broadcasted_iota(jnp.int32, (ts, LANES), 1) < HEAD_DIM
        for buf, t in ((kbuf, kv[:, 0:KV_W]), (vbuf, kv[:, KV_W:2 * KV_W])):
            tr = pltpu.roll(t, HEAD_DIM, axis=1)
            buf[0, BLOCK:BLOCK + ts, :] = jnp.where(lo, t, 0.0).astype(buf.dtype)
            buf[1, BLOCK:BLOCK + ts, :] = jnp.where(lo, 0.0, tr).astype(buf.dtype)
            buf[2, BLOCK:BLOCK + ts, :] = jnp.where(lo, tr, 0.0).astype(buf.dtype)
            buf[3, BLOCK:BLOCK + ts, :] = jnp.where(lo, 0.0, t).astype(buf.dtype)

    return [norm, gate_u, gate_v, query, key_value]


def _attend_pieces(i, src, seq_start, sink_ref, bias_ref, mix_ref, s_scr, p_scr):
    _, _, q_ref, kbuf, vbuf = src
    rows = slice(i * BLOCK, (i + 1) * BLOCK)
    keys = slice(i * BLOCK, (i + 2) * BLOCK)
    first = jnp.where(seq_start, 1, 0) if i == 0 else 0
    upper = lax.broadcasted_iota(jnp.int32, (2 * BLOCK, 1), 0) < BLOCK
    query = lax.broadcasted_iota(jnp.int32, (2 * BLOCK, BLOCK), 0) % BLOCK
    from_prev = lax.broadcasted_iota(jnp.int32, (2 * BLOCK, BLOCK), 1) > query
    stacks = [(g, parity) for g in range(N_KV_HEADS) for parity in range(2)]
    state = {}

    def scores():
        for g, parity in stacks:
            qs = jnp.concatenate([q_ref[rows, 2 * g * LANES:(2 * g + 1) * LANES],
                                  q_ref[rows, (2 * g + 1) * LANES:(2 * g + 2) * LANES]], axis=0)
            s = _dot_nt(qs, kbuf[_VARIANT[g][parity], keys, :])
            s = jnp.where(from_prev, s[:, 0:BLOCK], s[:, BLOCK:2 * BLOCK])
            s_scr[N_STACKS * i + 2 * g + parity] = (
                s + bias_ref[first, 2 * g + parity].reshape(2 * BLOCK, BLOCK))

    def softmax():
        for g, parity in stacks:
            head = Q_PER_KV * g + parity
            sink = jnp.where(upper, sink_ref[head], sink_ref[head + 2]) * LOG2_E
            s = s_scr[N_STACKS * i + 2 * g + parity]
            m = jnp.maximum(jnp.max(s, axis=-1, keepdims=True), sink)
            e = jnp.exp2(s - m)
            denom = jnp.sum(e, axis=-1, keepdims=True) + jnp.exp2(sink - m)
            p_scr[N_STACKS * i + 2 * g + parity] = e.astype(jnp.bfloat16)
            state["inv", g, parity] = 1.0 / denom

    def unfolded(p):
        zero = jnp.zeros((2 * BLOCK, BLOCK), jnp.bfloat16)
        return jnp.concatenate(
            [jnp.where(from_prev, p, zero), jnp.where(from_prev, zero, p)], axis=1)

    def values():
        low = lax.broadcasted_iota(jnp.int32, (2 * BLOCK, LANES), 1) < HEAD_DIM
        for g in range(N_KV_HEADS):
            pair0 = D_GMLP + 2 * g * LANES
            acc = (_dot(unfolded(p_scr[N_STACKS * i + 2 * g]), vbuf[_VARIANT[g][0], keys, :])
                   + _dot(unfolded(p_scr[N_STACKS * i + 2 * g + 1]),
                          vbuf[_VARIANT[g][1], keys, :]))
            acc = acc * jnp.where(low, state["inv", g, 0], state["inv", g, 1])
            mix_ref[rows, pair0:pair0 + LANES] = acc[0:BLOCK].astype(mix_ref.dtype)
            mix_ref[rows, pair0 + LANES:pair0 + 2 * LANES] = acc[BLOCK:].astype(mix_ref.dtype)

    return scores, softmax, values


def _spatial_gate(src, wsp_ref, bm_ref, mix_ref):
    u_ref, vn_ref = src[0:2]
    t_idx = lax.broadcasted_iota(jnp.int32, (CHUNK, CHUNK), 0)
    s_idx = lax.broadcasted_iota(jnp.int32, (CHUNK, CHUNK), 1)
    causal = s_idx <= t_idx
    for g in range(GMLP_GROUPS):
        cols = slice(g * GROUP_DIM, (g + 1) * GROUP_DIM)
        w_g = jnp.where(causal, wsp_ref[g], 0.0).astype(jnp.bfloat16)
        b_g = bm_ref[:, g:g + 1]
        for c in range(u_ref.shape[0] // CHUNK):
            rows = slice(c * CHUNK, (c + 1) * CHUNK)
            sv = _dot(w_g, vn_ref[rows, cols]) + b_g
            mix_ref[rows, cols] = (u_ref[rows, cols] * sv).astype(mix_ref.dtype)


N_CAST = 4
SCORES_AHEAD = 2


def _mixer_kernel(x_ref, xprev_ref, g1_ref, w_in_hbm, vgain_ref, wsp_ref, bsp_ref, sink_ref,
                  rel_ref, w_out_hbm, *rest, tiles_per_seq, n_tiles):
    cast_src, o_ref = rest[0:N_CAST], rest[N_CAST]
    cast_dst, scratch = rest[N_CAST + 1:2 * N_CAST + 1], rest[2 * N_CAST + 1:]
    slots = (scratch[0:5], scratch[5:10])
    mix_ref, bias_ref, s_scr, p_scr = scratch[10:14]
    w_in_ref, w_out_ref, stage_in, stage_out, bm_ref, dma_sem = scratch[14:20]
    ts = x_ref.shape[0]
    t = pl.program_id(0)

    cur_starts = t % tiles_per_seq == 0
    prev_started = (t + tiles_per_seq - 1) % tiles_per_seq == 0
    n_blk = ts // BLOCK

    def step(dst, src):
        project = _project_pieces(x_ref, g1_ref, w_in_ref, vgain_ref, dst) if dst else []
        attend = [_attend_pieces(i, src, prev_started, sink_ref, bias_ref, mix_ref, s_scr,
                                 p_scr) for i in range(n_blk)] if src else []
        if src:
            for i in range(SCORES_AHEAD):
                attend[i][0]()
            _spatial_gate(src, wsp_ref, bm_ref, mix_ref)
        if dst:
            project.pop(0)()
        for i in range(n_blk):
            if src:
                if i + SCORES_AHEAD < n_blk:
                    attend[i + SCORES_AHEAD][0]()
                attend[i][1]()
            if dst:
                project.pop(0)()
            if src:
                attend[i][2]()
        assert not project
        if src:
            o_ref[...] = xprev_ref[...] + _dot(mix_ref[...], w_out_ref[...])
        for s, d in zip(cast_src, cast_dst):
            d[...] = s[...].astype(d.dtype)
        if dst:
            zeros = jnp.zeros((N_VARIANTS, BLOCK, LANES), dst[3].dtype)
            for k in (3, 4):
                dst[k][:, 0:BLOCK, :] = (
                    jnp.where(cur_starts, zeros, src[k][:, ts:ts + BLOCK, :]) if src else zeros)

    @pl.when(t == 0)
    def _():
        def build_tables():
            _build_bias(rel_ref, bias_ref)
            _build_spatial_bias(bsp_ref, bm_ref)

        fetch_w_out = pltpu.make_async_copy(w_out_hbm, stage_out, dma_sem.at[2])
        fetch_w_out.start()
        _load_as_bf16(w_in_hbm, w_in_ref, stage_in, dma_sem, 0, between=build_tables)
        step(slots[0], None)
        fetch_w_out.wait()
        w_out_ref[...] = stage_out[...].astype(w_out_ref.dtype)

    @pl.when(t == n_tiles)
    def _():
        step(None, slots[(n_tiles - 1) % 2])

    @pl.when((t > 0) & (t < n_tiles) & (t % 2 == 0))
    def _():
        step(slots[0], slots[1])

    @pl.when((t < n_tiles) & (t % 2 == 1))
    def _():
        step(slots[1], slots[0])


def _ffn_kernel(h_ref, p_ref, g2_ref, w1_ref, w2_ref, wproj_ref, wgate_ref, gf_ref, o_ref,
                acc_ref, *, final_norm, n_tiles):
    tm = h_ref.shape[0]
    parts = [slice(r, r + FFN_ROW_PART) for r in range(0, tm, FFN_ROW_PART)]

    def step(expand, finish):
        hidden = []

        def expand_chunk(c, hn_parts):
            cols = slice(c * FF_CHUNK, (c + 1) * FF_CHUNK)
            if c == 0:
                t = jnp.concatenate([_dot(part, w1_ref[:, cols]) for part in hn_parts], axis=0)
            else:
                t = _dot(jnp.concatenate(hn_parts, axis=0), w1_ref[:, cols])
            t = jnp.maximum(t, 0.0)
            hidden.append((t * t).astype(jnp.bfloat16))

        n_chunks = D_FF // FF_CHUNK
        assert len(parts) == n_chunks, "one finishing row part is issued per expand chunk"
        if expand:
            hn_parts = [_rms(h_ref[r, :], g2_ref[...]).astype(jnp.bfloat16) for r in parts]
        if finish:
            ple = _dot(p_ref[...].astype(jnp.bfloat16), wproj_ref[...])
        for c in range(n_chunks):
            if expand:
                expand_chunk(c, hn_parts)
            if finish:
                r = parts[c]
                z = _dot(acc_ref[r, :].astype(jnp.bfloat16), wgate_ref[...])
                out = acc_ref[r, :] + ple[r] / (1.0 + jnp.exp2(z * (-LOG2_E)))
                o_ref[r, :] = _rms(out, gf_ref[...]) if final_norm else out
        if expand:
            acc_ref[...] = h_ref[...] + _dot(jnp.concatenate(hidden, axis=1), w2_ref[...])

    t = pl.program_id(0)

    @pl.when(t == 0)
    def _():
        acc_ref[...] = jnp.zeros(acc_ref.shape, acc_ref.dtype)

    @pl.when(t < n_tiles)
    def _():
        step(True, True)

    @pl.when(t == n_tiles)
    def _():
        step(False, True)


def _resident(shape):
    return pl.BlockSpec(shape, lambda *_: (0,) * len(shape), pipeline_mode=pl.Buffered(1))


def _cast_slice_spec(arr, axis, n_tiles):
    size = arr.shape[axis]
    min_thick = BF16_SUBLANES if axis == 0 else LANES
    thick = max(size // n_tiles, min_thick)
    assert size % thick == 0
    last = size // thick - 1
    block = tuple(thick if a == axis else n for a, n in enumerate(arr.shape))
    return pl.BlockSpec(block, lambda t: tuple(jnp.minimum(t, last) if a == axis else 0
                                               for a in range(arr.ndim)))


def _mixer(x, g1, w_in, vgain, wsp, bsp, sinks, rel, w_out, cast):
    batch, seq, d = x.shape
    ts = SEQ_TILE
    n_blk = ts // BLOCK
    assert n_blk == 4, "stage-1 pieces are paired one-to-one with attention blocks"
    assert len(cast) == N_CAST
    n_tiles = batch * seq // ts
    x2 = x.reshape(batch * seq, d)
    smem = pl.BlockSpec(memory_space=pltpu.SMEM)
    hbm = pl.BlockSpec(memory_space=pl.ANY)
    cast_specs = [_cast_slice_spec(w, axis, n_tiles) for w, axis in cast]
    stage_slot = [
        pltpu.VMEM((ts, D_GMLP), jnp.float32),
        pltpu.VMEM((ts, D_GMLP), jnp.bfloat16),
        pltpu.VMEM((ts, D_ATTN), jnp.bfloat16),
        pltpu.VMEM((N_VARIANTS, BLOCK + ts, LANES), jnp.bfloat16),
        pltpu.VMEM((N_VARIANTS, BLOCK + ts, LANES), jnp.bfloat16),
    ]
    out, *cast_out = pl.pallas_call(
        functools.partial(_mixer_kernel, tiles_per_seq=seq // ts, n_tiles=n_tiles),
        grid=(n_tiles + 1,),
        in_specs=[
            pl.BlockSpec((ts, d), lambda t: (jnp.minimum(t, n_tiles - 1), 0)),
            pl.BlockSpec((ts, d), lambda t: (jnp.maximum(t - 1, 0), 0)),
            _resident(g1.shape), hbm, _resident(vgain.shape),
            _resident(wsp.shape), _resident(bsp.shape), smem, _resident(rel.shape),
            hbm,
        ] + cast_specs,
        out_specs=[pl.BlockSpec((ts, d), lambda t: (jnp.maximum(t - 1, 0), 0))] + cast_specs,
        out_shape=[jax.ShapeDtypeStruct(x2.shape, x2.dtype)]
        + [jax.ShapeDtypeStruct(w.shape, jnp.bfloat16) for w, _ in cast],
        scratch_shapes=stage_slot + stage_slot + [
            pltpu.VMEM((ts, D_MODEL), jnp.bfloat16),
            pltpu.VMEM((2, N_STACKS, 2, BLOCK, BLOCK), jnp.float32),
            pltpu.VMEM((n_blk * N_STACKS, 2 * BLOCK, BLOCK), jnp.float32),
            pltpu.VMEM((n_blk * N_STACKS, 2 * BLOCK, BLOCK), jnp.bfloat16),
            pltpu.VMEM(w_in.shape, jnp.bfloat16),
            pltpu.VMEM(w_out.shape, jnp.bfloat16),
            pltpu.VMEM((2, WEIGHT_STAGE_ROWS, w_in.shape[1]), jnp.float32),
            pltpu.VMEM(w_out.shape, jnp.float32),
            pltpu.VMEM((CHUNK, LANES), jnp.float32),
            pltpu.SemaphoreType.DMA((3,)),
        ],
        compiler_params=pltpu.CompilerParams(
            dimension_semantics=("arbitrary",),
            vmem_limit_bytes=VMEM_LIMIT_BYTES),
        name="mixer",
    )(x2, x2, g1, w_in, vgain, wsp, bsp, sinks, rel, w_out, *[w for w, _ in cast])
    return out.reshape(batch, seq, d), cast_out


def _ffn(h, p, g2, w1, w2, wproj, wgate, gf, *, final_norm):
    rows, d = h.shape
    tm = ROW_TILE
    n_tiles = rows // tm
    return pl.pallas_call(
        functools.partial(_ffn_kernel, final_norm=final_norm, n_tiles=n_tiles),
        grid=(n_tiles + 1,),
        in_specs=[
            pl.BlockSpec((tm, d), lambda t: (jnp.minimum(t, n_tiles - 1), 0)),
            pl.BlockSpec((tm, PLE_DIM), lambda t: (jnp.maximum(t - 1, 0), 0)),
            _resident(g2.shape), _resident(w1.shape), _resident(w2.shape),
            _resident(wproj.shape), _resident(wgate.shape), _resident(gf.shape),
        ],
        out_specs=pl.BlockSpec((tm, d), lambda t: (jnp.maximum(t - 1, 0), 0)),
        out_shape=jax.ShapeDtypeStruct(h.shape, h.dtype),
        scratch_shapes=[pltpu.VMEM((tm, d), jnp.float32)],
        compiler_params=pltpu.CompilerParams(
            dimension_semantics=("arbitrary",),
            vmem_limit_bytes=VMEM_LIMIT_BYTES),
        name="ffn",
    )(h, p, g2, w1, w2, wproj, wgate, gf)


def kernel(x, p, norm1_gain, w_in, gmlp_v_gain, w_spatial, b_spatial, attn_sinks, rel_bias_table, w_out, norm2_gain, w_ff1, w_ff2, w_ple_proj, w_ple_gate, final_gain):
    batch, seq, d = x.shape
    depth = w_in.shape[0]
    h = x
    for i in range(depth):
        h, (w1, w2, wproj, wgate) = _mixer(
            h, norm1_gain[i][None], w_in[i], gmlp_v_gain[i][None],
            w_spatial[i], b_spatial[i], attn_sinks[i], rel_bias_table.T, w_out[i],
            cast=[(w_ff1[i], 1), (w_ff2[i], 0), (w_ple_proj[i], 0), (w_ple_gate[i], 0)])
        h = _ffn(h.reshape(batch * seq, d), p[i].reshape(batch * seq, PLE_DIM),
                 norm2_gain[i][None], w1, w2, wproj, wgate,
                 final_gain[None], final_norm=(i == depth - 1)).reshape(batch, seq, d)
    return h
```

```python
import functools
import math

import jax
import jax.numpy as jnp
import numpy as np
from jax import lax
from jax.experimental import pallas as pl
from jax.experimental.pallas import tpu as pltpu

D_MODEL = 1024
PLE_DIM = 256
D_GMLP = 512
GMLP_GROUPS = 4
GROUP_DIM = D_GMLP // GMLP_GROUPS
CHUNK = 128
D_ATTN = 512
HEAD_DIM = 64
N_Q_HEADS = D_ATTN // HEAD_DIM
N_KV_HEADS = 2
Q_PER_KV = N_Q_HEADS // N_KV_HEADS
WINDOW = 128
BLOCK = WINDOW
REL_BUCKETS = 32
REL_MAX_DIST = 128
D_FF = 4 * D_MODEL
KV_W = N_KV_HEADS * HEAD_DIM
EPS = 1e-6
NEG_INF = -1e30
LOG2_E = math.log2(math.e)

LANES = 128
BF16_SUBLANES = 16
N_VARIANTS = 2 * N_KV_HEADS
N_STACKS = N_Q_HEADS // 2
SEQ_TILE = 512
ROW_TILE = 1024
FF_CHUNK = 1024
FFN_ROW_PART = 256
WEIGHT_STAGE_ROWS = 256
VMEM_LIMIT_BYTES = 56 * 1024 * 1024


def _bucket_thresholds():
    max_exact = REL_BUCKETS // 2
    n = np.arange(WINDOW)
    nf = np.maximum(n, 1).astype(np.float32)
    large = max_exact + (np.log(nf / max_exact) / np.float32(math.log(REL_MAX_DIST / max_exact))
                         * (REL_BUCKETS - max_exact)).astype(np.int32)
    bucket = np.where(n < max_exact, n, np.minimum(large, REL_BUCKETS - 1))
    assert (np.diff(bucket) >= 0).all()
    return [(b, int(np.argmax(bucket >= b))) for b in range(1, REL_BUCKETS) if (bucket >= b).any()]


_BUCKET_THRESHOLDS = _bucket_thresholds()


def _rms(x, gain):
    return x * lax.rsqrt(jnp.mean(x * x, axis=-1, keepdims=True) + EPS) * gain


def _gelu_tanh(x):
    c = float(np.float32(np.sqrt(2 / np.pi)))
    poly = x * x * (-2.0 * c * 0.044715 * LOG2_E) + (-2.0 * c * LOG2_E)
    return x / (1.0 + jnp.exp2(x * poly))


def _dot(a, b):
    return jnp.dot(a, b, preferred_element_type=jnp.float32)


def _dot_nt(a, b):
    return lax.dot_general(a, b, (((1,), (1,)), ((), ())), preferred_element_type=jnp.float32)


def _build_bias(rel_ref, bias_ref):
    a = lax.broadcasted_iota(jnp.int32, (BLOCK, BLOCK), 0)
    j = lax.broadcasted_iota(jnp.int32, (BLOCK, BLOCK), 1)
    from_prev = j > a
    n = jnp.where(from_prev, BLOCK + a - j, a - j)

    def entry(b, h):
        return jnp.broadcast_to(rel_ref[h:h + 1, b:b + 1], (BLOCK, BLOCK))

    for h in range(N_Q_HEADS):
        val = entry(0, h)
        for b, thr in _BUCKET_THRESHOLDS:
            val = jnp.where(n >= thr, entry(b, h), val)
        stack = 2 * (h // Q_PER_KV) + h % 2
        half = (h % Q_PER_KV) // 2
        val = val * LOG2_E
        bias_ref[0, stack, half] = val
        bias_ref[1, stack, half] = jnp.where(from_prev, NEG_INF, val)


def _build_spatial_tables(wsp_ref, bsp_ref, wtril_ref, bm_ref):
    row = lax.broadcasted_iota(jnp.int32, (CHUNK, CHUNK), 0)
    col = lax.broadcasted_iota(jnp.int32, (CHUNK, CHUNK), 1)
    padded = jnp.zeros((CHUNK, CHUNK), jnp.float32)
    for g in range(GMLP_GROUPS):
        wtril_ref[g] = jnp.where(col <= row, wsp_ref[g], 0.0).astype(wtril_ref.dtype)
        padded = jnp.where(row == g, jnp.broadcast_to(bsp_ref[g:g + 1, :], (CHUNK, CHUNK)), padded)
    bm_ref[...] = padded.T


def _load_as_bf16(src_hbm, dst_ref, stage_ref, sem_ref, sem_base, between=None):
    n_chunks = src_hbm.shape[0] // stage_ref.shape[1]
    rows = stage_ref.shape[1]

    def copy(k):
        return pltpu.make_async_copy(src_hbm.at[pl.ds(k * rows, rows), :],
                                     stage_ref.at[k % 2], sem_ref.at[sem_base + k % 2])

    copy(0).start()
    if between is not None:
        between()
    for k in range(n_chunks):
        if k + 1 < n_chunks:
            copy(k + 1).start()
        copy(k).wait()
        dst_ref[k * rows:(k + 1) * rows, :] = stage_ref[k % 2].astype(dst_ref.dtype)


_VARIANT = ((0, 1), (2, 3))


def _project_pieces(x_ref, g1_ref, w_in_ref, vgain_ref, dst):
    u_ref, vn_ref, q_ref, kbuf, vbuf = dst
    ts = x_ref.shape[0]
    o = 2 * D_GMLP
    state = {}

    def norm():
        state["hn"] = _rms(x_ref[...], g1_ref[...]).astype(jnp.bfloat16)

    def gate_u():
        u_ref[...] = _gelu_tanh(_dot(state["hn"], w_in_ref[:, 0:D_GMLP]))

    def gate_v():
        vg = _gelu_tanh(_dot(state["hn"], w_in_ref[:, D_GMLP:2 * D_GMLP]))
        for g in range(GMLP_GROUPS):
            cols = slice(g * GROUP_DIM, (g + 1) * GROUP_DIM)
            vn_ref[:, cols] = _rms(vg[:, cols], vgain_ref[:, cols]).astype(vn_ref.dtype)

    def query():
        q = _dot(state["hn"], w_in_ref[:, o:o + D_ATTN]) * (HEAD_DIM ** -0.5 * LOG2_E)
        q_ref[...] = q.astype(q_ref.dtype)

    def key_value():
        kv = _dot(state["hn"], w_in_ref[:, o + D_ATTN:o + D_ATTN + 2 * KV_W])
        lo = lax.broadcasted_iota(jnp.int32, (ts, LANES), 1) < HEAD_DIM
        for buf, t in ((kbuf, kv[:, 0:KV_W]), (vbuf, kv[:, KV_W:2 * KV_W])):
            tr = pltpu.roll(t, HEAD_DIM, axis=1)
            buf[0, BLOCK:BLOCK + ts, :] = jnp.where(lo, t, 0.0).astype(buf.dtype)
            buf[1, BLOCK:BLOCK + ts, :] = jnp.where(lo, 0.0, tr).astype(buf.dtype)
            buf[2, BLOCK:BLOCK + ts, :] = jnp.where(lo, tr, 0.0).astype(buf.dtype)
            buf[3, BLOCK:BLOCK + ts, :] = jnp.where(lo, 0.0, t).astype(buf.dtype)

    return [norm, gate_u, gate_v, query, key_value]


def _attend_pieces(i, src, seq_start, sink_ref, bias_ref, mix_ref, s_scr, p_scr):
    _, _, q_ref, kbuf, vbuf = src
    rows = slice(i * BLOCK, (i + 1) * BLOCK)
    keys = slice(i * BLOCK, (i + 2) * BLOCK)
    first = jnp.where(seq_start, 1, 0) if i == 0 else 0
    upper = lax.broadcasted_iota(jnp.int32, (2 * BLOCK, 1), 0) < BLOCK
    query = lax.broadcasted_iota(jnp.int32, (2 * BLOCK, BLOCK), 0) % BLOCK
    from_prev = lax.broadcasted_iota(jnp.int32, (2 * BLOCK, BLOCK), 1) > query
    stacks = [(g, parity) for g in range(N_KV_HEADS) for parity in range(2)]
    state = {}

    def scores():
        for g, parity in stacks:
            qs = jnp.concatenate([q_ref[rows, 2 * g * LANES:(2 * g + 1) * LANES],
                                  q_ref[rows, (2 * g + 1) * LANES:(2 * g + 2) * LANES]], axis=0)
            s = _dot_nt(qs, kbuf[_VARIANT[g][parity], keys, :])
            s = jnp.where(from_prev, s[:, 0:BLOCK], s[:, BLOCK:2 * BLOCK])
            s_scr[N_STACKS * i + 2 * g + parity] = (
                s + bias_ref[first, 2 * g + parity].reshape(2 * BLOCK, BLOCK))

    def softmax():
        zero = jnp.zeros((2 * BLOCK, BLOCK), jnp.bfloat16)
        for g, parity in stacks:
            head = Q_PER_KV * g + parity
            sink = jnp.where(upper, sink_ref[head], sink_ref[head + 2]) * LOG2_E
            s = s_scr[N_STACKS * i + 2 * g + parity]
            m = jnp.maximum(jnp.max(s, axis=-1, keepdims=True), sink)
            e = jnp.exp2(s - m)
            denom = jnp.sum(e, axis=-1, keepdims=True) + jnp.exp2(sink - m)
            p = e.astype(jnp.bfloat16)
            p_scr[N_STACKS * i + 2 * g + parity] = jnp.concatenate(
                [jnp.where(from_prev, p, zero), jnp.where(from_prev, zero, p)], axis=1)
            state["inv", g, parity] = 1.0 / denom

    def values():
        low = lax.broadcasted_iota(jnp.int32, (2 * BLOCK, LANES), 1) < HEAD_DIM
        for g in range(N_KV_HEADS):
            pair0 = D_GMLP + 2 * g * LANES
            acc = (_dot(p_scr[N_STACKS * i + 2 * g], vbuf[_VARIANT[g][0], keys, :])
                   + _dot(p_scr[N_STACKS * i + 2 * g + 1], vbuf[_VARIANT[g][1], keys, :]))
            acc = acc * jnp.where(low, state["inv", g, 0], state["inv", g, 1])
            mix_ref[rows, pair0:pair0 + LANES] = acc[0:BLOCK].astype(mix_ref.dtype)
            mix_ref[rows, pair0 + LANES:pair0 + 2 * LANES] = acc[BLOCK:].astype(mix_ref.dtype)

    return scores, softmax, values


def _spatial_gate(src, wtril_ref, bm_ref, mix_ref):
    u_ref, vn_ref = src[0:2]
    for g in range(GMLP_GROUPS):
        cols = slice(g * GROUP_DIM, (g + 1) * GROUP_DIM)
        b_g = bm_ref[:, g:g + 1]
        for c in range(u_ref.shape[0] // CHUNK):
            rows = slice(c * CHUNK, (c + 1) * CHUNK)
            sv = _dot(wtril_ref[g], vn_ref[rows, cols]) + b_g
            mix_ref[rows, cols] = (u_ref[rows, cols] * sv).astype(mix_ref.dtype)


N_CAST = 4
SCORES_AHEAD = 2


def _mixer_kernel(x_ref, xprev_ref, g1_ref, w_in_hbm, vgain_ref, wsp_ref, bsp_ref, sink_ref,
                  rel_ref, w_out_hbm, *rest, tiles_per_seq, n_tiles):
    cast_src, o_ref = rest[0:N_CAST], rest[N_CAST]
    cast_dst, scratch = rest[N_CAST + 1:2 * N_CAST + 1], rest[2 * N_CAST + 1:]
    slots = (scratch[0:5], scratch[5:10])
    mix_ref, bias_ref, s_scr, p_scr = scratch[10:14]
    w_in_ref, w_out_ref, stage_in, stage_out, bm_ref, wtril_ref, dma_sem = scratch[14:21]
    ts = x_ref.shape[0]
    t = pl.program_id(0)

    cur_starts = t % tiles_per_seq == 0
    prev_started = (t + tiles_per_seq - 1) % tiles_per_seq == 0
    n_blk = ts // BLOCK

    def step(dst, src):
        project = _project_pieces(x_ref, g1_ref, w_in_ref, vgain_ref, dst) if dst else []
        attend = [_attend_pieces(i, src, prev_started, sink_ref, bias_ref, mix_ref, s_scr,
                                 p_scr) for i in range(n_blk)] if src else []
        if src:
            for i in range(SCORES_AHEAD):
                attend[i][0]()
            _spatial_gate(src, wtril_ref, bm_ref, mix_ref)
        if dst:
            project.pop(0)()
        for i in range(n_blk):
            if src:
                if i + SCORES_AHEAD < n_blk:
                    attend[i + SCORES_AHEAD][0]()
                attend[i][1]()
            if dst:
                project.pop(0)()
            if src:
                attend[i][2]()
        assert not project
        if src:
            o_ref[...] = xprev_ref[...] + _dot(mix_ref[...], w_out_ref[...])
        for s, d in zip(cast_src, cast_dst):
            d[...] = s[...].astype(d.dtype)
        if dst:
            zeros = jnp.zeros((N_VARIANTS, BLOCK, LANES), dst[3].dtype)
            for k in (3, 4):
                dst[k][:, 0:BLOCK, :] = (
                    jnp.where(cur_starts, zeros, src[k][:, ts:ts + BLOCK, :]) if src else zeros)

    @pl.when(t == 0)
    def _():
        def build_tables():
            _build_bias(rel_ref, bias_ref)
            _build_spatial_tables(wsp_ref, bsp_ref, wtril_ref, bm_ref)

        fetch_w_out = pltpu.make_async_copy(w_out_hbm, stage_out, dma_sem.at[2])
        fetch_w_out.start()
        _load_as_bf16(w_in_hbm, w_in_ref, stage_in, dma_sem, 0, between=build_tables)
        step(slots[0], None)
        fetch_w_out.wait()
        w_out_ref[...] = stage_out[...].astype(w_out_ref.dtype)

    @pl.when(t == n_tiles)
    def _():
        step(None, slots[(n_tiles - 1) % 2])

    @pl.when((t > 0) & (t < n_tiles) & (t % 2 == 0))
    def _():
        step(slots[0], slots[1])

    @pl.when((t < n_tiles) & (t % 2 == 1))
    def _():
        step(slots[1], slots[0])


def _ffn_kernel(h_ref, p_ref, g2_ref, w1_ref, w2_ref, wproj_ref, wgate_ref, gf_ref, o_ref,
                acc_ref, accb_ref, *, final_norm, n_tiles):
    tm = h_ref.shape[0]
    parts = [slice(r, r + FFN_ROW_PART) for r in range(0, tm, FFN_ROW_PART)]

    def step(expand, finish):
        hidden = []

        def expand_chunk(c, hn_parts):
            cols = slice(c * FF_CHUNK, (c + 1) * FF_CHUNK)
            if c == 0:
                t = jnp.concatenate([_dot(part, w1_ref[:, cols]) for part in hn_parts], axis=0)
            else:
                t = _dot(jnp.concatenate(hn_parts, axis=0), w1_ref[:, cols])
            t = jnp.maximum(t, 0.0)
            hidden.append((t * t).astype(jnp.bfloat16))

        n_chunks = D_FF // FF_CHUNK
        assert len(parts) == n_chunks, "one finishing row part is issued per expand chunk"
        if expand:
            hn_parts = [_rms(h_ref[r, :], g2_ref[...]).astype(jnp.bfloat16) for r in parts]
        if finish:
            ple = _dot(p_ref[...].astype(jnp.bfloat16), wproj_ref[...])
        for c in range(n_chunks):
            if expand:
                expand_chunk(c, hn_parts)
            if finish:
                r = parts[c]
                z = _dot(accb_ref[r, :], wgate_ref[...])
                out = acc_ref[r, :] + ple[r] / (1.0 + jnp.exp2(z * (-LOG2_E)))
                o_ref[r, :] = _rms(out, gf_ref[...]) if final_norm else out
        if expand:
            acc = h_ref[...] + _dot(jnp.concatenate(hidden, axis=1), w2_ref[...])
            acc_ref[...] = acc
            accb_ref[...] = acc.astype(accb_ref.dtype)

    t = pl.program_id(0)

    @pl.when(t == 0)
    def _():
        acc_ref[...] = jnp.zeros(acc_ref.shape, acc_ref.dtype)
        accb_ref[...] = jnp.zeros(accb_ref.shape, accb_ref.dtype)

    @pl.when(t < n_tiles)
    def _():
        step(True, True)

    @pl.when(t == n_tiles)
    def _():
        step(False, True)


def _resident(shape):
    return pl.BlockSpec(shape, lambda *_: (0,) * len(shape), pipeline_mode=pl.Buffered(1))


def _cast_slice_spec(arr, axis, n_tiles):
    size = arr.shape[axis]
    min_thick = BF16_SUBLANES if axis == 0 else LANES
    thick = max(size // n_tiles, min_thick)
    assert size % thick == 0
    last = size // thick - 1
    block = tuple(thick if a == axis else n for a, n in enumerate(arr.shape))
    return pl.BlockSpec(block, lambda t: tuple(jnp.minimum(t, last) if a == axis else 0
                                               for a in range(arr.ndim)))


def _mixer(x, g1, w_in, vgain, wsp, bsp, sinks, rel, w_out, cast):
    batch, seq, d = x.shape
    ts = SEQ_TILE
    n_blk = ts // BLOCK
    assert n_blk == 4, "stage-1 pieces are paired one-to-one with attention blocks"
    assert len(cast) == N_CAST
    n_tiles = batch * seq // ts
    x2 = x.reshape(batch * seq, d)
    smem = pl.BlockSpec(memory_space=pltpu.SMEM)
    hbm = pl.BlockSpec(memory_space=pl.ANY)
    cast_specs = [_cast_slice_spec(w, axis, n_tiles) for w, axis in cast]
    stage_slot = [
        pltpu.VMEM((ts, D_GMLP), jnp.float32),
        pltpu.VMEM((ts, D_GMLP), jnp.bfloat16),
        pltpu.VMEM((ts, D_ATTN), jnp.bfloat16),
        pltpu.VMEM((N_VARIANTS, BLOCK + ts, LANES), jnp.bfloat16),
        pltpu.VMEM((N_VARIANTS, BLOCK + ts, LANES), jnp.bfloat16),
    ]
    out, *cast_out = pl.pallas_call(
        functools.partial(_mixer_kernel, tiles_per_seq=seq // ts, n_tiles=n_tiles),
        grid=(n_tiles + 1,),
        in_specs=[
            pl.BlockSpec((ts, d), lambda t: (jnp.minimum(t, n_tiles - 1), 0)),
            pl.BlockSpec((ts, d), lambda t: (jnp.maximum(t - 1, 0), 0)),
            _resident(g1.shape), hbm, _resident(vgain.shape),
            _resident(wsp.shape), _resident(bsp.shape), smem, _resident(rel.shape),
            hbm,
        ] + cast_specs,
        out_specs=[pl.BlockSpec((ts, d), lambda t: (jnp.maximum(t - 1, 0), 0))] + cast_specs,
        out_shape=[jax.ShapeDtypeStruct(x2.shape, x2.dtype)]
        + [jax.ShapeDtypeStruct(w.shape, jnp.bfloat16) for w, _ in cast],
        scratch_shapes=stage_slot + stage_slot + [
            pltpu.VMEM((ts, D_MODEL), jnp.bfloat16),
            pltpu.VMEM((2, N_STACKS, 2, BLOCK, BLOCK), jnp.float32),
            pltpu.VMEM((n_blk * N_STACKS, 2 * BLOCK, BLOCK), jnp.float32),
            pltpu.VMEM((n_blk * N_STACKS, 2 * BLOCK, 2 * BLOCK), jnp.bfloat16),
            pltpu.VMEM(w_in.shape, jnp.bfloat16),
            pltpu.VMEM(w_out.shape, jnp.bfloat16),
            pltpu.VMEM((2, WEIGHT_STAGE_ROWS, w_in.shape[1]), jnp.float32),
            pltpu.VMEM(w_out.shape, jnp.float32),
            pltpu.VMEM((CHUNK, LANES), jnp.float32),
            pltpu.VMEM((GMLP_GROUPS, CHUNK, CHUNK), jnp.bfloat16),
            pltpu.SemaphoreType.DMA((3,)),
        ],
        compiler_params=pltpu.CompilerParams(
            dimension_semantics=("arbitrary",),
            vmem_limit_bytes=VMEM_LIMIT_BYTES),
        name="mixer",
    )(x2, x2, g1, w_in, vgain, wsp, bsp, sinks, rel, w_out, *[w for w, _ in cast])
    return out.reshape(batch, seq, d), cast_out


def _ffn(h, p, g2, w1, w2, wproj, wgate, gf, *, final_norm):
    rows, d = h.shape
    tm = ROW_TILE
    n_tiles = rows // tm
    return pl.pallas_call(
        functools.partial(_ffn_kernel, final_norm=final_norm, n_tiles=n_tiles),
        grid=(n_tiles + 1,),
        in_specs=[
            pl.BlockSpec((tm, d), lambda t: (jnp.minimum(t, n_tiles - 1), 0)),
            pl.BlockSpec((tm, PLE_DIM), lambda t: (jnp.maximum(t - 1, 0), 0)),
            _resident(g2.shape), _resident(w1.shape), _resident(w2.shape),
            _resident(wproj.shape), _resident(wgate.shape), _resident(gf.shape),
        ],
        out_specs=pl.BlockSpec((tm, d), lambda t: (jnp.maximum(t - 1, 0), 0)),
        out_shape=jax.ShapeDtypeStruct(h.shape, h.dtype),
        scratch_shapes=[pltpu.VMEM((tm, d), jnp.float32),
                        pltpu.VMEM((tm, d), jnp.bfloat16)],
        compiler_params=pltpu.CompilerParams(
            dimension_semantics=("arbitrary",),
            vmem_limit_bytes=VMEM_LIMIT_BYTES),
        name="ffn",
    )(h, p, g2, w1, w2, wproj, wgate, gf)


def kernel(x, p, norm1_gain, w_in, gmlp_v_gain, w_spatial, b_spatial, attn_sinks, rel_bias_table, w_out, norm2_gain, w_ff1, w_ff2, w_ple_proj, w_ple_gate, final_gain):
    batch, seq, d = x.shape
    depth = w_in.shape[0]
    h = x
    for i in range(depth):
        h, (w1, w2, wproj, wgate) = _mixer(
            h, norm1_gain[i][None], w_in[i], gmlp_v_gain[i][None],
            w_spatial[i], b_spatial[i], attn_sinks[i], rel_bias_table.T, w_out[i],
            cast=[(w_ff1[i], 1), (w_ff2[i], 0), (w_ple_proj[i], 0), (w_ple_gate[i], 0)])
        h = _ffn(h.reshape(batch * seq, d), p[i].reshape(batch * seq, PLE_DIM),
                 norm2_gain[i][None], w1, w2, wproj, wgate,
                 final_gain[None], final_norm=(i == depth - 1)).reshape(batch, seq, d)
    return h
```

```python
import functools
import math

import jax
import jax.numpy as jnp
import numpy as np
from jax import lax
from jax.experimental import pallas as pl
from jax.experimental.pallas import tpu as pltpu

D_MODEL = 1024
PLE_DIM = 256
D_GMLP = 512
GMLP_GROUPS = 4
GROUP_DIM = D_GMLP // GMLP_GROUPS
CHUNK = 128
D_ATTN = 512
HEAD_DIM = 64
N_Q_HEADS = D_ATTN // HEAD_DIM
N_KV_HEADS = 2
Q_PER_KV = N_Q_HEADS // N_KV_HEADS
WINDOW = 128
BLOCK = WINDOW
REL_BUCKETS = 32
REL_MAX_DIST = 128
D_FF = 4 * D_MODEL
KV_W = N_KV_HEADS * HEAD_DIM
EPS = 1e-6
NEG_INF = -1e30
LOG2_E = math.log2(math.e)

LANES = 128
BF16_SUBLANES = 16
N_VARIANTS = 2 * N_KV_HEADS
N_STACKS = N_Q_HEADS // 2
SEQ_TILE = 512
ROW_TILE = 1024
FF_CHUNK = 1024
FFN_ROW_PART = 256
WEIGHT_STAGE_ROWS = 256
VMEM_LIMIT_BYTES = 56 * 1024 * 1024


def _bucket_thresholds():
    max_exact = REL_BUCKETS // 2
    n = np.arange(WINDOW)
    nf = np.maximum(n, 1).astype(np.float32)
    large = max_exact + (np.log(nf / max_exact) / np.float32(math.log(REL_MAX_DIST / max_exact))
                         * (REL_BUCKETS - max_exact)).astype(np.int32)
    bucket = np.where(n < max_exact, n, np.minimum(large, REL_BUCKETS - 1))
    assert (np.diff(bucket) >= 0).all()
    return [(b, int(np.argmax(bucket >= b))) for b in range(1, REL_BUCKETS) if (bucket >= b).any()]


_BUCKET_THRESHOLDS = _bucket_thresholds()


def _rms(x, gain):
    return x * lax.rsqrt(jnp.mean(x * x, axis=-1, keepdims=True) + EPS) * gain


def _gelu_tanh(x):
    c = float(np.float32(np.sqrt(2 / np.pi)))
    poly = x * x * (-2.0 * c * 0.044715 * LOG2_E) + (-2.0 * c * LOG2_E)
    return x / (1.0 + jnp.exp2(x * poly))


def _dot(a, b):
    return jnp.dot(a, b, preferred_element_type=jnp.float32)


def _dot_nt(a, b):
    return lax.dot_general(a, b, (((1,), (1,)), ((), ())), preferred_element_type=jnp.float32)


def _build_bias(rel_ref, bias_ref):
    a = lax.broadcasted_iota(jnp.int32, (BLOCK, BLOCK), 0)
    j = lax.broadcasted_iota(jnp.int32, (BLOCK, BLOCK), 1)
    from_prev = j > a
    n = jnp.where(from_prev, BLOCK + a - j, a - j)

    def entry(b, h):
        return jnp.broadcast_to(rel_ref[h:h + 1, b:b + 1], (BLOCK, BLOCK))

    for h in range(N_Q_HEADS):
        val = entry(0, h)
        for b, thr in _BUCKET_THRESHOLDS:
            val = jnp.where(n >= thr, entry(b, h), val)
        stack = 2 * (h // Q_PER_KV) + h % 2
        half = (h % Q_PER_KV) // 2
        val = val * LOG2_E
        bias_ref[0, stack, half] = val
        bias_ref[1, stack, half] = jnp.where(from_prev, NEG_INF, val)


def _build_spatial_bias(bsp_ref, bm_ref):
    row = lax.broadcasted_iota(jnp.int32, (CHUNK, CHUNK), 0)
    padded = jnp.zeros((CHUNK, CHUNK), jnp.float32)
    for g in range(GMLP_GROUPS):
        padded = jnp.where(row == g, jnp.broadcast_to(bsp_ref[g:g + 1, :], (CHUNK, CHUNK)), padded)
    bm_ref[...] = padded.T


def _load_as_bf16(src_hbm, dst_ref, stage_ref, sem_ref, sem_base, between=None):
    n_chunks = src_hbm.shape[0] // stage_ref.shape[1]
    rows = stage_ref.shape[1]

    def copy(k):
        return pltpu.make_async_copy(src_hbm.at[pl.ds(k * rows, rows), :],
                                     stage_ref.at[k % 2], sem_ref.at[sem_base + k % 2])

    copy(0).start()
    if between is not None:
        between()
    for k in range(n_chunks):
        if k + 1 < n_chunks:
            copy(k + 1).start()
        copy(k).wait()
        dst_ref[k * rows:(k + 1) * rows, :] = stage_ref[k % 2].astype(dst_ref.dtype)


_VARIANT = ((0, 1), (2, 3))


def _project_pieces(x_ref, g1_ref, w_in_ref, vgain_ref, dst):
    u_ref, vn_ref, q_ref, kbuf, vbuf = dst
    ts = x_ref.shape[0]
    o = 2 * D_GMLP
    state = {}

    def norm():
        state["hn"] = _rms(x_ref[...], g1_ref[...]).astype(jnp.bfloat16)

    def gate_u():
        u_ref[...] = _gelu_tanh(_dot(state["hn"], w_in_ref[:, 0:D_GMLP]))

    def gate_v():
        vg = _gelu_tanh(_dot(state["hn"], w_in_ref[:, D_GMLP:2 * D_GMLP]))
        for g in range(GMLP_GROUPS):
            cols = slice(g * GROUP_DIM, (g + 1) * GROUP_DIM)
            vn_ref[:, cols] = _rms(vg[:, cols], vgain_ref[:, cols]).astype(vn_ref.dtype)

    def query():
        q = _dot(state["hn"], w_in_ref[:, o:o + D_ATTN]) * (HEAD_DIM ** -0.5 * LOG2_E)
        q_ref[...] = q.astype(q_ref.dtype)

    def key_value():
        kv = _dot(state["hn"], w_in_ref[:, o + D_ATTN:o + D_ATTN + 2 * KV_W])
        lo = lax.broadcasted_iota(jnp.int32, (ts, LANES), 1) < HEAD_DIM
        for buf, t in ((kbuf, kv[:, 0:KV_W]), (vbuf, kv[:, KV_W:2 * KV_W])):
            tr = pltpu.roll(t, HEAD_DIM, axis=1)
            buf[0, BLOCK:BLOCK + ts, :] = jnp.where(lo, t, 0.0).astype(buf.dtype)
            buf[1, BLOCK:BLOCK + ts, :] = jnp.where(lo, 0.0, tr).astype(buf.dtype)
            buf[2, BLOCK:BLOCK + ts, :] = jnp.where(lo, tr, 0.0).astype(buf.dtype)
            buf[3, BLOCK:BLOCK + ts, :] = jnp.where(lo, 0.0, t).astype(buf.dtype)

    return [norm, gate_u, gate_v, query, key_value]


def _attend_pieces(i, src, seq_start, sink_ref, bias_ref, mix_ref, s_scr, p_scr):
    _, _, q_ref, kbuf, vbuf = src
    rows = slice(i * BLOCK, (i + 1) * BLOCK)
    keys = slice(i * BLOCK, (i + 2) * BLOCK)
    first = jnp.where(seq_start, 1, 0) if i == 0 else 0
    upper = lax.broadcasted_iota(jnp.int32, (2 * BLOCK, 1), 0) < BLOCK
    query = lax.broadcasted_iota(jnp.int32, (2 * BLOCK, BLOCK), 0) % BLOCK
    from_prev = lax.broadcasted_iota(jnp.int32, (2 * BLOCK, BLOCK), 1) > query
    stacks = [(g, parity) for g in range(N_KV_HEADS) for parity in range(2)]
    state = {}

    def scores():
        for g, parity in stacks:
            qs = jnp.concatenate([q_ref[rows, 2 * g * LANES:(2 * g + 1) * LANES],
                                  q_ref[rows, (2 * g + 1) * LANES:(2 * g + 2) * LANES]], axis=0)
            s = _dot_nt(qs, kbuf[_VARIANT[g][parity], keys, :])
            s = jnp.where(from_prev, s[:, 0:BLOCK], s[:, BLOCK:2 * BLOCK])
            s_scr[N_STACKS * i + 2 * g + parity] = (
                s + bias_ref[first, 2 * g + parity].reshape(2 * BLOCK, BLOCK))

    def softmax():
        zero = jnp.zeros((2 * BLOCK, BLOCK), jnp.bfloat16)
        for g, parity in stacks:
            head = Q_PER_KV * g + parity
            sink = jnp.where(upper, sink_ref[head], sink_ref[head + 2]) * LOG2_E
            s = s_scr[N_STACKS * i + 2 * g + parity]
            m = jnp.maximum(jnp.max(s, axis=-1, keepdims=True), sink)
            e = jnp.exp2(s - m)
            denom = jnp.sum(e, axis=-1, keepdims=True) + jnp.exp2(sink - m)
            p = e.astype(jnp.bfloat16)
            p_scr[N_STACKS * i + 2 * g + parity] = jnp.concatenate(
                [jnp.where(from_prev, p, zero), jnp.where(from_prev, zero, p)], axis=1)
            state["inv", g, parity] = 1.0 / denom

    def values():
        low = lax.broadcasted_iota(jnp.int32, (2 * BLOCK, LANES), 1) < HEAD_DIM
        for g in range(N_KV_HEADS):
            pair0 = D_GMLP + 2 * g * LANES
            acc = (_dot(p_scr[N_STACKS * i + 2 * g], vbuf[_VARIANT[g][0], keys, :])
                   + _dot(p_scr[N_STACKS * i + 2 * g + 1], vbuf[_VARIANT[g][1], keys, :]))
            acc = acc * jnp.where(low, state["inv", g, 0], state["inv", g, 1])
            mix_ref[rows, pair0:pair0 + LANES] = acc[0:BLOCK].astype(mix_ref.dtype)
            mix_ref[rows, pair0 + LANES:pair0 + 2 * LANES] = acc[BLOCK:].astype(mix_ref.dtype)

    return scores, softmax, values


def _spatial_gate(src, wsp_ref, bm_ref, mix_ref):
    u_ref, vn_ref = src[0:2]
    t_idx = lax.broadcasted_iota(jnp.int32, (CHUNK, CHUNK), 0)
    s_idx = lax.broadcasted_iota(jnp.int32, (CHUNK, CHUNK), 1)
    causal = s_idx <= t_idx
    for g in range(GMLP_GROUPS):
        cols = slice(g * GROUP_DIM, (g + 1) * GROUP_DIM)
        w_g = jnp.where(causal, wsp_ref[g], 0.0).astype(jnp.bfloat16)
        b_g = bm_ref[:, g:g + 1]
        for c in range(u_ref.shape[0] // CHUNK):
            rows = slice(c * CHUNK, (c + 1) * CHUNK)
            sv = _dot(w_g, vn_ref[rows, cols]) + b_g
            mix_ref[rows, cols] = (u_ref[rows, cols] * sv).astype(mix_ref.dtype)


N_CAST = 4
SCORES_AHEAD = 2


def _mixer_kernel(x_ref, xprev_ref, g1_ref, w_in_hbm, vgain_ref, wsp_ref, bsp_ref, sink_ref,
                  rel_ref, w_out_hbm, *rest, tiles_per_seq, n_tiles):
    cast_src, o_ref = rest[0:N_CAST], rest[N_CAST]
    cast_dst, scratch = rest[N_CAST + 1:2 * N_CAST + 1], rest[2 * N_CAST + 1:]
    slots = (scratch[0:5], scratch[5:10])
    mix_ref, bias_ref, s_scr, p_scr = scratch[10:14]
    w_in_ref, w_out_ref, stage_in, stage_out, bm_ref, dma_sem = scratch[14:20]
    ts = x_ref.shape[0]
    t = pl.program_id(0)

    cur_starts = t % tiles_per_seq == 0
    prev_started = (t + tiles_per_seq - 1) % tiles_per_seq == 0
    n_blk = ts // BLOCK

    def step(dst, src):
        project = _project_pieces(x_ref, g1_ref, w_in_ref, vgain_ref, dst) if dst else []
        attend = [_attend_pieces(i, src, prev_started, sink_ref, bias_ref, mix_ref, s_scr,
                                 p_scr) for i in range(n_blk)] if src else []
        if src:
            for i in range(SCORES_AHEAD):
                attend[i][0]()
            _spatial_gate(src, wsp_ref, bm_ref, mix_ref)
        if dst:
            project.pop(0)()
        for i in range(n_blk):
            if src:
                if i + SCORES_AHEAD < n_blk:
                    attend[i + SCORES_AHEAD][0]()
                attend[i][1]()
            if dst:
                project.pop(0)()
            if src:
                attend[i][2]()
        assert not project
        if src:
            o_ref[...] = xprev_ref[...] + _dot(mix_ref[...], w_out_ref[...])
        for s, d in zip(cast_src, cast_dst):
            d[...] = s[...].astype(d.dtype)
        if dst:
            zeros = jnp.zeros((N_VARIANTS, BLOCK, LANES), dst[3].dtype)
            for k in (3, 4):
                dst[k][:, 0:BLOCK, :] = (
                    jnp.where(cur_starts, zeros, src[k][:, ts:ts + BLOCK, :]) if src else zeros)

    @pl.when(t == 0)
    def _():
        def build_tables():
            _build_bias(rel_ref, bias_ref)
            _build_spatial_bias(bsp_ref, bm_ref)

        fetch_w_out = pltpu.make_async_copy(w_out_hbm, stage_out, dma_sem.at[2])
        fetch_w_out.start()
        _load_as_bf16(w_in_hbm, w_in_ref, stage_in, dma_sem, 0, between=build_tables)
        step(slots[0], None)
        fetch_w_out.wait()
        w_out_ref[...] = stage_out[...].astype(w_out_ref.dtype)

    @pl.when(t == n_tiles)
    def _():
        step(None, slots[(n_tiles - 1) % 2])

    @pl.when((t > 0) & (t < n_tiles) & (t % 2 == 0))
    def _():
        step(slots[0], slots[1])

    @pl.when((t < n_tiles) & (t % 2 == 1))
    def _():
        step(slots[1], slots[0])


def _ffn_kernel(h_ref, p_ref, g2_ref, w1_ref, w2_ref, wproj_ref, wgate_ref, gf_ref, o_ref,
                acc_ref, *, final_norm, n_tiles):
    tm = h_ref.shape[0]
    parts = [slice(r, r + FFN_ROW_PART) for r in range(0, tm, FFN_ROW_PART)]

    def step(expand, finish):
        hidden = []

        def expand_chunk(c, hn_parts):
            cols = slice(c * FF_CHUNK, (c + 1) * FF_CHUNK)
            if c == 0:
                t = jnp.concatenate([_dot(part, w1_ref[:, cols]) for part in hn_parts], axis=0)
            else:
                t = _dot(jnp.concatenate(hn_parts, axis=0), w1_ref[:, cols])
            t = jnp.maximum(t, 0.0)
            hidden.append((t * t).astype(jnp.bfloat16))

        n_chunks = D_FF // FF_CHUNK
        assert len(parts) == n_chunks, "one finishing row part is issued per expand chunk"
        if expand:
            hn_parts = [_rms(h_ref[r, :], g2_ref[...]).astype(jnp.bfloat16) for r in parts]
        if finish:
            ple = _dot(p_ref[...].astype(jnp.bfloat16), wproj_ref[...])
        for c in range(n_chunks):
            if expand:
                expand_chunk(c, hn_parts)
            if finish:
                r = parts[c]
                z = _dot(acc_ref[r, :].astype(jnp.bfloat16), wgate_ref[...])
                out = acc_ref[r, :] + ple[r] / (1.0 + jnp.exp2(z * (-LOG2_E)))
                o_ref[r, :] = _rms(out, gf_ref[...]) if final_norm else out
        if expand:
            acc_ref[...] = h_ref[...] + _dot(jnp.concatenate(hidden, axis=1), w2_ref[...])

    t = pl.program_id(0)

    @pl.when(t == 0)
    def _():
        acc_ref[...] = jnp.zeros(acc_ref.shape, acc_ref.dtype)

    @pl.when(t < n_tiles)
    def _():
        step(True, True)

    @pl.when(t == n_tiles)
    def _():
        step(False, True)


def _resident(shape):
    return pl.BlockSpec(shape, lambda *_: (0,) * len(shape), pipeline_mode=pl.Buffered(1))


def _cast_slice_spec(arr, axis, n_tiles):
    size = arr.shape[axis]
    min_thick = BF16_SUBLANES if axis == 0 else LANES
    thick = max(size // n_tiles, min_thick)
    assert size % thick == 0
    last = size // thick - 1
    block = tuple(thick if a == axis else n for a, n in enumerate(arr.shape))
    return pl.BlockSpec(block, lambda t: tuple(jnp.minimum(t, last) if a == axis else 0
                                               for a in range(arr.ndim)))


def _mixer(x, g1, w_in, vgain, wsp, bsp, sinks, rel, w_out, cast):
    batch, seq, d = x.shape
    ts = SEQ_TILE
    n_blk = ts // BLOCK
    assert n_blk == 4, "stage-1 pieces are paired one-to-one with attention blocks"
    assert len(cast) == N_CAST
    n_tiles = batch * seq // ts
    x2 = x.reshape(batch * seq, d)
    smem = pl.BlockSpec(memory_space=pltpu.SMEM)
    hbm = pl.BlockSpec(memory_space=pl.ANY)
    cast_specs = [_cast_slice_spec(w, axis, n_tiles) for w, axis in cast]
    stage_slot = [
        pltpu.VMEM((ts, D_GMLP), jnp.float32),
        pltpu.VMEM((ts, D_GMLP), jnp.bfloat16),
        pltpu.VMEM((ts, D_ATTN), jnp.bfloat16),
        pltpu.VMEM((N_VARIANTS, BLOCK + ts, LANES), jnp.bfloat16),
        pltpu.VMEM((N_VARIANTS, BLOCK + ts, LANES), jnp.bfloat16),
    ]
    out, *cast_out = pl.pallas_call(
        functools.partial(_mixer_kernel, tiles_per_seq=seq // ts, n_tiles=n_tiles),
        grid=(n_tiles + 1,),
        in_specs=[
            pl.BlockSpec((ts, d), lambda t: (jnp.minimum(t, n_tiles - 1), 0)),
            pl.BlockSpec((ts, d), lambda t: (jnp.maximum(t - 1, 0), 0)),
            _resident(g1.shape), hbm, _resident(vgain.shape),
            _resident(wsp.shape), _resident(bsp.shape), smem, _resident(rel.shape),
            hbm,
        ] + cast_specs,
        out_specs=[pl.BlockSpec((ts, d), lambda t: (jnp.maximum(t - 1, 0), 0))] + cast_specs,
        out_shape=[jax.ShapeDtypeStruct(x2.shape, x2.dtype)]
        + [jax.ShapeDtypeStruct(w.shape, jnp.bfloat16) for w, _ in cast],
        scratch_shapes=stage_slot + stage_slot + [
            pltpu.VMEM((ts, D_MODEL), jnp.bfloat16),
            pltpu.VMEM((2, N_STACKS, 2, BLOCK, BLOCK), jnp.float32),
            pltpu.VMEM((n_blk * N_STACKS, 2 * BLOCK, BLOCK), jnp.float32),
            pltpu.VMEM((n_blk * N_STACKS, 2 * BLOCK, 2 * BLOCK), jnp.bfloat16),
            pltpu.VMEM(w_in.shape, jnp.bfloat16),
            pltpu.VMEM(w_out.shape, jnp.bfloat16),
            pltpu.VMEM((2, WEIGHT_STAGE_ROWS, w_in.shape[1]), jnp.float32),
            pltpu.VMEM(w_out.shape, jnp.float32),
            pltpu.VMEM((CHUNK, LANES), jnp.float32),
            pltpu.SemaphoreType.DMA((3,)),
        ],
        compiler_params=pltpu.CompilerParams(
            dimension_semantics=("arbitrary",),
            vmem_limit_bytes=VMEM_LIMIT_BYTES),
        name="mixer",
    )(x2, x2, g1, w_in, vgain, wsp, bsp, sinks, rel, w_out, *[w for w, _ in cast])
    return out.reshape(batch, seq, d), cast_out


def _ffn(h, p, g2, w1, w2, wproj, wgate, gf, *, final_norm):
    rows, d = h.shape
    tm = ROW_TILE
    n_tiles = rows // tm
    return pl.pallas_call(
        functools.partial(_ffn_kernel, final_norm=final_norm, n_tiles=n_tiles),
        grid=(n_tiles + 1,),
        in_specs=[
            pl.BlockSpec((tm, d), lambda t: (jnp.minimum(t, n_tiles - 1), 0)),
            pl.BlockSpec((tm, PLE_DIM), lambda t: (jnp.maximum(t - 1, 0), 0)),
            _resident(g2.shape), _resident(w1.shape), _resident(w2.shape),
            _resident(wproj.shape), _resident(wgate.shape), _resident(gf.shape),
        ],
        out_specs=pl.BlockSpec((tm, d), lambda t: (jnp.maximum(t - 1, 0), 0)),
        out_shape=jax.ShapeDtypeStruct(h.shape, h.dtype),
        scratch_shapes=[pltpu.VMEM((tm, d), jnp.float32)],
        compiler_params=pltpu.CompilerParams(
            dimension_semantics=("arbitrary",),
            vmem_limit_bytes=VMEM_LIMIT_BYTES),
        name="ffn",
    )(h, p, g2, w1, w2, wproj, wgate, gf)


def kernel(x, p, norm1_gain, w_in, gmlp_v_gain, w_spatial, b_spatial, attn_sinks, rel_bias_table, w_out, norm2_gain, w_ff1, w_ff2, w_ple_proj, w_ple_gate, final_gain):
    batch, seq, d = x.shape
    depth = w_in.shape[0]
    h = x
    for i in range(depth):
        h, (w1, w2, wproj, wgate) = _mixer(
            h, norm1_gain[i][None], w_in[i], gmlp_v_gain[i][None],
            w_spatial[i], b_spatial[i], attn_sinks[i], rel_bias_table.T, w_out[i],
            cast=[(w_ff1[i], 1), (w_ff2[i], 0), (w_ple_proj[i], 0), (w_ple_gate[i], 0)])
        h = _ffn(h.reshape(batch * seq, d), p[i].reshape(batch * seq, PLE_DIM),
                 norm2_gain[i][None], w1, w2, wproj, wgate,
                 final_gain[None], final_norm=(i == depth - 1)).reshape(batch, seq, d)
    return h
```

```python
import functools
import math

import jax
import jax.numpy as jnp
import numpy as np
from jax import lax
from jax.experimental import pallas as pl
from jax.experimental.pallas import tpu as pltpu

D_MODEL = 1024
PLE_DIM = 256
D_GMLP = 512
GMLP_GROUPS = 4
GROUP_DIM = D_GMLP // GMLP_GROUPS
CHUNK = 128
D_ATTN = 512
HEAD_DIM = 64
N_Q_HEADS = D_ATTN // HEAD_DIM
N_KV_HEADS = 2
Q_PER_KV = N_Q_HEADS // N_KV_HEADS
WINDOW = 128
BLOCK = WINDOW
REL_BUCKETS = 32
REL_MAX_DIST = 128
D_FF = 4 * D_MODEL
KV_W = N_KV_HEADS * HEAD_DIM
EPS = 1e-6
NEG_INF = -1e30
LOG2_E = math.log2(math.e)

LANES = 128
BF16_SUBLANES = 16
N_VARIANTS = 2 * N_KV_HEADS
N_STACKS = N_Q_HEADS // 2
SEQ_TILE = 512
ROW_TILE = 1024
FF_CHUNK = 1024
FFN_ROW_PART = 256
WEIGHT_STAGE_ROWS = 256
VMEM_LIMIT_BYTES = 56 * 1024 * 1024


def _bucket_thresholds():
    max_exact = REL_BUCKETS // 2
    n = np.arange(WINDOW)
    nf = np.maximum(n, 1).astype(np.float32)
    large = max_exact + (np.log(nf / max_exact) / np.float32(math.log(REL_MAX_DIST / max_exact))
                         * (REL_BUCKETS - max_exact)).astype(np.int32)
    bucket = np.where(n < max_exact, n, np.minimum(large, REL_BUCKETS - 1))
    assert (np.diff(bucket) >= 0).all()
    return [(b, int(np.argmax(bucket >= b))) for b in range(1, REL_BUCKETS) if (bucket >= b).any()]


_BUCKET_THRESHOLDS = _bucket_thresholds()


def _rms(x, gain):
    return x * lax.rsqrt(jnp.mean(x * x, axis=-1, keepdims=True) + EPS) * gain


def _gelu_tanh(x):
    c = float(np.float32(np.sqrt(2 / np.pi)))
    poly = x * x * (-2.0 * c * 0.044715 * LOG2_E) + (-2.0 * c * LOG2_E)
    return x / (1.0 + jnp.exp2(x * poly))


def _dot(a, b):
    return jnp.dot(a, b, preferred_element_type=jnp.float32)


def _dot_nt(a, b):
    return lax.dot_general(a, b, (((1,), (1,)), ((), ())), preferred_element_type=jnp.float32)


def _build_bias(rel_ref, bias_ref):
    a = lax.broadcasted_iota(jnp.int32, (BLOCK, BLOCK), 0)
    j = lax.broadcasted_iota(jnp.int32, (BLOCK, BLOCK), 1)
    from_prev = j > a
    n = jnp.where(from_prev, BLOCK + a - j, a - j)

    def entry(b, h):
        return jnp.broadcast_to(rel_ref[h:h + 1, b:b + 1], (BLOCK, BLOCK))

    for h in range(N_Q_HEADS):
        val = entry(0, h)
        for b, thr in _BUCKET_THRESHOLDS:
            val = jnp.where(n >= thr, entry(b, h), val)
        stack = 2 * (h // Q_PER_KV) + h % 2
        half = (h % Q_PER_KV) // 2
        val = val * LOG2_E
        bias_ref[0, stack, half] = val
        bias_ref[1, stack, half] = jnp.where(from_prev, NEG_INF, val)


def _build_spatial_bias(bsp_ref, bm_ref):
    row = lax.broadcasted_iota(jnp.int32, (CHUNK, CHUNK), 0)
    padded = jnp.zeros((CHUNK, CHUNK), jnp.float32)
    for g in range(GMLP_GROUPS):
        padded = jnp.where(row == g, jnp.broadcast_to(bsp_ref[g:g + 1, :], (CHUNK, CHUNK)), padded)
    bm_ref[...] = padded.T


def _load_as_bf16(src_hbm, dst_ref, stage_ref, sem_ref, sem_base, between=None):
    n_chunks = src_hbm.shape[0] // stage_ref.shape[1]
    rows = stage_ref.shape[1]

    def copy(k):
        return pltpu.make_async_copy(src_hbm.at[pl.ds(k * rows, rows), :],
                                     stage_ref.at[k % 2], sem_ref.at[sem_base + k % 2])

    copy(0).start()
    if between is not None:
        between()
    for k in range(n_chunks):
        if k + 1 < n_chunks:
            copy(k + 1).start()
        copy(k).wait()
        dst_ref[k * rows:(k + 1) * rows, :] = stage_ref[k % 2].astype(dst_ref.dtype)


_VARIANT = ((0, 1), (2, 3))


def _project_pieces(x_ref, g1_ref, w_in_ref, vgain_ref, dst):
    u_ref, vn_ref, q_ref, kbuf, vbuf = dst
    ts = x_ref.shape[0]
    o = 2 * D_GMLP
    state = {}

    def norm():
        state["hn"] = _rms(x_ref[...], g1_ref[...]).astype(jnp.bfloat16)

    def gate_u():
        u_ref[...] = _gelu_tanh(_dot(state["hn"], w_in_ref[:, 0:D_GMLP]))

    def gate_v():
        vg = _gelu_tanh(_dot(state["hn"], w_in_ref[:, D_GMLP:2 * D_GMLP]))
        for g in range(GMLP_GROUPS):
            cols = slice(g * GROUP_DIM, (g + 1) * GROUP_DIM)
            vn_ref[:, cols] = _rms(vg[:, cols], vgain_ref[:, cols]).astype(vn_ref.dtype)

    def query():
        q = _dot(state["hn"], w_in_ref[:, o:o + D_ATTN]) * (HEAD_DIM ** -0.5 * LOG2_E)
        q_ref[...] = q.astype(q_ref.dtype)

    def key_value():
        kv = _dot(state["hn"], w_in_ref[:, o + D_ATTN:o + D_ATTN + 2 * KV_W])
        lo = lax.broadcasted_iota(jnp.int32, (ts, LANES), 1) < HEAD_DIM
        for buf, t in ((kbuf, kv[:, 0:KV_W]), (vbuf, kv[:, KV_W:2 * KV_W])):
            tr = pltpu.roll(t, HEAD_DIM, axis=1)
            buf[0, BLOCK:BLOCK + ts, :] = jnp.where(lo, t, 0.0).astype(buf.dtype)
            buf[1, BLOCK:BLOCK + ts, :] = jnp.where(lo, 0.0, tr).astype(buf.dtype)
            buf[2, BLOCK:BLOCK + ts, :] = jnp.where(lo, tr, 0.0).astype(buf.dtype)
            buf[3, BLOCK:BLOCK + ts, :] = jnp.where(lo, 0.0, t).astype(buf.dtype)

    return [norm, gate_u, gate_v, query, key_value]


def _attend_pieces(i, src, seq_start, sink_ref, bias_ref, mix_ref, s_scr, p_scr):
    _, _, q_ref, kbuf, vbuf = src
    rows = slice(i * BLOCK, (i + 1) * BLOCK)
    keys = slice(i * BLOCK, (i + 2) * BLOCK)
    first = jnp.where(seq_start, 1, 0) if i == 0 else 0
    upper = lax.broadcasted_iota(jnp.int32, (2 * BLOCK, 1), 0) < BLOCK
    query = lax.broadcasted_iota(jnp.int32, (2 * BLOCK, BLOCK), 0) % BLOCK
    from_prev = lax.broadcasted_iota(jnp.int32, (2 * BLOCK, BLOCK), 1) > query
    stacks = [(g, parity) for g in range(N_KV_HEADS) for parity in range(2)]
    state = {}

    def scores():
        for g, parity in stacks:
            qs = jnp.concatenate([q_ref[rows, 2 * g * LANES:(2 * g + 1) * LANES],
                                  q_ref[rows, (2 * g + 1) * LANES:(2 * g + 2) * LANES]], axis=0)
            s = _dot_nt(qs, kbuf[_VARIANT[g][parity], keys, :])
            s = jnp.where(from_prev, s[:, 0:BLOCK], s[:, BLOCK:2 * BLOCK])
            s_scr[N_STACKS * i + 2 * g + parity] = (
                s + bias_ref[first, 2 * g + parity].reshape(2 * BLOCK, BLOCK))

    def softmax():
        prev_one = jnp.where(from_prev, 1.0, 0.0).astype(jnp.bfloat16)
        for g, parity in stacks:
            head = Q_PER_KV * g + parity
            sink = jnp.where(upper, sink_ref[head], sink_ref[head + 2]) * LOG2_E
            s = s_scr[N_STACKS * i + 2 * g + parity]
            m = jnp.maximum(jnp.max(s, axis=-1, keepdims=True), sink)
            e = jnp.exp2(s - m)
            denom = jnp.sum(e, axis=-1, keepdims=True) + jnp.exp2(sink - m)
            p = e.astype(jnp.bfloat16)
            p_prev = p * prev_one
            p_scr[N_STACKS * i + 2 * g + parity] = jnp.concatenate([p_prev, p - p_prev], axis=1)
            state["inv", g, parity] = 1.0 / denom

    def values():
        low = lax.broadcasted_iota(jnp.int32, (2 * BLOCK, LANES), 1) < HEAD_DIM
        for g in range(N_KV_HEADS):
            pair0 = D_GMLP + 2 * g * LANES
            acc = (_dot(p_scr[N_STACKS * i + 2 * g], vbuf[_VARIANT[g][0], keys, :])
                   + _dot(p_scr[N_STACKS * i + 2 * g + 1], vbuf[_VARIANT[g][1], keys, :]))
            acc = acc * jnp.where(low, state["inv", g, 0], state["inv", g, 1])
            mix_ref[rows, pair0:pair0 + LANES] = acc[0:BLOCK].astype(mix_ref.dtype)
            mix_ref[rows, pair0 + LANES:pair0 + 2 * LANES] = acc[BLOCK:].astype(mix_ref.dtype)

    return scores, softmax, values


def _spatial_gate(src, wsp_ref, bm_ref, mix_ref):
    u_ref, vn_ref = src[0:2]
    t_idx = lax.broadcasted_iota(jnp.int32, (CHUNK, CHUNK), 0)
    s_idx = lax.broadcasted_iota(jnp.int32, (CHUNK, CHUNK), 1)
    causal = s_idx <= t_idx
    for g in range(GMLP_GROUPS):
        cols = slice(g * GROUP_DIM, (g + 1) * GROUP_DIM)
        w_g = jnp.where(causal, wsp_ref[g], 0.0).astype(jnp.bfloat16)
        b_g = bm_ref[:, g:g + 1]
        for c in range(u_ref.shape[0] // CHUNK):
            rows = slice(c * CHUNK, (c + 1) * CHUNK)
            sv = _dot(w_g, vn_ref[rows, cols]) + b_g
            mix_ref[rows, cols] = (u_ref[rows, cols] * sv).astype(mix_ref.dtype)


N_CAST = 4
SCORES_AHEAD = 2


def _mixer_kernel(x_ref, xprev_ref, g1_ref, w_in_hbm, vgain_ref, wsp_ref, bsp_ref, sink_ref,
                  rel_ref, w_out_hbm, *rest, tiles_per_seq, n_tiles):
    cast_src, o_ref = rest[0:N_CAST], rest[N_CAST]
    cast_dst, scratch = rest[N_CAST + 1:2 * N_CAST + 1], rest[2 * N_CAST + 1:]
    slots = (scratch[0:5], scratch[5:10])
    mix_ref, bias_ref, s_scr, p_scr = scratch[10:14]
    w_in_ref, w_out_ref, stage_in, stage_out, bm_ref, dma_sem = scratch[14:20]
    ts = x_ref.shape[0]
    t = pl.program_id(0)

    cur_starts = t % tiles_per_seq == 0
    prev_started = (t + tiles_per_seq - 1) % tiles_per_seq == 0
    n_blk = ts // BLOCK

    def step(dst, src):
        project = _project_pieces(x_ref, g1_ref, w_in_ref, vgain_ref, dst) if dst else []
        attend = [_attend_pieces(i, src, prev_started, sink_ref, bias_ref, mix_ref, s_scr,
                                 p_scr) for i in range(n_blk)] if src else []
        if src:
            for i in range(SCORES_AHEAD):
                attend[i][0]()
            _spatial_gate(src, wsp_ref, bm_ref, mix_ref)
        if dst:
            project.pop(0)()
        for i in range(n_blk):
            if src:
                if i + SCORES_AHEAD < n_blk:
                    attend[i + SCORES_AHEAD][0]()
                attend[i][1]()
            if dst:
                project.pop(0)()
            if src:
                attend[i][2]()
        assert not project
        if src:
            o_ref[...] = xprev_ref[...] + _dot(mix_ref[...], w_out_ref[...])
        for s, d in zip(cast_src, cast_dst):
            d[...] = s[...].astype(d.dtype)
        if dst:
            zeros = jnp.zeros((N_VARIANTS, BLOCK, LANES), dst[3].dtype)
            for k in (3, 4):
                dst[k][:, 0:BLOCK, :] = (
                    jnp.where(cur_starts, zeros, src[k][:, ts:ts + BLOCK, :]) if src else zeros)

    @pl.when(t == 0)
    def _():
        def build_tables():
            _build_bias(rel_ref, bias_ref)
            _build_spatial_bias(bsp_ref, bm_ref)

        fetch_w_out = pltpu.make_async_copy(w_out_hbm, stage_out, dma_sem.at[2])
        fetch_w_out.start()
        _load_as_bf16(w_in_hbm, w_in_ref, stage_in, dma_sem, 0, between=build_tables)
        step(slots[0], None)
        fetch_w_out.wait()
        w_out_ref[...] = stage_out[...].astype(w_out_ref.dtype)

    @pl.when(t == n_tiles)
    def _():
        step(None, slots[(n_tiles - 1) % 2])

    @pl.when((t > 0) & (t < n_tiles) & (t % 2 == 0))
    def _():
        step(slots[0], slots[1])

    @pl.when((t < n_tiles) & (t % 2 == 1))
    def _():
        step(slots[1], slots[0])


def _ffn_kernel(h_ref, p_ref, g2_ref, w1_ref, w2_ref, wproj_ref, wgate_ref, gf_ref, o_ref,
                acc_ref, *, final_norm, n_tiles):
    tm = h_ref.shape[0]
    parts = [slice(r, r + FFN_ROW_PART) for r in range(0, tm, FFN_ROW_PART)]

    def step(expand, finish):
        hidden = []

        def expand_chunk(c, hn_parts):
            cols = slice(c * FF_CHUNK, (c + 1) * FF_CHUNK)
            if c == 0:
                t = jnp.concatenate([_dot(part, w1_ref[:, cols]) for part in hn_parts], axis=0)
            else:
                t = _dot(jnp.concatenate(hn_parts, axis=0), w1_ref[:, cols])
            t = jnp.maximum(t, 0.0)
            hidden.append((t * t).astype(jnp.bfloat16))

        n_chunks = D_FF // FF_CHUNK
        assert len(parts) == n_chunks, "one finishing row part is issued per expand chunk"
        if expand:
            hn_parts = [_rms(h_ref[r, :], g2_ref[...]).astype(jnp.bfloat16) for r in parts]
        if finish:
            ple = _dot(p_ref[...].astype(jnp.bfloat16), wproj_ref[...])
        for c in range(n_chunks):
            if expand:
                expand_chunk(c, hn_parts)
            if finish:
                r = parts[c]
                z = _dot(acc_ref[r, :].astype(jnp.bfloat16), wgate_ref[...])
                out = acc_ref[r, :] + ple[r] / (1.0 + jnp.exp2(z * (-LOG2_E)))
                o_ref[r, :] = _rms(out, gf_ref[...]) if final_norm else out
        if expand:
            acc_ref[...] = h_ref[...] + _dot(jnp.concatenate(hidden, axis=1), w2_ref[...])

    t = pl.program_id(0)

    @pl.when(t == 0)
    def _():
        acc_ref[...] = jnp.zeros(acc_ref.shape, acc_ref.dtype)

    @pl.when(t < n_tiles)
    def _():
        step(True, True)

    @pl.when(t == n_tiles)
    def _():
        step(False, True)


def _resident(shape):
    return pl.BlockSpec(shape, lambda *_: (0,) * len(shape), pipeline_mode=pl.Buffered(1))


def _cast_slice_spec(arr, axis, n_tiles):
    size = arr.shape[axis]
    min_thick = BF16_SUBLANES if axis == 0 else LANES
    thick = max(size // n_tiles, min_thick)
    assert size % thick == 0
    last = size // thick - 1
    block = tuple(thick if a == axis else n for a, n in enumerate(arr.shape))
    return pl.BlockSpec(block, lambda t: tuple(jnp.minimum(t, last) if a == axis else 0
                                               for a in range(arr.ndim)))


def _mixer(x, g1, w_in, vgain, wsp, bsp, sinks, rel, w_out, cast):
    batch, seq, d = x.shape
    ts = SEQ_TILE
    n_blk = ts // BLOCK
    assert n_blk == 4, "stage-1 pieces are paired one-to-one with attention blocks"
    assert len(cast) == N_CAST
    n_tiles = batch * seq // ts
    x2 = x.reshape(batch * seq, d)
    smem = pl.BlockSpec(memory_space=pltpu.SMEM)
    hbm = pl.BlockSpec(memory_space=pl.ANY)
    cast_specs = [_cast_slice_spec(w, axis, n_tiles) for w, axis in cast]
    stage_slot = [
        pltpu.VMEM((ts, D_GMLP), jnp.float32),
        pltpu.VMEM((ts, D_GMLP), jnp.bfloat16),
        pltpu.VMEM((ts, D_ATTN), jnp.bfloat16),
        pltpu.VMEM((N_VARIANTS, BLOCK + ts, LANES), jnp.bfloat16),
        pltpu.VMEM((N_VARIANTS, BLOCK + ts, LANES), jnp.bfloat16),
    ]
    out, *cast_out = pl.pallas_call(
        functools.partial(_mixer_kernel, tiles_per_seq=seq // ts, n_tiles=n_tiles),
        grid=(n_tiles + 1,),
        in_specs=[
            pl.BlockSpec((ts, d), lambda t: (jnp.minimum(t, n_tiles - 1), 0)),
            pl.BlockSpec((ts, d), lambda t: (jnp.maximum(t - 1, 0), 0)),
            _resident(g1.shape), hbm, _resident(vgain.shape),
            _resident(wsp.shape), _resident(bsp.shape), smem, _resident(rel.shape),
            hbm,
        ] + cast_specs,
        out_specs=[pl.BlockSpec((ts, d), lambda t: (jnp.maximum(t - 1, 0), 0))] + cast_specs,
        out_shape=[jax.ShapeDtypeStruct(x2.shape, x2.dtype)]
        + [jax.ShapeDtypeStruct(w.shape, jnp.bfloat16) for w, _ in cast],
        scratch_shapes=stage_slot + stage_slot + [
            pltpu.VMEM((ts, D_MODEL), jnp.bfloat16),
            pltpu.VMEM((2, 4, 2, BLOCK, BLOCK), jnp.float32),
            pltpu.VMEM((n_blk * N_STACKS, 2 * BLOCK, BLOCK), jnp.float32),
            pltpu.VMEM((n_blk * N_STACKS, 2 * BLOCK, 2 * BLOCK), jnp.bfloat16),
            pltpu.VMEM(w_in.shape, jnp.bfloat16),
            pltpu.VMEM(w_out.shape, jnp.bfloat16),
            pltpu.VMEM((2, WEIGHT_STAGE_ROWS, w_in.shape[1]), jnp.float32),
            pltpu.VMEM(w_out.shape, jnp.float32),
            pltpu.VMEM((CHUNK, LANES), jnp.float32),
            pltpu.SemaphoreType.DMA((3,)),
        ],
        compiler_params=pltpu.CompilerParams(
            dimension_semantics=("arbitrary",),
            vmem_limit_bytes=VMEM_LIMIT_BYTES),
        name="mixer",
    )(x2, x2, g1, w_in, vgain, wsp, bsp, sinks, rel, w_out, *[w for w, _ in cast])
    return out.reshape(batch, seq, d), cast_out


def _ffn(h, p, g2, w1, w2, wproj, wgate, gf, *, final_norm):
    rows, d = h.shape
    tm = ROW_TILE
    n_tiles = rows // tm
    return pl.pallas_call(
        functools.partial(_ffn_kernel, final_norm=final_norm, n_tiles=n_tiles),
        grid=(n_tiles + 1,),
        in_specs=[
            pl.BlockSpec((tm, d), lambda t: (jnp.minimum(t, n_tiles - 1), 0)),
            pl.BlockSpec((tm, PLE_DIM), lambda t: (jnp.maximum(t - 1, 0), 0)),
            _resident(g2.shape), _resident(w1.shape), _resident(w2.shape),
            _resident(wproj.shape), _resident(wgate.shape), _resident(gf.shape),
        ],
        out_specs=pl.BlockSpec((tm, d), lambda t: (jnp.maximum(t - 1, 0), 0)),
        out_shape=jax.ShapeDtypeStruct(h.shape, h.dtype),
        scratch_shapes=[pltpu.VMEM((tm, d), jnp.float32)],
        compiler_params=pltpu.CompilerParams(
            dimension_semantics=("arbitrary",),
            vmem_limit_bytes=VMEM_LIMIT_BYTES),
        name="ffn",
    )(h, p, g2, w1, w2, wproj, wgate, gf)


def kernel(x, p, norm1_gain, w_in, gmlp_v_gain, w_spatial, b_spatial, attn_sinks, rel_bias_table, w_out, norm2_gain, w_ff1, w_ff2, w_ple_proj, w_ple_gate, final_gain):
    batch, seq, d = x.shape
    depth = w_in.shape[0]
    bf16 = jnp.bfloat16
    h = x
    for i in range(depth):
        h, (w1, w2, wproj, wgate) = _mixer(
            h, norm1_gain[i][None], w_in[i], gmlp_v_gain[i][None],
            w_spatial[i], b_spatial[i], attn_sinks[i], rel_bias_table.T, w_out[i],
            cast=[(w_ff1[i], 1), (w_ff2[i], 0), (w_ple_proj[i], 0), (w_ple_gate[i], 0)])
        h = _ffn(h.reshape(batch * seq, d), p[i].reshape(batch * seq, PLE_DIM),
                 norm2_gain[i][None], w1, w2, wproj, wgate,
                 final_gain[None], final_norm=(i == depth - 1)).reshape(batch, seq, d)
    return h
```

```python
import functools
import math

import jax
import jax.numpy as jnp
import numpy as np
from jax import lax
from jax.experimental import pallas as pl
from jax.experimental.pallas import tpu as pltpu

D_MODEL = 1024
PLE_DIM = 256
D_GMLP = 512
GMLP_GROUPS = 4
GROUP_DIM = D_GMLP // GMLP_GROUPS
CHUNK = 128
D_ATTN = 512
HEAD_DIM = 64
N_Q_HEADS = D_ATTN // HEAD_DIM
N_KV_HEADS = 2
Q_PER_KV = N_Q_HEADS // N_KV_HEADS
WINDOW = 128
BLOCK = WINDOW
REL_BUCKETS = 32
REL_MAX_DIST = 128
D_FF = 4 * D_MODEL
KV_W = N_KV_HEADS * HEAD_DIM
EPS = 1e-6
NEG_INF = -1e30
LOG2_E = math.log2(math.e)

LANES = 128
BF16_SUBLANES = 16
N_VARIANTS = 2 * N_KV_HEADS
N_STACKS = N_Q_HEADS // 2
SEQ_TILE = 512
ROW_TILE = 1024
FF_CHUNK = 1024
FFN_ROW_PART = 256
WEIGHT_STAGE_ROWS = 256
VMEM_LIMIT_BYTES = 56 * 1024 * 1024


def _bucket_thresholds():
    max_exact = REL_BUCKETS // 2
    n = np.arange(WINDOW)
    nf = np.maximum(n, 1).astype(np.float32)
    large = max_exact + (np.log(nf / max_exact) / np.float32(math.log(REL_MAX_DIST / max_exact))
                         * (REL_BUCKETS - max_exact)).astype(np.int32)
    bucket = np.where(n < max_exact, n, np.minimum(large, REL_BUCKETS - 1))
    assert (np.diff(bucket) >= 0).all()
    return [(b, int(np.argmax(bucket >= b))) for b in range(1, REL_BUCKETS) if (bucket >= b).any()]


_BUCKET_THRESHOLDS = _bucket_thresholds()


def _rms(x, gain):
    return x * lax.rsqrt(jnp.mean(x * x, axis=-1, keepdims=True) + EPS) * gain


def _gelu_tanh(x):
    c = float(np.float32(np.sqrt(2 / np.pi)))
    poly = x * x * (-2.0 * c * 0.044715 * LOG2_E) + (-2.0 * c * LOG2_E)
    return x / (1.0 + jnp.exp2(x * poly))


def _dot(a, b):
    return jnp.dot(a, b, preferred_element_type=jnp.float32)


def _dot_nt(a, b):
    return lax.dot_general(a, b, (((1,), (1,)), ((), ())), preferred_element_type=jnp.float32)


def _build_bias(rel_ref, bias_ref):
    a = lax.broadcasted_iota(jnp.int32, (BLOCK, BLOCK), 0)
    j = lax.broadcasted_iota(jnp.int32, (BLOCK, BLOCK), 1)
    from_prev = j > a
    n = jnp.where(from_prev, BLOCK + a - j, a - j)

    def entry(b, h):
        return jnp.broadcast_to(rel_ref[h:h + 1, b:b + 1], (BLOCK, BLOCK))

    for h in range(N_Q_HEADS):
        val = entry(0, h)
        for b, thr in _BUCKET_THRESHOLDS:
            val = jnp.where(n >= thr, entry(b, h), val)
        stack = 2 * (h // Q_PER_KV) + h % 2
        half = (h % Q_PER_KV) // 2
        val = val * LOG2_E
        bias_ref[0, stack, half] = val
        bias_ref[1, stack, half] = jnp.where(from_prev, NEG_INF, val)


def _build_spatial_bias(bsp_ref, bm_ref):
    row = lax.broadcasted_iota(jnp.int32, (CHUNK, CHUNK), 0)
    padded = jnp.zeros((CHUNK, CHUNK), jnp.float32)
    for g in range(GMLP_GROUPS):
        padded = jnp.where(row == g, jnp.broadcast_to(bsp_ref[g:g + 1, :], (CHUNK, CHUNK)), padded)
    bm_ref[...] = padded.T


def _load_as_bf16(src_hbm, dst_ref, stage_ref, sem_ref, sem_base, between=None):
    n_chunks = src_hbm.shape[0] // stage_ref.shape[1]
    rows = stage_ref.shape[1]

    def copy(k):
        return pltpu.make_async_copy(src_hbm.at[pl.ds(k * rows, rows), :],
                                     stage_ref.at[k % 2], sem_ref.at[sem_base + k % 2])

    copy(0).start()
    if between is not None:
        between()
    for k in range(n_chunks):
        if k + 1 < n_chunks:
            copy(k + 1).start()
        copy(k).wait()
        dst_ref[k * rows:(k + 1) * rows, :] = stage_ref[k % 2].astype(dst_ref.dtype)


_VARIANT = ((0, 1), (2, 3))


def _norm_tile(x_ref, g1_ref, hn_ref):
    hn_ref[...] = _rms(x_ref[...], g1_ref[...]).astype(hn_ref.dtype)


def _project_pieces(w_in_ref, vgain_ref, dst):
    u_ref, vn_ref, q_ref, kbuf, vbuf, hn_ref = dst
    ts = hn_ref.shape[0]
    o = 2 * D_GMLP

    def gate_u():
        u_ref[...] = _gelu_tanh(_dot(hn_ref[...], w_in_ref[:, 0:D_GMLP]))

    def gate_v():
        vg = _gelu_tanh(_dot(hn_ref[...], w_in_ref[:, D_GMLP:2 * D_GMLP]))
        for g in range(GMLP_GROUPS):
            cols = slice(g * GROUP_DIM, (g + 1) * GROUP_DIM)
            vn_ref[:, cols] = _rms(vg[:, cols], vgain_ref[:, cols]).astype(vn_ref.dtype)

    def query():
        q = _dot(hn_ref[...], w_in_ref[:, o:o + D_ATTN]) * (HEAD_DIM ** -0.5 * LOG2_E)
        q_ref[...] = q.astype(q_ref.dtype)

    def key_value():
        kv = _dot(hn_ref[...], w_in_ref[:, o + D_ATTN:o + D_ATTN + 2 * KV_W])
        lo = lax.broadcasted_iota(jnp.int32, (ts, LANES), 1) < HEAD_DIM
        for buf, t in ((kbuf, kv[:, 0:KV_W]), (vbuf, kv[:, KV_W:2 * KV_W])):
            tr = pltpu.roll(t, HEAD_DIM, axis=1)
            buf[0, BLOCK:BLOCK + ts, :] = jnp.where(lo, t, 0.0).astype(buf.dtype)
            buf[1, BLOCK:BLOCK + ts, :] = jnp.where(lo, 0.0, tr).astype(buf.dtype)
            buf[2, BLOCK:BLOCK + ts, :] = jnp.where(lo, tr, 0.0).astype(buf.dtype)
            buf[3, BLOCK:BLOCK + ts, :] = jnp.where(lo, 0.0, t).astype(buf.dtype)

    return [gate_u, gate_v, query, key_value]


def _attend_pieces(i, src, seq_start, sink_ref, bias_ref, mix_ref, s_scr, p_scr):
    q_ref, kbuf, vbuf = src[2:5]
    rows = slice(i * BLOCK, (i + 1) * BLOCK)
    keys = slice(i * BLOCK, (i + 2) * BLOCK)
    first = jnp.where(seq_start, 1, 0) if i == 0 else 0
    upper = lax.broadcasted_iota(jnp.int32, (2 * BLOCK, 1), 0) < BLOCK
    query = lax.broadcasted_iota(jnp.int32, (2 * BLOCK, BLOCK), 0) % BLOCK
    from_prev = lax.broadcasted_iota(jnp.int32, (2 * BLOCK, BLOCK), 1) > query
    stacks = [(g, parity) for g in range(N_KV_HEADS) for parity in range(2)]
    state = {}

    def scores():
        for g, parity in stacks:
            qs = jnp.concatenate([q_ref[rows, 2 * g * LANES:(2 * g + 1) * LANES],
                                  q_ref[rows, (2 * g + 1) * LANES:(2 * g + 2) * LANES]], axis=0)
            s = _dot_nt(qs, kbuf[_VARIANT[g][parity], keys, :])
            s = jnp.where(from_prev, s[:, 0:BLOCK], s[:, BLOCK:2 * BLOCK])
            s_scr[N_STACKS * i + 2 * g + parity] = (
                s + bias_ref[first, 2 * g + parity].reshape(2 * BLOCK, BLOCK))

    def softmax():
        zero = jnp.zeros((2 * BLOCK, BLOCK), jnp.bfloat16)
        for g, parity in stacks:
            head = Q_PER_KV * g + parity
            sink = jnp.where(upper, sink_ref[head], sink_ref[head + 2]) * LOG2_E
            s = s_scr[N_STACKS * i + 2 * g + parity]
            m = jnp.maximum(jnp.max(s, axis=-1, keepdims=True), sink)
            e = jnp.exp2(s - m)
            denom = jnp.sum(e, axis=-1, keepdims=True) + jnp.exp2(sink - m)
            p = e.astype(jnp.bfloat16)
            p_scr[N_STACKS * i + 2 * g + parity] = jnp.concatenate(
                [jnp.where(from_prev, p, zero), jnp.where(from_prev, zero, p)], axis=1)
            state["inv", g, parity] = 1.0 / denom

    def values():
        low = lax.broadcasted_iota(jnp.int32, (2 * BLOCK, LANES), 1) < HEAD_DIM
        for g in range(N_KV_HEADS):
            pair0 = D_GMLP + 2 * g * LANES
            acc = (_dot(p_scr[N_STACKS * i + 2 * g], vbuf[_VARIANT[g][0], keys, :])
                   + _dot(p_scr[N_STACKS * i + 2 * g + 1], vbuf[_VARIANT[g][1], keys, :]))
            acc = acc * jnp.where(low, state["inv", g, 0], state["inv", g, 1])
            mix_ref[rows, pair0:pair0 + LANES] = acc[0:BLOCK].astype(mix_ref.dtype)
            mix_ref[rows, pair0 + LANES:pair0 + 2 * LANES] = acc[BLOCK:].astype(mix_ref.dtype)

    return scores, softmax, values


def _spatial_gate(src, wsp_ref, bm_ref, mix_ref):
    u_ref, vn_ref = src[0:2]
    t_idx = lax.broadcasted_iota(jnp.int32, (CHUNK, CHUNK), 0)
    s_idx = lax.broadcasted_iota(jnp.int32, (CHUNK, CHUNK), 1)
    causal = s_idx <= t_idx
    for g in range(GMLP_GROUPS):
        cols = slice(g * GROUP_DIM, (g + 1) * GROUP_DIM)
        w_g = jnp.where(causal, wsp_ref[g], 0.0).astype(jnp.bfloat16)
        b_g = bm_ref[:, g:g + 1]
        for c in range(u_ref.shape[0] // CHUNK):
            rows = slice(c * CHUNK, (c + 1) * CHUNK)
            sv = _dot(w_g, vn_ref[rows, cols]) + b_g
            mix_ref[rows, cols] = (u_ref[rows, cols] * sv).astype(mix_ref.dtype)


N_CAST = 4
SCORES_AHEAD = 2
NORM_WITH_BLOCK = 2


def _mixer_kernel(x_ref, xprev_ref, xnext_ref, g1_ref, w_in_hbm, vgain_ref, wsp_ref, bsp_ref,
                  sink_ref, rel_ref, w_out_hbm, *rest, tiles_per_seq, n_tiles):
    cast_src, o_ref = rest[0:N_CAST], rest[N_CAST]
    cast_dst, scratch = rest[N_CAST + 1:2 * N_CAST + 1], rest[2 * N_CAST + 1:]
    slots = (scratch[0:6], scratch[6:12])
    mix_ref, bias_ref, s_scr, p_scr = scratch[12:16]
    w_in_ref, w_out_ref, stage_in, stage_out, bm_ref, dma_sem = scratch[16:22]
    ts = x_ref.shape[0]
    t = pl.program_id(0)

    cur_starts = t % tiles_per_seq == 0
    prev_started = (t + tiles_per_seq - 1) % tiles_per_seq == 0
    n_blk = ts // BLOCK

    def step(dst, src, hn_next):
        project = _project_pieces(w_in_ref, vgain_ref, dst) if dst else []
        attend = [_attend_pieces(i, src, prev_started, sink_ref, bias_ref, mix_ref, s_scr,
                                 p_scr) for i in range(n_blk)] if src else []
        if src:
            for i in range(SCORES_AHEAD):
                attend[i][0]()
            _spatial_gate(src, wsp_ref, bm_ref, mix_ref)
        for i in range(n_blk):
            if src:
                if i + SCORES_AHEAD < n_blk:
                    attend[i + SCORES_AHEAD][0]()
                attend[i][1]()
            if dst:
                project.pop(0)()
            if hn_next is not None and i == NORM_WITH_BLOCK:
                _norm_tile(xnext_ref, g1_ref, hn_next)
            if src:
                attend[i][2]()
        assert not project
        if src:
            o_ref[...] = xprev_ref[...] + _dot(mix_ref[...], w_out_ref[...])
        for s, d in zip(cast_src, cast_dst):
            d[...] = s[...].astype(d.dtype)
        if dst:
            zeros = jnp.zeros((N_VARIANTS, BLOCK, LANES), dst[3].dtype)
            for k in (3, 4):
                dst[k][:, 0:BLOCK, :] = (
                    jnp.where(cur_starts, zeros, src[k][:, ts:ts + BLOCK, :]) if src else zeros)

    @pl.when(t == 0)
    def _():
        def build_tables():
            _build_bias(rel_ref, bias_ref)
            _build_spatial_bias(bsp_ref, bm_ref)

        fetch_w_out = pltpu.make_async_copy(w_out_hbm, stage_out, dma_sem.at[2])
        fetch_w_out.start()
        _norm_tile(x_ref, g1_ref, slots[0][5])
        _load_as_bf16(w_in_hbm, w_in_ref, stage_in, dma_sem, 0, between=build_tables)
        step(slots[0], None, slots[1][5])
        fetch_w_out.wait()
        w_out_ref[...] = stage_out[...].astype(w_out_ref.dtype)

    @pl.when(t == n_tiles)
    def _():
        step(None, slots[(n_tiles - 1) % 2], None)

    @pl.when((t > 0) & (t < n_tiles) & (t % 2 == 0))
    def _():
        step(slots[0], slots[1], slots[1][5])

    @pl.when((t < n_tiles) & (t % 2 == 1))
    def _():
        step(slots[1], slots[0], slots[0][5])


def _ffn_kernel(h_ref, p_ref, g2_ref, w1_ref, w2_ref, wproj_ref, wgate_ref, gf_ref, o_ref,
                acc_ref, *, final_norm, n_tiles):
    tm = h_ref.shape[0]
    parts = [slice(r, r + FFN_ROW_PART) for r in range(0, tm, FFN_ROW_PART)]

    def step(expand, finish):
        hidden = []

        def expand_chunk(c, hn_parts):
            cols = slice(c * FF_CHUNK, (c + 1) * FF_CHUNK)
            if c == 0:
                t = jnp.concatenate([_dot(part, w1_ref[:, cols]) for part in hn_parts], axis=0)
            else:
                t = _dot(jnp.concatenate(hn_parts, axis=0), w1_ref[:, cols])
            t = jnp.maximum(t, 0.0)
            hidden.append((t * t).astype(jnp.bfloat16))

        n_chunks = D_FF // FF_CHUNK
        assert len(parts) == n_chunks, "one finishing row part is issued per expand chunk"
        if expand:
            hn_parts = [_rms(h_ref[r, :], g2_ref[...]).astype(jnp.bfloat16) for r in parts]
        if finish:
            ple = _dot(p_ref[...].astype(jnp.bfloat16), wproj_ref[...])
        for c in range(n_chunks):
            if expand:
                expand_chunk(c, hn_parts)
            if finish:
                r = parts[c]
                z = _dot(acc_ref[r, :].astype(jnp.bfloat16), wgate_ref[...])
                out = acc_ref[r, :] + ple[r] / (1.0 + jnp.exp2(z * (-LOG2_E)))
                o_ref[r, :] = _rms(out, gf_ref[...]) if final_norm else out
        if expand:
            acc_ref[...] = h_ref[...] + _dot(jnp.concatenate(hidden, axis=1), w2_ref[...])

    t = pl.program_id(0)

    @pl.when(t == 0)
    def _():
        acc_ref[...] = jnp.zeros(acc_ref.shape, acc_ref.dtype)

    @pl.when(t < n_tiles)
    def _():
        step(True, True)

    @pl.when(t == n_tiles)
    def _():
        step(False, True)


def _resident(shape):
    return pl.BlockSpec(shape, lambda *_: (0,) * len(shape), pipeline_mode=pl.Buffered(1))


def _cast_slice_spec(arr, axis, n_tiles):
    size = arr.shape[axis]
    min_thick = BF16_SUBLANES if axis == 0 else LANES
    thick = max(size // n_tiles, min_thick)
    assert size % thick == 0
    last = size // thick - 1
    block = tuple(thick if a == axis else n for a, n in enumerate(arr.shape))
    return pl.BlockSpec(block, lambda t: tuple(jnp.minimum(t, last) if a == axis else 0
                                               for a in range(arr.ndim)))


def _mixer(x, g1, w_in, vgain, wsp, bsp, sinks, rel, w_out, cast):
    batch, seq, d = x.shape
    ts = SEQ_TILE
    n_blk = ts // BLOCK
    assert n_blk == 4, "stage-1 pieces are paired one-to-one with attention blocks"
    assert len(cast) == N_CAST
    n_tiles = batch * seq // ts
    x2 = x.reshape(batch * seq, d)
    smem = pl.BlockSpec(memory_space=pltpu.SMEM)
    hbm = pl.BlockSpec(memory_space=pl.ANY)
    cast_specs = [_cast_slice_spec(w, axis, n_tiles) for w, axis in cast]
    stage_slot = [
        pltpu.VMEM((ts, D_GMLP), jnp.float32),
        pltpu.VMEM((ts, D_GMLP), jnp.bfloat16),
        pltpu.VMEM((ts, D_ATTN), jnp.bfloat16),
        pltpu.VMEM((N_VARIANTS, BLOCK + ts, LANES), jnp.bfloat16),
        pltpu.VMEM((N_VARIANTS, BLOCK + ts, LANES), jnp.bfloat16),
        pltpu.VMEM((ts, d), jnp.bfloat16),
    ]
    out, *cast_out = pl.pallas_call(
        functools.partial(_mixer_kernel, tiles_per_seq=seq // ts, n_tiles=n_tiles),
        grid=(n_tiles + 1,),
        in_specs=[
            pl.BlockSpec((ts, d), lambda t: (jnp.minimum(t, n_tiles - 1), 0)),
            pl.BlockSpec((ts, d), lambda t: (jnp.maximum(t - 1, 0), 0)),
            pl.BlockSpec((ts, d), lambda t: (jnp.minimum(t + 1, n_tiles - 1), 0)),
            _resident(g1.shape), hbm, _resident(vgain.shape),
            _resident(wsp.shape), _resident(bsp.shape), smem, _resident(rel.shape),
            hbm,
        ] + cast_specs,
        out_specs=[pl.BlockSpec((ts, d), lambda t: (jnp.maximum(t - 1, 0), 0))] + cast_specs,
        out_shape=[jax.ShapeDtypeStruct(x2.shape, x2.dtype)]
        + [jax.ShapeDtypeStruct(w.shape, jnp.bfloat16) for w, _ in cast],
        scratch_shapes=stage_slot + stage_slot + [
            pltpu.VMEM((ts, D_MODEL), jnp.bfloat16),
            pltpu.VMEM((2, 4, 2, BLOCK, BLOCK), jnp.float32),
            pltpu.VMEM((n_blk * N_STACKS, 2 * BLOCK, BLOCK), jnp.float32),
            pltpu.VMEM((n_blk * N_STACKS, 2 * BLOCK, 2 * BLOCK), jnp.bfloat16),
            pltpu.VMEM(w_in.shape, jnp.bfloat16),
            pltpu.VMEM(w_out.shape, jnp.bfloat16),
            pltpu.VMEM((2, WEIGHT_STAGE_ROWS, w_in.shape[1]), jnp.float32),
            pltpu.VMEM(w_out.shape, jnp.float32),
            pltpu.VMEM((CHUNK, LANES), jnp.float32),
            pltpu.SemaphoreType.DMA((3,)),
        ],
        compiler_params=pltpu.CompilerParams(
            dimension_semantics=("arbitrary",),
            vmem_limit_bytes=VMEM_LIMIT_BYTES),
        name="mixer",
    )(x2, x2, x2, g1, w_in, vgain, wsp, bsp, sinks, rel, w_out, *[w for w, _ in cast])
    return out.reshape(batch, seq, d), cast_out


def _ffn(h, p, g2, w1, w2, wproj, wgate, gf, *, final_norm):
    rows, d = h.shape
    tm = ROW_TILE
    n_tiles = rows // tm
    return pl.pallas_call(
        functools.partial(_ffn_kernel, final_norm=final_norm, n_tiles=n_tiles),
        grid=(n_tiles + 1,),
        in_specs=[
            pl.BlockSpec((tm, d), lambda t: (jnp.minimum(t, n_tiles - 1), 0)),
            pl.BlockSpec((tm, PLE_DIM), lambda t: (jnp.maximum(t - 1, 0), 0)),
            _resident(g2.shape), _resident(w1.shape), _resident(w2.shape),
            _resident(wproj.shape), _resident(wgate.shape), _resident(gf.shape),
        ],
        out_specs=pl.BlockSpec((tm, d), lambda t: (jnp.maximum(t - 1, 0), 0)),
        out_shape=jax.ShapeDtypeStruct(h.shape, h.dtype),
        scratch_shapes=[pltpu.VMEM((tm, d), jnp.float32)],
        compiler_params=pltpu.CompilerParams(
            dimension_semantics=("arbitrary",),
            vmem_limit_bytes=VMEM_LIMIT_BYTES),
        name="ffn",
    )(h, p, g2, w1, w2, wproj, wgate, gf)


def kernel(x, p, norm1_gain, w_in, gmlp_v_gain, w_spatial, b_spatial, attn_sinks, rel_bias_table, w_out, norm2_gain, w_ff1, w_ff2, w_ple_proj, w_ple_gate, final_gain):
    batch, seq, d = x.shape
    depth = w_in.shape[0]
    bf16 = jnp.bfloat16
    h = x
    for i in range(depth):
        h, (w1, w2, wproj, wgate) = _mixer(
            h, norm1_gain[i][None], w_in[i], gmlp_v_gain[i][None],
            w_spatial[i], b_spatial[i], attn_sinks[i], rel_bias_table.T, w_out[i],
            cast=[(w_ff1[i], 1), (w_ff2[i], 0), (w_ple_proj[i], 0), (w_ple_gate[i], 0)])
        h = _ffn(h.reshape(batch * seq, d), p[i].reshape(batch * seq, PLE_DIM),
                 norm2_gain[i][None], w1, w2, wproj, wgate,
                 final_gain[None], final_norm=(i == depth - 1)).reshape(batch, seq, d)
    return h
```

```python
import functools
import math

import jax
import jax.numpy as jnp
import numpy as np
from jax import lax
from jax.experimental import pallas as pl
from jax.experimental.pallas import tpu as pltpu

D_MODEL = 1024
PLE_DIM = 256
D_GMLP = 512
GMLP_GROUPS = 4
GROUP_DIM = D_GMLP // GMLP_GROUPS
CHUNK = 128
D_ATTN = 512
HEAD_DIM = 64
N_Q_HEADS = D_ATTN // HEAD_DIM
N_KV_HEADS = 2
Q_PER_KV = N_Q_HEADS // N_KV_HEADS
WINDOW = 128
BLOCK = WINDOW
REL_BUCKETS = 32
REL_MAX_DIST = 128
D_FF = 4 * D_MODEL
KV_W = N_KV_HEADS * HEAD_DIM
EPS = 1e-6
NEG_INF = -1e30
LOG2_E = math.log2(math.e)

LANES = 128
BF16_SUBLANES = 16
N_VARIANTS = 2 * N_KV_HEADS
N_STACKS = N_Q_HEADS // 2
SEQ_TILE = 512
ROW_TILE = 1024
FF_CHUNK = 1024
FFN_ROW_PART = 256
WEIGHT_STAGE_ROWS = 256
VMEM_LIMIT_BYTES = 56 * 1024 * 1024


def _bucket_thresholds():
    max_exact = REL_BUCKETS // 2
    n = np.arange(WINDOW)
    nf = np.maximum(n, 1).astype(np.float32)
    large = max_exact + (np.log(nf / max_exact) / np.float32(math.log(REL_MAX_DIST / max_exact))
                         * (REL_BUCKETS - max_exact)).astype(np.int32)
    bucket = np.where(n < max_exact, n, np.minimum(large, REL_BUCKETS - 1))
    assert (np.diff(bucket) >= 0).all()
    return [(b, int(np.argmax(bucket >= b))) for b in range(1, REL_BUCKETS) if (bucket >= b).any()]


_BUCKET_THRESHOLDS = _bucket_thresholds()


def _rms(x, gain):
    return x * lax.rsqrt(jnp.mean(x * x, axis=-1, keepdims=True) + EPS) * gain


def _gelu_tanh(x):
    c = float(np.float32(np.sqrt(2 / np.pi)))
    poly = x * x * (-2.0 * c * 0.044715 * LOG2_E) + (-2.0 * c * LOG2_E)
    return x / (1.0 + jnp.exp2(x * poly))


def _dot(a, b):
    return jnp.dot(a, b, preferred_element_type=jnp.float32)


def _dot_nt(a, b):
    return lax.dot_general(a, b, (((1,), (1,)), ((), ())), preferred_element_type=jnp.float32)


def _build_bias(rel_ref, bias_ref):
    a = lax.broadcasted_iota(jnp.int32, (BLOCK, BLOCK), 0)
    j = lax.broadcasted_iota(jnp.int32, (BLOCK, BLOCK), 1)
    from_prev = j > a
    n = jnp.where(from_prev, BLOCK + a - j, a - j)

    def entry(b, h):
        return jnp.broadcast_to(rel_ref[h:h + 1, b:b + 1], (BLOCK, BLOCK))

    for h in range(N_Q_HEADS):
        val = entry(0, h)
        for b, thr in _BUCKET_THRESHOLDS:
            val = jnp.where(n >= thr, entry(b, h), val)
        stack = 2 * (h // Q_PER_KV) + h % 2
        half = (h % Q_PER_KV) // 2
        val = val * LOG2_E
        bias_ref[0, stack, half] = val
        bias_ref[1, stack, half] = jnp.where(from_prev, NEG_INF, val)


def _build_spatial_bias(bsp_ref, bm_ref):
    row = lax.broadcasted_iota(jnp.int32, (CHUNK, CHUNK), 0)
    padded = jnp.zeros((CHUNK, CHUNK), jnp.float32)
    for g in range(GMLP_GROUPS):
        padded = jnp.where(row == g, jnp.broadcast_to(bsp_ref[g:g + 1, :], (CHUNK, CHUNK)), padded)
    bm_ref[...] = padded.T


def _load_as_bf16(src_hbm, dst_ref, stage_ref, sem_ref, sem_base, between=None):
    n_chunks = src_hbm.shape[0] // stage_ref.shape[1]
    rows = stage_ref.shape[1]

    def copy(k):
        return pltpu.make_async_copy(src_hbm.at[pl.ds(k * rows, rows), :],
                                     stage_ref.at[k % 2], sem_ref.at[sem_base + k % 2])

    copy(0).start()
    if between is not None:
        between()
    for k in range(n_chunks):
        if k + 1 < n_chunks:
            copy(k + 1).start()
        copy(k).wait()
        dst_ref[k * rows:(k + 1) * rows, :] = stage_ref[k % 2].astype(dst_ref.dtype)


_VARIANT = ((0, 1), (2, 3))


def _project_pieces(x_ref, g1_ref, w_in_ref, vgain_ref, dst):
    u_ref, vn_ref, q_ref, kbuf, vbuf = dst
    ts = x_ref.shape[0]
    o = 2 * D_GMLP
    state = {}

    def norm():
        state["hn"] = _rms(x_ref[...], g1_ref[...]).astype(jnp.bfloat16)

    def gate_u():
        u_ref[...] = _gelu_tanh(_dot(state["hn"], w_in_ref[:, 0:D_GMLP]))

    def gate_v():
        vg = _gelu_tanh(_dot(state["hn"], w_in_ref[:, D_GMLP:2 * D_GMLP]))
        for g in range(GMLP_GROUPS):
            cols = slice(g * GROUP_DIM, (g + 1) * GROUP_DIM)
            vn_ref[:, cols] = _rms(vg[:, cols], vgain_ref[:, cols]).astype(vn_ref.dtype)

    def query():
        q = _dot(state["hn"], w_in_ref[:, o:o + D_ATTN]) * (HEAD_DIM ** -0.5 * LOG2_E)
        q_ref[...] = q.astype(q_ref.dtype)

    def key_value():
        kv = _dot(state["hn"], w_in_ref[:, o + D_ATTN:o + D_ATTN + 2 * KV_W])
        lo = lax.broadcasted_iota(jnp.int32, (ts, LANES), 1) < HEAD_DIM
        for buf, t in ((kbuf, kv[:, 0:KV_W]), (vbuf, kv[:, KV_W:2 * KV_W])):
            tr = pltpu.roll(t, HEAD_DIM, axis=1)
            buf[0, BLOCK:BLOCK + ts, :] = jnp.where(lo, t, 0.0).astype(buf.dtype)
            buf[1, BLOCK:BLOCK + ts, :] = jnp.where(lo, 0.0, tr).astype(buf.dtype)
            buf[2, BLOCK:BLOCK + ts, :] = jnp.where(lo, tr, 0.0).astype(buf.dtype)
            buf[3, BLOCK:BLOCK + ts, :] = jnp.where(lo, 0.0, t).astype(buf.dtype)

    return [norm, gate_u, gate_v, query, key_value]


def _attend_pieces(i, src, seq_start, sink_ref, bias_ref, mix_ref, s_scr, p_scr):
    _, _, q_ref, kbuf, vbuf = src
    rows = slice(i * BLOCK, (i + 1) * BLOCK)
    keys = slice(i * BLOCK, (i + 2) * BLOCK)
    first = jnp.where(seq_start, 1, 0) if i == 0 else 0
    upper = lax.broadcasted_iota(jnp.int32, (2 * BLOCK, 1), 0) < BLOCK
    query = lax.broadcasted_iota(jnp.int32, (2 * BLOCK, BLOCK), 0) % BLOCK
    from_prev = lax.broadcasted_iota(jnp.int32, (2 * BLOCK, BLOCK), 1) > query
    stacks = [(g, parity) for g in range(N_KV_HEADS) for parity in range(2)]
    state = {}

    def scores():
        for g, parity in stacks:
            qs = jnp.concatenate([q_ref[rows, 2 * g * LANES:(2 * g + 1) * LANES],
                                  q_ref[rows, (2 * g + 1) * LANES:(2 * g + 2) * LANES]], axis=0)
            s = _dot_nt(qs, kbuf[_VARIANT[g][parity], keys, :])
            s = jnp.where(from_prev, s[:, 0:BLOCK], s[:, BLOCK:2 * BLOCK])
            s_scr[N_STACKS * i + 2 * g + parity] = (
                s + bias_ref[first, 2 * g + parity].reshape(2 * BLOCK, BLOCK))

    def softmax():
        zero = jnp.zeros((2 * BLOCK, BLOCK), jnp.bfloat16)
        for g, parity in stacks:
            head = Q_PER_KV * g + parity
            sink = jnp.where(upper, sink_ref[head], sink_ref[head + 2]) * LOG2_E
            s = s_scr[N_STACKS * i + 2 * g + parity]
            m = jnp.maximum(jnp.max(s, axis=-1, keepdims=True), sink)
            e = jnp.exp2(s - m)
            denom = jnp.sum(e, axis=-1, keepdims=True) + jnp.exp2(sink - m)
            p = e.astype(jnp.bfloat16)
            p_scr[N_STACKS * i + 2 * g + parity] = jnp.concatenate(
                [jnp.where(from_prev, p, zero), jnp.where(from_prev, zero, p)], axis=1)
            state["inv", g, parity] = 1.0 / denom

    def values():
        low = lax.broadcasted_iota(jnp.int32, (2 * BLOCK, LANES), 1) < HEAD_DIM
        for g in range(N_KV_HEADS):
            pair0 = D_GMLP + 2 * g * LANES
            acc = (_dot(p_scr[N_STACKS * i + 2 * g], vbuf[_VARIANT[g][0], keys, :])
                   + _dot(p_scr[N_STACKS * i + 2 * g + 1], vbuf[_VARIANT[g][1], keys, :]))
            acc = acc * jnp.where(low, state["inv", g, 0], state["inv", g, 1])
            mix_ref[rows, pair0:pair0 + LANES] = acc[0:BLOCK].astype(mix_ref.dtype)
            mix_ref[rows, pair0 + LANES:pair0 + 2 * LANES] = acc[BLOCK:].astype(mix_ref.dtype)

    return scores, softmax, values


def _spatial_gate(src, wsp_ref, bm_ref, mix_ref):
    u_ref, vn_ref = src[0:2]
    t_idx = lax.broadcasted_iota(jnp.int32, (CHUNK, CHUNK), 0)
    s_idx = lax.broadcasted_iota(jnp.int32, (CHUNK, CHUNK), 1)
    causal = s_idx <= t_idx
    for g in range(GMLP_GROUPS):
        cols = slice(g * GROUP_DIM, (g + 1) * GROUP_DIM)
        w_g = jnp.where(causal, wsp_ref[g], 0.0).astype(jnp.bfloat16)
        b_g = bm_ref[:, g:g + 1]
        for c in range(u_ref.shape[0] // CHUNK):
            rows = slice(c * CHUNK, (c + 1) * CHUNK)
            sv = _dot(w_g, vn_ref[rows, cols]) + b_g
            mix_ref[rows, cols] = (u_ref[rows, cols] * sv).astype(mix_ref.dtype)


N_CAST = 4
SCORES_AHEAD = 2


def _mixer_kernel(x_ref, xprev_ref, g1_ref, w_in_hbm, vgain_ref, wsp_ref, bsp_ref, sink_ref,
                  rel_ref, w_out_hbm, *rest, tiles_per_seq, n_tiles):
    cast_src, o_ref = rest[0:N_CAST], rest[N_CAST]
    cast_dst, scratch = rest[N_CAST + 1:2 * N_CAST + 1], rest[2 * N_CAST + 1:]
    def slot(s):
        return tuple(ref.at[s] for ref in scratch[0:5])

    slots = (slot(0), slot(1))
    mix_ref, bias_ref, s_scr, p_scr = scratch[5:9]
    w_in_ref, w_out_ref, stage_in, stage_out, bm_ref, dma_sem = scratch[9:15]
    ts = x_ref.shape[0]
    t = pl.program_id(0)

    cur_starts = t % tiles_per_seq == 0
    prev_started = (t + tiles_per_seq - 1) % tiles_per_seq == 0
    n_blk = ts // BLOCK

    def step(dst, src):
        project = _project_pieces(x_ref, g1_ref, w_in_ref, vgain_ref, dst) if dst else []
        attend = [_attend_pieces(i, src, prev_started, sink_ref, bias_ref, mix_ref, s_scr,
                                 p_scr) for i in range(n_blk)] if src else []
        if src:
            for i in range(SCORES_AHEAD):
                attend[i][0]()
            _spatial_gate(src, wsp_ref, bm_ref, mix_ref)
        if dst:
            project.pop(0)()
        for i in range(n_blk):
            if src:
                if i + SCORES_AHEAD < n_blk:
                    attend[i + SCORES_AHEAD][0]()
                attend[i][1]()
            if dst:
                project.pop(0)()
            if src:
                attend[i][2]()
        assert not project
        if src:
            o_ref[...] = xprev_ref[...] + _dot(mix_ref[...], w_out_ref[...])
        for s, d in zip(cast_src, cast_dst):
            d[...] = s[...].astype(d.dtype)
        if dst:
            zeros = jnp.zeros((N_VARIANTS, BLOCK, LANES), dst[3].dtype)
            for k in (3, 4):
                dst[k][:, 0:BLOCK, :] = (
                    jnp.where(cur_starts, zeros, src[k][:, ts:ts + BLOCK, :]) if src else zeros)

    @pl.when(t == 0)
    def _():
        def build_tables():
            _build_bias(rel_ref, bias_ref)
            _build_spatial_bias(bsp_ref, bm_ref)

        fetch_w_out = pltpu.make_async_copy(w_out_hbm, stage_out, dma_sem.at[2])
        fetch_w_out.start()
        _load_as_bf16(w_in_hbm, w_in_ref, stage_in, dma_sem, 0, between=build_tables)
        step(slots[0], None)
        fetch_w_out.wait()
        w_out_ref[...] = stage_out[...].astype(w_out_ref.dtype)

    @pl.when(t == n_tiles)
    def _():
        step(None, slots[(n_tiles - 1) % 2])

    @pl.when((t > 0) & (t < n_tiles))
    def _():
        step(slot(t % 2), slot(1 - t % 2))


def _ffn_kernel(h_ref, p_ref, g2_ref, w1_ref, w2_ref, wproj_ref, wgate_ref, gf_ref, o_ref,
                acc_ref, *, final_norm, n_tiles):
    tm = h_ref.shape[0]
    parts = [slice(r, r + FFN_ROW_PART) for r in range(0, tm, FFN_ROW_PART)]

    def step(expand, finish):
        hidden = []

        def expand_chunk(c, hn_parts):
            cols = slice(c * FF_CHUNK, (c + 1) * FF_CHUNK)
            if c == 0:
                t = jnp.concatenate([_dot(part, w1_ref[:, cols]) for part in hn_parts], axis=0)
            else:
                t = _dot(jnp.concatenate(hn_parts, axis=0), w1_ref[:, cols])
            t = jnp.maximum(t, 0.0)
            hidden.append((t * t).astype(jnp.bfloat16))

        n_chunks = D_FF // FF_CHUNK
        assert len(parts) == n_chunks, "one finishing row part is issued per expand chunk"
        if expand:
            hn_parts = [_rms(h_ref[r, :], g2_ref[...]).astype(jnp.bfloat16) for r in parts]
        if finish:
            ple = _dot(p_ref[...].astype(jnp.bfloat16), wproj_ref[...])
        for c in range(n_chunks):
            if expand:
                expand_chunk(c, hn_parts)
            if finish:
                r = parts[c]
                z = _dot(acc_ref[r, :].astype(jnp.bfloat16), wgate_ref[...])
                out = acc_ref[r, :] + ple[r] / (1.0 + jnp.exp2(z * (-LOG2_E)))
                o_ref[r, :] = _rms(out, gf_ref[...]) if final_norm else out
        if expand:
            acc_ref[...] = h_ref[...] + _dot(jnp.concatenate(hidden, axis=1), w2_ref[...])

    t = pl.program_id(0)

    @pl.when(t == 0)
    def _():
        acc_ref[...] = jnp.zeros(acc_ref.shape, acc_ref.dtype)

    @pl.when(t < n_tiles)
    def _():
        step(True, True)

    @pl.when(t == n_tiles)
    def _():
        step(False, True)


def _resident(shape):
    return pl.BlockSpec(shape, lambda *_: (0,) * len(shape), pipeline_mode=pl.Buffered(1))


def _cast_slice_spec(arr, axis, n_tiles):
    size = arr.shape[axis]
    min_thick = BF16_SUBLANES if axis == 0 else LANES
    thick = max(size // n_tiles, min_thick)
    assert size % thick == 0
    last = size // thick - 1
    block = tuple(thick if a == axis else n for a, n in enumerate(arr.shape))
    return pl.BlockSpec(block, lambda t: tuple(jnp.minimum(t, last) if a == axis else 0
                                               for a in range(arr.ndim)))


def _mixer(x, g1, w_in, vgain, wsp, bsp, sinks, rel, w_out, cast):
    batch, seq, d = x.shape
    ts = SEQ_TILE
    n_blk = ts // BLOCK
    assert n_blk == 4, "stage-1 pieces are paired one-to-one with attention blocks"
    assert len(cast) == N_CAST
    n_tiles = batch * seq // ts
    x2 = x.reshape(batch * seq, d)
    smem = pl.BlockSpec(memory_space=pltpu.SMEM)
    hbm = pl.BlockSpec(memory_space=pl.ANY)
    cast_specs = [_cast_slice_spec(w, axis, n_tiles) for w, axis in cast]
    stage_slot = [
        pltpu.VMEM((2, ts, D_GMLP), jnp.float32),
        pltpu.VMEM((2, ts, D_GMLP), jnp.bfloat16),
        pltpu.VMEM((2, ts, D_ATTN), jnp.bfloat16),
        pltpu.VMEM((2, N_VARIANTS, BLOCK + ts, LANES), jnp.bfloat16),
        pltpu.VMEM((2, N_VARIANTS, BLOCK + ts, LANES), jnp.bfloat16),
    ]
    out, *cast_out = pl.pallas_call(
        functools.partial(_mixer_kernel, tiles_per_seq=seq // ts, n_tiles=n_tiles),
        grid=(n_tiles + 1,),
        in_specs=[
            pl.BlockSpec((ts, d), lambda t: (jnp.minimum(t, n_tiles - 1), 0)),
            pl.BlockSpec((ts, d), lambda t: (jnp.maximum(t - 1, 0), 0)),
            _resident(g1.shape), hbm, _resident(vgain.shape),
            _resident(wsp.shape), _resident(bsp.shape), smem, _resident(rel.shape),
            hbm,
        ] + cast_specs,
        out_specs=[pl.BlockSpec((ts, d), lambda t: (jnp.maximum(t - 1, 0), 0))] + cast_specs,
        out_shape=[jax.ShapeDtypeStruct(x2.shape, x2.dtype)]
        + [jax.ShapeDtypeStruct(w.shape, jnp.bfloat16) for w, _ in cast],
        scratch_shapes=stage_slot + [
            pltpu.VMEM((ts, D_MODEL), jnp.bfloat16),
            pltpu.VMEM((2, 4, 2, BLOCK, BLOCK), jnp.float32),
            pltpu.VMEM((n_blk * N_STACKS, 2 * BLOCK, BLOCK), jnp.float32),
            pltpu.VMEM((n_blk * N_STACKS, 2 * BLOCK, 2 * BLOCK), jnp.bfloat16),
            pltpu.VMEM(w_in.shape, jnp.bfloat16),
            pltpu.VMEM(w_out.shape, jnp.bfloat16),
            pltpu.VMEM((2, WEIGHT_STAGE_ROWS, w_in.shape[1]), jnp.float32),
            pltpu.VMEM(w_out.shape, jnp.float32),
            pltpu.VMEM((CHUNK, LANES), jnp.float32),
            pltpu.SemaphoreType.DMA((3,)),
        ],
        compiler_params=pltpu.CompilerParams(
            dimension_semantics=("arbitrary",),
            vmem_limit_bytes=VMEM_LIMIT_BYTES),
        name="mixer",
    )(x2, x2, g1, w_in, vgain, wsp, bsp, sinks, rel, w_out, *[w for w, _ in cast])
    return out.reshape(batch, seq, d), cast_out


def _ffn(h, p, g2, w1, w2, wproj, wgate, gf, *, final_norm):
    rows, d = h.shape
    tm = ROW_TILE
    n_tiles = rows // tm
    return pl.pallas_call(
        functools.partial(_ffn_kernel, final_norm=final_norm, n_tiles=n_tiles),
        grid=(n_tiles + 1,),
        in_specs=[
            pl.BlockSpec((tm, d), lambda t: (jnp.minimum(t, n_tiles - 1), 0)),
            pl.BlockSpec((tm, PLE_DIM), lambda t: (jnp.maximum(t - 1, 0), 0)),
            _resident(g2.shape), _resident(w1.shape), _resident(w2.shape),
            _resident(wproj.shape), _resident(wgate.shape), _resident(gf.shape),
        ],
        out_specs=pl.BlockSpec((tm, d), lambda t: (jnp.maximum(t - 1, 0), 0)),
        out_shape=jax.ShapeDtypeStruct(h.shape, h.dtype),
        scratch_shapes=[pltpu.VMEM((tm, d), jnp.float32)],
        compiler_params=pltpu.CompilerParams(
            dimension_semantics=("arbitrary",),
            vmem_limit_bytes=VMEM_LIMIT_BYTES),
        name="ffn",
    )(h, p, g2, w1, w2, wproj, wgate, gf)


def kernel(x, p, norm1_gain, w_in, gmlp_v_gain, w_spatial, b_spatial, attn_sinks, rel_bias_table, w_out, norm2_gain, w_ff1, w_ff2, w_ple_proj, w_ple_gate, final_gain):
    batch, seq, d = x.shape
    depth = w_in.shape[0]
    bf16 = jnp.bfloat16
    h = x
    for i in range(depth):
        h, (w1, w2, wproj, wgate) = _mixer(
            h, norm1_gain[i][None], w_in[i], gmlp_v_gain[i][None],
            w_spatial[i], b_spatial[i], attn_sinks[i], rel_bias_table.T, w_out[i],
            cast=[(w_ff1[i], 1), (w_ff2[i], 0), (w_ple_proj[i], 0), (w_ple_gate[i], 0)])
        h = _ffn(h.reshape(batch * seq, d), p[i].reshape(batch * seq, PLE_DIM),
                 norm2_gain[i][None], w1, w2, wproj, wgate,
                 final_gain[None], final_norm=(i == depth - 1)).reshape(batch, seq, d)
    return h
```
